```python
import jax, jax.numpy as jnp
from jax import lax
import numpy as np

D_MODEL = 2048
BATCH = 8
SEQ = 2048
DEPTH = 1

HEAD_DIM = 64
ATTN_WIDTH = D_MODEL // 2
ATTN_Q_HEADS = ATTN_WIDTH // HEAD_DIM
ATTN_KV_HEADS = 4
KV_WIDTH = ATTN_KV_HEADS * HEAD_DIM
WINDOW = 128
BLOCK = 128
RWKV_WIDTH = D_MODEL - ATTN_WIDTH
RWKV_HEADS = RWKV_WIDTH // HEAD_DIM
DECAY_LORA = 64
ICLR_LORA = 64
MIX_WIDTH = ATTN_WIDTH + RWKV_WIDTH
ATTN_SIZES = [ATTN_WIDTH, KV_WIDTH, KV_WIDTH, ATTN_WIDTH]
RWKV_SHIFT_SIZES = [RWKV_WIDTH, RWKV_WIDTH, RWKV_WIDTH, DECAY_LORA, ICLR_LORA]
RWKV_SHIFT_COLS = sum(RWKV_SHIFT_SIZES)
ATTN_COLS = sum(ATTN_SIZES)
RWKV_COLS = RWKV_SHIFT_COLS + RWKV_WIDTH
IN_COLS = ATTN_COLS + RWKV_COLS
RMS_EPS = 1e-6
GN_EPS = 64e-5
NEG_BIG = -1e30

kernel_name = "hymba_swa_sink_rwkv7_hybrid"


def _split(x, sizes):
    offs = np.cumsum(sizes)[:-1].tolist()
    return jnp.split(x, offs, axis=-1)


def _rms_norm(x, g):
    xf = x.astype(jnp.float32)
    y = xf * lax.rsqrt(jnp.mean(xf * xf, axis=-1, keepdims=True) + RMS_EPS)
    return (y * g.astype(jnp.float32)).astype(x.dtype)


def _sliding_window_attention(q, k, v, sinks):
    B, T = q.shape[0], q.shape[1]
    nb = T // BLOCK
    G = ATTN_Q_HEADS // ATTN_KV_HEADS
    qb = q.reshape(B, nb, BLOCK, ATTN_KV_HEADS, G, HEAD_DIM)
    kb = k.reshape(B, nb, BLOCK, ATTN_KV_HEADS, HEAD_DIM)
    vb = v.reshape(B, nb, BLOCK, ATTN_KV_HEADS, HEAD_DIM)
    pad = ((0, 0), (1, 0), (0, 0), (0, 0), (0, 0))
    kcat = jnp.concatenate([jnp.pad(kb, pad)[:, :-1], kb], axis=2)
    vcat = jnp.concatenate([jnp.pad(vb, pad)[:, :-1], vb], axis=2)
    scale = HEAD_DIM ** -0.5
    s = jnp.einsum('bnqhgd,bnshd->bnhgqs', qb, kcat).astype(jnp.float32) * scale
    qi = jnp.arange(BLOCK)[:, None]
    si = jnp.arange(2 * BLOCK)[None, :]
    rel = si - BLOCK - qi
    in_win = (rel <= 0) & (rel > -WINDOW)
    has_prev = (jnp.arange(nb)[:, None, None] > 0) | (si[None] >= BLOCK)
    valid = in_win[None] & has_prev
    s = jnp.where(valid[None, :, None, None], s, NEG_BIG)
    sink = sinks.astype(jnp.float32).reshape(ATTN_KV_HEADS, G)[None, None, :, :, None, None]
    sink = jnp.broadcast_to(sink, s.shape[:-1] + (1,))
    p = jax.nn.softmax(jnp.concatenate([s, sink], axis=-1), axis=-1)[..., :-1]
    o = jnp.einsum('bnhgqs,bnshd->bnqhgd', p.astype(v.dtype), vcat)
    return o.reshape(B, T, ATTN_WIDTH)


def _rwkv7_step(S, inp):
    r_t, w_t, k_t, v_t, a_t, b_t = inp
    sa = jnp.einsum('bhvk,bhk->bhv', S, a_t)
    S = S * w_t[:, :, None, :] + sa[..., None] * b_t[:, :, None, :] + v_t[..., None] * k_t[:, :, None, :]
    y = jnp.einsum('bhvk,bhk->bhv', S, r_t)
    return S, y


def _rwkv7_time_mix(p, mu, w0, w_up, a0, a_up, k_k, k_a, r_k, ln_w, ln_b):
    B, T = p.shape[0], p.shape[1]
    prev = jnp.pad(p, ((0, 0), (1, 0), (0, 0)))[:, :-1]
    p = p + (prev - p) * mu
    r, k, v, wd, ad = _split(p, RWKV_SHIFT_SIZES)
    w = -jax.nn.softplus(-(w0 + jnp.tanh(wd) @ w_up)) - 0.5
    decay = jnp.exp(-jnp.exp(w.astype(jnp.float32)))
    a = jax.nn.sigmoid(a0 + ad @ a_up)
    hs = (B, T, RWKV_HEADS, HEAD_DIM)
    kk = (k * k_k).reshape(hs).astype(jnp.float32)
    kk = kk * lax.rsqrt(jnp.maximum(jnp.sum(kk * kk, -1, keepdims=True), 1e-24))
    k = k * (1 + (a - 1) * k_a)
    rh = r.reshape(hs).astype(jnp.float32)
    kh = k.reshape(hs).astype(jnp.float32)
    vh = v.reshape(hs).astype(jnp.float32)
    ah = a.reshape(hs).astype(jnp.float32)
    xs = tuple(jnp.moveaxis(t, 1, 0) for t in (rh, decay.reshape(hs), kh, vh, -kk, kk * ah))
    S0 = jnp.zeros((B, RWKV_HEADS, HEAD_DIM, HEAD_DIM), jnp.float32)
    _, ys = lax.scan(_rwkv7_step, S0, xs)
    y = jnp.moveaxis(ys, 0, 1)
    mean = jnp.mean(y, -1, keepdims=True)
    var = jnp.mean(jnp.square(y - mean), -1, keepdims=True)
    y = ((y - mean) * lax.rsqrt(var + GN_EPS)).reshape(B, T, RWKV_WIDTH)
    y = y * ln_w.astype(jnp.float32) + ln_b.astype(jnp.float32)
    bonus = jnp.sum(rh * kh * r_k.astype(jnp.float32), -1, keepdims=True) * vh
    y = y + bonus.reshape(B, T, RWKV_WIDTH)
    return y.astype(p.dtype)


def setup_inputs(seed: int = 0) -> dict:
    key = jax.random.key(seed)
    ks = jax.random.split(key, 20)
    f = jnp.float32
    L, D = DEPTH, D_MODEL
    return {
        "x": jax.random.normal(ks[0], (BATCH, SEQ, D), f),
        "c": jax.random.normal(ks[1], (BATCH, D), f),
        "w_ada": jax.random.normal(ks[2], (L, D, 3 * D), f) * (0.5 * D ** -0.5),
        "b_ada": jax.random.normal(ks[3], (L, 3 * D), f) * 0.02,
        "pre_norm_g": 1.0 + 0.05 * jax.random.normal(ks[4], (L, D), f),
        "post_norm_g": 1.0 + 0.05 * jax.random.normal(ks[5], (L, D), f),
        "w_in": jax.random.normal(ks[6], (L, D, IN_COLS), f) * D ** -0.5,
        "w_out": jax.random.normal(ks[7], (L, MIX_WIDTH, D), f) * MIX_WIDTH ** -0.5,
        "attn_sinks": jax.random.normal(ks[8], (L, ATTN_Q_HEADS), f),
        "rwkv_mu": jax.random.uniform(ks[9], (L, RWKV_SHIFT_COLS), f),
        "rwkv_w0": jax.random.uniform(ks[10], (L, RWKV_WIDTH), f, -6.0, 0.0),
        "rwkv_w_up": jax.random.normal(ks[11], (L, DECAY_LORA, RWKV_WIDTH), f) * 0.1,
        "rwkv_a0": jax.random.normal(ks[12], (L, RWKV_WIDTH), f) * 0.1,
        "rwkv_a_up": jax.random.normal(ks[13], (L, ICLR_LORA, RWKV_WIDTH), f) * 0.1,
        "rwkv_k_k": 0.85 + 0.1 * jax.random.normal(ks[14], (L, RWKV_WIDTH), f),
        "rwkv_k_a": 1.0 + 0.1 * jax.random.normal(ks[15], (L, RWKV_WIDTH), f),
        "rwkv_r_k": jax.random.normal(ks[16], (L, RWKV_HEADS, HEAD_DIM), f) * 0.1,
        "rwkv_ln_w": 1.0 + 0.05 * jax.random.normal(ks[17], (L, RWKV_WIDTH), f),
        "rwkv_ln_b": 0.02 * jax.random.normal(ks[18], (L, RWKV_WIDTH), f),
    }


def reference(x, c, w_ada, b_ada, pre_norm_g, post_norm_g, w_in, w_out, attn_sinks,
              rwkv_mu, rwkv_w0, rwkv_w_up, rwkv_a0, rwkv_a_up, rwkv_k_k, rwkv_k_a,
              rwkv_r_k, rwkv_ln_w, rwkv_ln_b):
    B, T = x.shape[0], x.shape[1]
    for l in range(DEPTH):
        mod = jax.nn.silu(c) @ w_ada[l] + b_ada[l]
        shift, scale, gate = jnp.split(mod, 3, axis=-1)
        h = _rms_norm(x, pre_norm_g[l]) * (1 + scale[:, None]) + shift[:, None]
        proj = h @ w_in[l]
        p_attn, p_rwkv = proj[..., :ATTN_COLS], proj[..., ATTN_COLS:]
        q, ka, va, ga = _split(p_attn, ATTN_SIZES)
        y_attn = _sliding_window_attention(
            q.reshape(B, T, ATTN_Q_HEADS, HEAD_DIM),
            ka.reshape(B, T, ATTN_KV_HEADS, HEAD_DIM),
            va.reshape(B, T, ATTN_KV_HEADS, HEAD_DIM),
            attn_sinks[l]) * jax.nn.silu(ga)
        p_shift, gr = p_rwkv[..., :RWKV_SHIFT_COLS], p_rwkv[..., RWKV_SHIFT_COLS:]
        y_rwkv = _rwkv7_time_mix(p_shift, rwkv_mu[l], rwkv_w0[l], rwkv_w_up[l], rwkv_a0[l],
                                 rwkv_a_up[l], rwkv_k_k[l], rwkv_k_a[l], rwkv_r_k[l],
                                 rwkv_ln_w[l], rwkv_ln_b[l]) * jax.nn.silu(gr)
        mix = jnp.concatenate([y_attn, y_rwkv], axis=-1) @ w_out[l]
        x = x + gate[:, None] * _rms_norm(mix, post_norm_g[l])
    return x
```

```python
import functools

import jax
import jax.numpy as jnp
from jax import lax
from jax.experimental import pallas as pl
from jax.experimental.pallas import tpu as pltpu

D_MODEL = 2048
HEAD_DIM = 64
ATTN_WIDTH = 1024
ATTN_Q_HEADS = 16
KV_WIDTH = 256
WINDOW = 128
RWKV_WIDTH = 1024
LORA = 64
IN_COLS = 6784
RMS_EPS = 1e-6
GN_EPS = 64e-5
NEG_BIG = -1e30

LANES = 128
PAIRS = RWKV_WIDTH // LANES
CHUNK = 64

COL_Q = 0
COL_GA = 1024
COL_K = 2048
COL_V = 2304
COL_R = 2560
COL_RK = 3584
COL_RV = 4608
COL_RG = 5632
COL_WA = 6656
IN_COLS_PAD = 6912

VMEM_LIMIT = 56 * 1024 * 1024

F32 = jnp.float32
BF16 = jnp.bfloat16


def _dot(a, b):
    return jnp.dot(a.astype(BF16), b.astype(BF16), preferred_element_type=F32)


def _dot_nt(a, b):
    return lax.dot_general(a.astype(BF16), b.astype(BF16), (((1,), (1,)), ((), ())),
                           preferred_element_type=F32)


def _dot_tn(a, b):
    return lax.dot_general(a.astype(BF16), b.astype(BF16), (((0,), (0,)), ((), ())),
                           preferred_element_type=F32)


def _split3(x):
    hi = x.astype(BF16)
    r1 = x - hi.astype(F32)
    mid = r1.astype(BF16)
    lo = (r1 - mid.astype(F32)).astype(BF16)
    return hi, mid, lo


def _dot_exact_rhs(a, b_bf16):
    hi, mid, lo = _split3(a)
    d = lambda p: jnp.dot(p, b_bf16, preferred_element_type=F32)
    return d(hi) + (d(mid) + d(lo))


def _dot_exact_lhs(a_bf16, b):
    hi, mid, lo = _split3(b)
    d = lambda p: jnp.dot(a_bf16, p, preferred_element_type=F32)
    return d(hi) + (d(mid) + d(lo))


def _sigmoid(x):
    return 1.0 / (1.0 + jnp.exp(-x))


def _silu(x):
    return x * _sigmoid(x)


def _adaln_kernel(c_ref, w_ref, b_ref, o_ref):
    s = _silu(c_ref[...])
    s_hi, s_mid, s_lo = _split3(s)
    w_hi, w_mid, w_lo = _split3(w_ref[...])
    d = lambda p, q: jnp.dot(p, q, preferred_element_type=F32)
    acc = d(s_hi, w_hi) + (d(s_hi, w_mid) + d(s_mid, w_hi)) + (d(s_hi, w_lo) + d(s_mid, w_mid) + d(s_lo, w_hi))
    o_ref[...] = acc + b_ref[...]


def _adaln(c, w, b):
    B, D = c.shape
    N = w.shape[1]
    tn = 512
    return pl.pallas_call(
        _adaln_kernel,
        out_shape=jax.ShapeDtypeStruct((B, N), F32),
        grid=(N // tn,),
        in_specs=[pl.BlockSpec((B, D), lambda j: (0, 0)),
                  pl.BlockSpec((D, tn), lambda j: (0, j)),
                  pl.BlockSpec((1, tn), lambda j: (0, j))],
        out_specs=pl.BlockSpec((B, tn), lambda j: (0, j)),
        compiler_params=pltpu.CompilerParams(dimension_semantics=("arbitrary",),
                                             vmem_limit_bytes=VMEM_LIMIT),
        name="adaln_mod",
    )(c, w, b)


def _in_proj_kernel(x_ref, g_ref, scale_ref, shift_ref, w_ref, o_ref, h_ref):
    @pl.when(pl.program_id(1) == 0)
    def _():
        x = x_ref[...]
        inv = lax.rsqrt(jnp.mean(x * x, axis=-1, keepdims=True) + RMS_EPS)
        h = (x * inv) * g_ref[...] * (1.0 + scale_ref[0]) + shift_ref[0]
        h_ref[...] = h.astype(BF16)

    o_ref[...] = jnp.dot(h_ref[...], w_ref[...], preferred_element_type=F32).astype(o_ref.dtype)


def _in_proj(x2, g, scale, shift, w_packed, seq):
    M, D = x2.shape
    NP = w_packed.shape[1]
    tm, tn = 1024, 768
    per_b = seq // tm
    return pl.pallas_call(
        _in_proj_kernel,
        out_shape=jax.ShapeDtypeStruct((M, NP), BF16),
        grid=(M // tm, NP // tn),
        in_specs=[pl.BlockSpec((tm, D), lambda i, j: (i, 0)),
                  pl.BlockSpec((1, D), lambda i, j: (0, 0)),
                  pl.BlockSpec((1, 1, D), lambda i, j: (i // per_b, 0, 0)),
                  pl.BlockSpec((1, 1, D), lambda i, j: (i // per_b, 0, 0)),
                  pl.BlockSpec((D, tn), lambda i, j: (0, j))],
        out_specs=pl.BlockSpec((tm, tn), lambda i, j: (i, j)),
        scratch_shapes=[pltpu.VMEM((tm, D), BF16)],
        compiler_params=pltpu.CompilerParams(dimension_semantics=("parallel", "arbitrary"),
                                             vmem_limit_bytes=VMEM_LIMIT),
        name="in_proj",
    )(x2, g, scale, shift, w_packed)


def _dup_halves(slab, lane_lo):
    swapped = pltpu.roll(slab, HEAD_DIM, axis=1)
    return jnp.where(lane_lo, slab, swapped), jnp.where(lane_lo, swapped, slab)


def _attn_kernel(sink_ref, q_ref, ga_ref, k_ref, v_ref, o_ref, kprev_ref, vprev_ref):
    i = pl.program_id(1)
    blk = WINDOW
    lane_lo = lax.broadcasted_iota(jnp.int32, (blk, LANES), 1) < HEAD_DIM
    lane_lo2 = lax.broadcasted_iota(jnp.int32, (2 * blk, LANES), 1) < HEAD_DIM

    @pl.when(i == 0)
    def _():
        kprev_ref[...] = jnp.zeros_like(kprev_ref)
        vprev_ref[...] = jnp.zeros_like(vprev_ref)

    kcur, vcur = [], []
    for s in range(KV_WIDTH // LANES):
        ks = k_ref[:, s * LANES:(s + 1) * LANES].astype(F32)
        vs = v_ref[:, s * LANES:(s + 1) * LANES].astype(F32)
        kcur.extend(t.astype(BF16) for t in _dup_halves(ks, lane_lo))
        vcur.extend(t.astype(BF16) for t in _dup_halves(vs, lane_lo))

    qi = lax.broadcasted_iota(jnp.int32, (2 * blk, blk), 0) % blk
    si = lax.broadcasted_iota(jnp.int32, (2 * blk, blk), 1)
    cur_ok = si <= qi
    prev_ok = jnp.logical_and(si > qi, i > 0)
    scale = HEAD_DIM ** -0.5
    group = ATTN_Q_HEADS // (KV_WIDTH // HEAD_DIM)
    row_lo = lax.broadcasted_iota(jnp.int32, (2 * blk, 1), 0) < blk

    for p in range(PAIRS):
        g = (2 * p) // group
        q = q_ref[:, p * LANES:(p + 1) * LANES]
        zero = jnp.zeros_like(q)
        qs = jnp.concatenate([jnp.where(lane_lo, q, zero), jnp.where(lane_lo, zero, q)], axis=0)
        s_cur = jnp.where(cur_ok, _dot_nt(qs, kcur[g]) * scale, NEG_BIG)
        s_prev = jnp.where(prev_ok, _dot_nt(qs, kprev_ref[g]) * scale, NEG_BIG)
        sink = jnp.where(row_lo, sink_ref[2 * p], sink_ref[2 * p + 1])
        m = jnp.maximum(jnp.maximum(jnp.max(s_cur, axis=-1, keepdims=True),
                                    jnp.max(s_prev, axis=-1, keepdims=True)), sink)
        e_cur = jnp.exp(s_cur - m)
        e_prev = jnp.exp(s_prev - m)
        denom = (jnp.sum(e_cur, axis=-1, keepdims=True) + jnp.sum(e_prev, axis=-1, keepdims=True)
                 + jnp.exp(sink - m))
        inv = 1.0 / denom
        o2 = _dot((e_cur * inv).astype(BF16), vcur[g]) + _dot((e_prev * inv).astype(BF16), vprev_ref[g])
        o = jnp.where(lane_lo, o2[:blk], o2[blk:])
        ga = ga_ref[:, p * LANES:(p + 1) * LANES].astype(F32)
        o_ref[:, p * LANES:(p + 1) * LANES] = (o * _silu(ga)).astype(o_ref.dtype)

    for g in range(KV_WIDTH // HEAD_DIM):
        kprev_ref[g] = kcur[g]
        vprev_ref[g] = vcur[g]


def _attention(proj, sinks, batch, seq):
    M = proj.shape[0]
    nb = seq // WINDOW
    row = lambda b, i: b * nb + i
    n_kv = KV_WIDTH // HEAD_DIM
    return pl.pallas_call(
        _attn_kernel,
        out_shape=jax.ShapeDtypeStruct((M, ATTN_WIDTH), BF16),
        grid=(batch, nb),
        in_specs=[pl.BlockSpec(memory_space=pltpu.SMEM),
                  pl.BlockSpec((WINDOW, ATTN_WIDTH), lambda b, i: (row(b, i), COL_Q // ATTN_WIDTH)),
                  pl.BlockSpec((WINDOW, ATTN_WIDTH), lambda b, i: (row(b, i), COL_GA // ATTN_WIDTH)),
                  pl.BlockSpec((WINDOW, KV_WIDTH), lambda b, i: (row(b, i), COL_K // KV_WIDTH)),
                  pl.BlockSpec((WINDOW, KV_WIDTH), lambda b, i: (row(b, i), COL_V // KV_WIDTH))],
        out_specs=pl.BlockSpec((WINDOW, ATTN_WIDTH), lambda b, i: (row(b, i), 0)),
        scratch_shapes=[pltpu.VMEM((n_kv, WINDOW, LANES), BF16),
                        pltpu.VMEM((n_kv, WINDOW, LANES), BF16)],
        compiler_params=pltpu.CompilerParams(dimension_semantics=("parallel", "arbitrary"),
                                             vmem_limit_bytes=VMEM_LIMIT),
        name="swa_attn",
    )(sinks, proj, proj, proj, proj)


RWKV_TT = 256


def _rwkv_kernel(r_ref, k_ref, v_ref, g_ref, wa_ref,
                 mu_r_ref, mu_k_ref, mu_v_ref, mu_wa_ref,
                 w0_ref, wup_ref, a0_ref, aup_ref, kk_ref, ka_ref, rk_ref, lnw_ref, lnb_ref,
                 o_ref, state_ref, last_ref):
    t = pl.program_id(2)
    TT = RWKV_TT
    L = CHUNK

    @pl.when(t == 0)
    def _():
        state_ref[...] = jnp.zeros_like(state_ref)
        last_ref[...] = jnp.zeros_like(last_ref)

    row0 = lax.broadcasted_iota(jnp.int32, (TT, LANES), 0) == 0

    def shifted(ref, slot, mu_ref):
        x = ref[...].astype(F32)
        prev = jnp.where(row0, last_ref[slot, 7:8, :], pltpu.roll(x, 1, axis=0))
        last_ref[slot] = x[TT - 8:, :]
        return x + (prev - x) * mu_ref[...]

    r = shifted(r_ref, 0, mu_r_ref)
    k = shifted(k_ref, 1, mu_k_ref)
    v = shifted(v_ref, 2, mu_v_ref)
    wa = shifted(wa_ref, 3, mu_wa_ref)
    wd = wa[:, :LORA]
    ad = wa[:, LORA:]

    z = -(w0_ref[...] + _dot(jnp.tanh(wd), wup_ref[...]))
    softplus = jnp.maximum(z, 0.0) + jnp.log(1.0 + jnp.exp(-jnp.abs(z)))
    lw = -jnp.exp(-softplus - 0.5)
    a = _sigmoid(a0_ref[...] + _dot(ad, aup_ref[...]))

    lane_r = lax.broadcasted_iota(jnp.int32, (LANES, LANES), 0)
    lane_c = lax.broadcasted_iota(jnp.int32, (LANES, LANES), 1)
    head_blk = (lane_r // HEAD_DIM) == (lane_c // HEAD_DIM)
    ones_blk = jnp.where(head_blk, 1.0, 0.0).astype(BF16)

    kk = k * kk_ref[...]
    kk = kk * lax.rsqrt(jnp.maximum(_dot_exact_rhs(kk * kk, ones_blk), 1e-24))
    k = k * (1.0 + (a - 1.0) * ka_ref[...])
    av = -kk
    bv = kk * a

    tr = lax.broadcasted_iota(jnp.int32, (TT, TT), 0)
    tc = lax.broadcasted_iota(jnp.int32, (TT, TT), 1)
    tri = jnp.where(jnp.logical_and(tr // L == tc // L, tc <= tr), 1.0, 0.0).astype(BF16)
    cum = _dot_exact_lhs(tri, lw)

    e_pos = jnp.exp(cum)
    e_neg = jnp.exp(-cum)
    r_t = r * e_pos
    k_t = k * e_neg
    b_t = bv * e_neg
    a_t = av * jnp.exp(cum - lw)

    lane_lo = lax.broadcasted_iota(jnp.int32, (L, LANES), 1) < HEAD_DIM
    sr = lax.broadcasted_iota(jnp.int32, (2 * L, 2 * L), 0)
    sc = lax.broadcasted_iota(jnp.int32, (2 * L, 2 * L), 1)
    same = (sr // L) == (sc // L)
    strict = jnp.logical_and(same, sc < sr)
    incl = jnp.logical_and(same, sc <= sr)
    eye = sr == sc

    def stack(x):
        zero = jnp.zeros_like(x)
        return jnp.concatenate([jnp.where(lane_lo, x, zero), jnp.where(lane_lo, zero, x)], axis=0)

    def fold(x):
        return x[:L] + x[L:]

    H = state_ref[...]
    ys = []
    for c in range(TT // L):
        sl = slice(c * L, (c + 1) * L)
        cum_c = cum[sl]
        cum_l = cum_c[L - 1:L, :]
        to_end = jnp.exp(cum_l - cum_c)
        k_bar = k[sl] * to_end
        b_bar = bv[sl] * to_end
        v_c = v[sl]
        at_s, rt_s, v_s = stack(a_t[sl]), stack(r_t[sl]), stack(v_c)
        bt2 = jnp.concatenate([b_t[sl], b_t[sl]], axis=0)
        kt2 = jnp.concatenate([k_t[sl], k_t[sl]], axis=0)
        a_ab = jnp.where(strict, _dot_nt(at_s, bt2), 0.0)
        a_ak = jnp.where(strict, _dot_nt(at_s, kt2), 0.0)
        a_rb = jnp.where(incl, _dot_nt(rt_s, bt2), 0.0)
        a_rk = jnp.where(incl, _dot_nt(rt_s, kt2), 0.0)
        inv = jnp.where(eye, 1.0, 0.0) + a_ab
        pw = a_ab
        for _ in range(5):
            pw = _dot(pw, pw)
            inv = inv + _dot(pw, inv)
        ahat_s = _dot(inv, at_s)
        u0_s = _dot(inv, _dot(a_ak, v_s))
        ahat = fold(ahat_s)
        u0 = fold(u0_s)
        m_blk = jnp.where(head_blk, _dot_tn(b_bar, ahat), 0.0) + jnp.where(eye, jnp.exp(cum_l), 0.0)
        n_blk = jnp.where(head_blk, _dot_tn(b_bar, u0) + _dot_tn(k_bar, v_c), 0.0)
        r_hat = r_t[sl] + fold(_dot(a_rb, ahat_s))
        y0 = fold(_dot(a_rb, u0_s) + _dot(a_rk, v_s))
        ys.append(_dot(r_hat, H) + y0)
        H = _dot(m_blk, H) + n_blk
    state_ref[...] = H
    y = jnp.concatenate(ys, axis=0)

    mean_blk = jnp.where(head_blk, 1.0 / HEAD_DIM, 0.0).astype(BF16)
    mean = _dot_exact_rhs(y, mean_blk)
    yc = y - mean
    var = _dot_exact_rhs(yc * yc, mean_blk)
    yn = yc * lax.rsqrt(var + GN_EPS) * lnw_ref[...] + lnb_ref[...]
    bonus = _dot_exact_rhs(r * k * rk_ref[...], ones_blk) * v
    o_ref[...] = ((yn + bonus) * _silu(g_ref[...].astype(F32))).astype(o_ref.dtype)


def _rwkv(proj, mu, w0, w_up, a0, a_up, k_k, k_a, r_k, ln_w, ln_b, batch, seq):
    M = proj.shape[0]
    TT = RWKV_TT
    nt = seq // TT
    row = lambda b, p, t: b * nt + t
    slab = lambda col: pl.BlockSpec((TT, LANES), lambda b, p, t: (row(b, p, t), col // LANES + p))
    vec = pl.BlockSpec((1, LANES), lambda b, p, t: (0, p))
    vec0 = pl.BlockSpec((1, LANES), lambda b, p, t: (0, 0))
    lora = pl.BlockSpec((LORA, LANES), lambda b, p, t: (0, p))
    mu_r, mu_k, mu_v, mu_wa = (mu[:, :1024], mu[:, 1024:2048], mu[:, 2048:3072], mu[:, 3072:])
    return pl.pallas_call(
        _rwkv_kernel,
        out_shape=jax.ShapeDtypeStruct((M, RWKV_WIDTH), BF16),
        grid=(batch, PAIRS, nt),
        in_specs=[slab(COL_R), slab(COL_RK), slab(COL_RV), slab(COL_RG),
                  pl.BlockSpec((TT, LANES), lambda b, p, t: (row(b, p, t), COL_WA // LANES)),
                  vec, vec, vec, vec0,
                  vec, lora, vec, lora, vec, vec, vec, vec, vec],
        out_specs=pl.BlockSpec((TT, LANES), lambda b, p, t: (row(b, p, t), p)),
        scratch_shapes=[pltpu.VMEM((LANES, LANES), F32),
                        pltpu.VMEM((4, 8, LANES), F32)],
        compiler_params=pltpu.CompilerParams(dimension_semantics=("parallel", "parallel", "arbitrary"),
                                             vmem_limit_bytes=VMEM_LIMIT),
        name="rwkv7_mix",
    )(proj, proj, proj, proj, proj, mu_r, mu_k, mu_v, mu_wa,
      w0, w_up, a0, a_up, k_k, k_a, r_k, ln_w, ln_b)


def _out_proj_kernel(x_ref, ya_ref, yr_ref, wa_ref, wr_ref, g_ref, gate_ref, o_ref):
    mix = (jnp.dot(ya_ref[...], wa_ref[...], preferred_element_type=F32)
           + jnp.dot(yr_ref[...], wr_ref[...], preferred_element_type=F32))
    inv = lax.rsqrt(jnp.mean(mix * mix, axis=-1, keepdims=True) + RMS_EPS)
    o_ref[...] = x_ref[...] + gate_ref[0] * ((mix * inv) * g_ref[...])


def _out_proj(x2, ya, yr, w_a, w_r, g, gate, seq):
    M, D = x2.shape
    tm = 512
    per_b = seq // tm
    half = ya.shape[1]
    return pl.pallas_call(
        _out_proj_kernel,
        out_shape=jax.ShapeDtypeStruct((M, D), F32),
        grid=(M // tm,),
        in_specs=[pl.BlockSpec((tm, D), lambda i: (i, 0)),
                  pl.BlockSpec((tm, half), lambda i: (i, 0)),
                  pl.BlockSpec((tm, half), lambda i: (i, 0)),
                  pl.BlockSpec((half, D), lambda i: (0, 0)),
                  pl.BlockSpec((half, D), lambda i: (0, 0)),
                  pl.BlockSpec((1, D), lambda i: (0, 0)),
                  pl.BlockSpec((1, 1, D), lambda i: (i // per_b, 0, 0))],
        out_specs=pl.BlockSpec((tm, D), lambda i: (i, 0)),
        compiler_params=pltpu.CompilerParams(dimension_semantics=("parallel",),
                                             vmem_limit_bytes=VMEM_LIMIT),
        name="out_proj",
    )(x2, ya, yr, w_a, w_r, g, gate)


def _pack_w_in(w):
    q, ka, va, ga = w[:, 0:1024], w[:, 1024:1280], w[:, 1280:1536], w[:, 1536:2560]
    rkv, wa, gr = w[:, 2560:5632], w[:, 5632:5760], w[:, 5760:6784]
    pad = jnp.zeros((w.shape[0], IN_COLS_PAD - IN_COLS), w.dtype)
    return jnp.concatenate([q, ga, ka, va, rkv, gr, wa, pad], axis=1).astype(BF16)


def kernel(x, c, w_ada, b_ada, pre_norm_g, post_norm_g, w_in, w_out, attn_sinks, rwkv_mu, rwkv_w0,
           rwkv_w_up, rwkv_a0, rwkv_a_up, rwkv_k_k, rwkv_k_a, rwkv_r_k, rwkv_ln_w, rwkv_ln_b):
    B, T, D = x.shape
    depth = w_ada.shape[0]
    x2 = x.reshape(B * T, D)
    for l in range(depth):
        mod = _adaln(c, w_ada[l], b_ada[l][None, :])
        shift, scale, gate = (mod[:, i * D:(i + 1) * D].reshape(B, 1, D) for i in range(3))
        proj = _in_proj(x2, pre_norm_g[l][None, :], scale, shift, _pack_w_in(w_in[l]), T)
        y_attn = _attention(proj, attn_sinks[l], B, T)
        y_rwkv = _rwkv(proj, rwkv_mu[l][None, :], rwkv_w0[l][None, :], rwkv_w_up[l].astype(BF16),
                       rwkv_a0[l][None, :], rwkv_a_up[l].astype(BF16), rwkv_k_k[l][None, :],
                       rwkv_k_a[l][None, :], rwkv_r_k[l].reshape(1, RWKV_WIDTH),
                       rwkv_ln_w[l][None, :], rwkv_ln_b[l][None, :], B, T)
        w_o = w_out[l].astype(BF16)
        x2 = _out_proj(x2, y_attn, y_rwkv, w_o[:ATTN_WIDTH], w_o[ATTN_WIDTH:],
                       post_norm_g[l][None, :], gate, T)
    return x2.reshape(B, T, D)
```

```python
import functools

import jax
import jax.numpy as jnp
from jax import lax
from jax.experimental import pallas as pl
from jax.experimental.pallas import tpu as pltpu

D_MODEL = 2048
HEAD_DIM = 64
ATTN_WIDTH = 1024
ATTN_Q_HEADS = 16
KV_WIDTH = 256
WINDOW = 128
RWKV_WIDTH = 1024
LORA = 64
IN_COLS = 6784
RMS_EPS = 1e-6
GN_EPS = 64e-5
NEG_BIG = -1e30

LANES = 128
PAIRS = RWKV_WIDTH // LANES
CHUNK = 64

COL_Q = 0
COL_GA = 1024
COL_K = 2048
COL_V = 2304
COL_R = 2560
COL_RK = 3584
COL_RV = 4608
COL_RG = 5632
COL_WA = 6656
IN_COLS_PAD = 6912

VMEM_LIMIT = 56 * 1024 * 1024

F32 = jnp.float32
BF16 = jnp.bfloat16


def _dot(a, b):
    return jnp.dot(a.astype(BF16), b.astype(BF16), preferred_element_type=F32)


def _dot_nt(a, b):
    return lax.dot_general(a.astype(BF16), b.astype(BF16), (((1,), (1,)), ((), ())),
                           preferred_element_type=F32)


def _dot_tn(a, b):
    return lax.dot_general(a.astype(BF16), b.astype(BF16), (((0,), (0,)), ((), ())),
                           preferred_element_type=F32)


def _split3(x):
    hi = x.astype(BF16)
    r1 = x - hi.astype(F32)
    mid = r1.astype(BF16)
    lo = (r1 - mid.astype(F32)).astype(BF16)
    return hi, mid, lo


def _split2(x):
    hi = x.astype(BF16)
    return hi, (x - hi.astype(F32)).astype(BF16)


def _dot_exact_rhs(a, b_bf16):
    hi, lo = _split2(a)
    return jnp.dot(hi, b_bf16, preferred_element_type=F32) + jnp.dot(lo, b_bf16, preferred_element_type=F32)


def _dot_exact_lhs(a_bf16, b):
    hi, lo = _split2(b)
    return jnp.dot(a_bf16, hi, preferred_element_type=F32) + jnp.dot(a_bf16, lo, preferred_element_type=F32)


def _sigmoid(x):
    return 1.0 / (1.0 + jnp.exp(-x))


def _silu(x):
    return x * _sigmoid(x)


def _adaln_kernel(c_ref, w_ref, b_ref, o_ref):
    s = _silu(c_ref[...])
    s_hi, s_mid, s_lo = _split3(s)
    w_hi, w_mid, w_lo = _split3(w_ref[...])
    d = lambda p, q: jnp.dot(p, q, preferred_element_type=F32)
    acc = d(s_hi, w_hi) + (d(s_hi, w_mid) + d(s_mid, w_hi)) + (d(s_hi, w_lo) + d(s_mid, w_mid) + d(s_lo, w_hi))
    o_ref[...] = acc + b_ref[...]


def _adaln(c, w, b):
    B, D = c.shape
    N = w.shape[1]
    tn = 512
    return pl.pallas_call(
        _adaln_kernel,
        out_shape=jax.ShapeDtypeStruct((B, N), F32),
        grid=(N // tn,),
        in_specs=[pl.BlockSpec((B, D), lambda j: (0, 0)),
                  pl.BlockSpec((D, tn), lambda j: (0, j)),
                  pl.BlockSpec((1, tn), lambda j: (0, j))],
        out_specs=pl.BlockSpec((B, tn), lambda j: (0, j)),
        compiler_params=pltpu.CompilerParams(dimension_semantics=("arbitrary",),
                                             vmem_limit_bytes=VMEM_LIMIT),
        name="adaln_mod",
    )(c, w, b)


def _in_proj_kernel(x_ref, g_ref, scale_ref, shift_ref, w_ref, o_ref, h_ref):
    @pl.when(pl.program_id(1) == 0)
    def _():
        x = x_ref[...]
        inv = lax.rsqrt(jnp.mean(x * x, axis=-1, keepdims=True) + RMS_EPS)
        h = (x * inv) * g_ref[...] * (1.0 + scale_ref[0]) + shift_ref[0]
        h_ref[...] = h.astype(BF16)

    o_ref[...] = jnp.dot(h_ref[...], w_ref[...], preferred_element_type=F32).astype(o_ref.dtype)


def _in_proj(x2, g, scale, shift, w_packed, seq):
    M, D = x2.shape
    NP = w_packed.shape[1]
    tm, tn = 1024, 768
    per_b = seq // tm
    return pl.pallas_call(
        _in_proj_kernel,
        out_shape=jax.ShapeDtypeStruct((M, NP), BF16),
        grid=(M // tm, NP // tn),
        in_specs=[pl.BlockSpec((tm, D), lambda i, j: (i, 0)),
                  pl.BlockSpec((1, D), lambda i, j: (0, 0)),
                  pl.BlockSpec((1, 1, D), lambda i, j: (i // per_b, 0, 0)),
                  pl.BlockSpec((1, 1, D), lambda i, j: (i // per_b, 0, 0)),
                  pl.BlockSpec((D, tn), lambda i, j: (0, j))],
        out_specs=pl.BlockSpec((tm, tn), lambda i, j: (i, j)),
        scratch_shapes=[pltpu.VMEM((tm, D), BF16)],
        compiler_params=pltpu.CompilerParams(dimension_semantics=("parallel", "arbitrary"),
                                             vmem_limit_bytes=VMEM_LIMIT),
        name="in_proj",
    )(x2, g, scale, shift, w_packed)


def _dup_halves(slab, lane_lo):
    swapped = pltpu.roll(slab, HEAD_DIM, axis=1)
    return jnp.where(lane_lo, slab, swapped), jnp.where(lane_lo, swapped, slab)


def _attn_kernel(sink_ref, q_ref, ga_ref, k_ref, v_ref, o_ref, kprev_ref, vprev_ref):
    i = pl.program_id(1)
    blk = WINDOW
    lane_lo = lax.broadcasted_iota(jnp.int32, (blk, LANES), 1) < HEAD_DIM
    lane_lo2 = lax.broadcasted_iota(jnp.int32, (2 * blk, LANES), 1) < HEAD_DIM

    @pl.when(i == 0)
    def _():
        kprev_ref[...] = jnp.zeros_like(kprev_ref)
        vprev_ref[...] = jnp.zeros_like(vprev_ref)

    kcur, vcur = [], []
    for s in range(KV_WIDTH // LANES):
        ks = k_ref[:, s * LANES:(s + 1) * LANES].astype(F32)
        vs = v_ref[:, s * LANES:(s + 1) * LANES].astype(F32)
        kcur.extend(t.astype(BF16) for t in _dup_halves(ks, lane_lo))
        vcur.extend(t.astype(BF16) for t in _dup_halves(vs, lane_lo))

    qi = lax.broadcasted_iota(jnp.int32, (2 * blk, blk), 0) % blk
    si = lax.broadcasted_iota(jnp.int32, (2 * blk, blk), 1)
    cur_ok = si <= qi
    prev_ok = jnp.logical_and(si > qi, i > 0)
    scale = HEAD_DIM ** -0.5
    group = ATTN_Q_HEADS // (KV_WIDTH // HEAD_DIM)
    row_lo = lax.broadcasted_iota(jnp.int32, (2 * blk, 1), 0) < blk

    for p in range(PAIRS):
        g = (2 * p) // group
        q = q_ref[:, p * LANES:(p + 1) * LANES]
        zero = jnp.zeros_like(q)
        qs = jnp.concatenate([jnp.where(lane_lo, q, zero), jnp.where(lane_lo, zero, q)], axis=0)
        s_cur = jnp.where(cur_ok, _dot_nt(qs, kcur[g]) * scale, NEG_BIG)
        s_prev = jnp.where(prev_ok, _dot_nt(qs, kprev_ref[g]) * scale, NEG_BIG)
        sink = jnp.where(row_lo, sink_ref[2 * p], sink_ref[2 * p + 1])
        m = jnp.maximum(jnp.maximum(jnp.max(s_cur, axis=-1, keepdims=True),
                                    jnp.max(s_prev, axis=-1, keepdims=True)), sink)
        e_cur = jnp.exp(s_cur - m)
        e_prev = jnp.exp(s_prev - m)
        denom = (jnp.sum(e_cur, axis=-1, keepdims=True) + jnp.sum(e_prev, axis=-1, keepdims=True)
                 + jnp.exp(sink - m))
        inv = 1.0 / denom
        o2 = _dot((e_cur * inv).astype(BF16), vcur[g]) + _dot((e_prev * inv).astype(BF16), vprev_ref[g])
        o = jnp.where(lane_lo, o2[:blk], o2[blk:])
        ga = ga_ref[:, p * LANES:(p + 1) * LANES].astype(F32)
        o_ref[:, p * LANES:(p + 1) * LANES] = (o * _silu(ga)).astype(o_ref.dtype)

    for g in range(KV_WIDTH // HEAD_DIM):
        kprev_ref[g] = kcur[g]
        vprev_ref[g] = vcur[g]


def _attention(proj, sinks, batch, seq):
    M = proj.shape[0]
    nb = seq // WINDOW
    row = lambda b, i: b * nb + i
    n_kv = KV_WIDTH // HEAD_DIM
    return pl.pallas_call(
        _attn_kernel,
        out_shape=jax.ShapeDtypeStruct((M, ATTN_WIDTH), BF16),
        grid=(batch, nb),
        in_specs=[pl.BlockSpec(memory_space=pltpu.SMEM),
                  pl.BlockSpec((WINDOW, ATTN_WIDTH), lambda b, i: (row(b, i), COL_Q // ATTN_WIDTH)),
                  pl.BlockSpec((WINDOW, ATTN_WIDTH), lambda b, i: (row(b, i), COL_GA // ATTN_WIDTH)),
                  pl.BlockSpec((WINDOW, KV_WIDTH), lambda b, i: (row(b, i), COL_K // KV_WIDTH)),
                  pl.BlockSpec((WINDOW, KV_WIDTH), lambda b, i: (row(b, i), COL_V // KV_WIDTH))],
        out_specs=pl.BlockSpec((WINDOW, ATTN_WIDTH), lambda b, i: (row(b, i), 0)),
        scratch_shapes=[pltpu.VMEM((n_kv, WINDOW, LANES), BF16),
                        pltpu.VMEM((n_kv, WINDOW, LANES), BF16)],
        compiler_params=pltpu.CompilerParams(dimension_semantics=("parallel", "arbitrary"),
                                             vmem_limit_bytes=VMEM_LIMIT),
        name="swa_attn",
    )(sinks, proj, proj, proj, proj)


RWKV_TT = 512


def _rwkv_kernel(r_ref, k_ref, v_ref, g_ref, wa_ref,
                 mu_r_ref, mu_k_ref, mu_v_ref, mu_wa_ref,
                 w0_ref, wup_ref, a0_ref, aup_ref, kk_ref, ka_ref, rk_ref, lnw_ref, lnb_ref,
                 o_ref, state_ref, last_ref):
    t = pl.program_id(2)
    TT = RWKV_TT
    L = CHUNK

    @pl.when(t == 0)
    def _():
        state_ref[...] = jnp.zeros_like(state_ref)
        last_ref[...] = jnp.zeros_like(last_ref)

    row0 = lax.broadcasted_iota(jnp.int32, (TT, LANES), 0) == 0

    def shifted(ref, slot, mu_ref):
        x = ref[...].astype(F32)
        prev = jnp.where(row0, last_ref[slot, 7:8, :], pltpu.roll(x, 1, axis=0))
        last_ref[slot] = x[TT - 8:, :]
        return x + (prev - x) * mu_ref[...]

    r = shifted(r_ref, 0, mu_r_ref)
    k = shifted(k_ref, 1, mu_k_ref)
    v = shifted(v_ref, 2, mu_v_ref)
    wa = shifted(wa_ref, 3, mu_wa_ref)
    wd = wa[:, :LORA]
    ad = wa[:, LORA:]

    z = -(w0_ref[...] + _dot(jnp.tanh(wd), wup_ref[...]))
    softplus = jnp.maximum(z, 0.0) + jnp.log(1.0 + jnp.exp(-jnp.abs(z)))
    lw = -jnp.exp(-softplus - 0.5)
    a = _sigmoid(a0_ref[...] + _dot(ad, aup_ref[...]))

    lane_r = lax.broadcasted_iota(jnp.int32, (LANES, LANES), 0)
    lane_c = lax.broadcasted_iota(jnp.int32, (LANES, LANES), 1)
    head_blk = (lane_r // HEAD_DIM) == (lane_c // HEAD_DIM)
    ones_blk = jnp.where(head_blk, 1.0, 0.0).astype(BF16)

    kk = k * kk_ref[...]
    kk = kk * lax.rsqrt(jnp.maximum(_dot_exact_rhs(kk * kk, ones_blk), 1e-24))
    k = k * (1.0 + (a - 1.0) * ka_ref[...])
    av = -kk
    bv = kk * a

    TB = 4 * L
    tr = lax.broadcasted_iota(jnp.int32, (TB, TB), 0)
    tc = lax.broadcasted_iota(jnp.int32, (TB, TB), 1)
    tri = jnp.where(jnp.logical_and(tr // L == tc // L, tc <= tr), 1.0, 0.0).astype(BF16)
    cum = jnp.concatenate([_dot_exact_lhs(tri, lw[i * TB:(i + 1) * TB]) for i in range(TT // TB)], axis=0)

    e_pos = jnp.exp(cum)
    e_neg = jnp.exp(-cum)
    r_t = r * e_pos
    k_t = k * e_neg
    b_t = bv * e_neg
    a_t = av * jnp.exp(cum - lw)

    lane_lo = lax.broadcasted_iota(jnp.int32, (L, LANES), 1) < HEAD_DIM
    sr = lax.broadcasted_iota(jnp.int32, (2 * L, 2 * L), 0)
    sc = lax.broadcasted_iota(jnp.int32, (2 * L, 2 * L), 1)
    same = (sr // L) == (sc // L)
    strict = jnp.logical_and(same, sc < sr)
    incl = jnp.logical_and(same, sc <= sr)
    eye = sr == sc

    def stack(x):
        zero = jnp.zeros_like(x)
        return jnp.concatenate([jnp.where(lane_lo, x, zero), jnp.where(lane_lo, zero, x)], axis=0)

    def fold(x):
        return x[:L] + x[L:]

    chunks = range(TT // L)
    W2 = 2 * L
    cs = [slice(c * L, (c + 1) * L) for c in chunks]
    cum_l = [cum[s][L - 1:L, :] for s in cs]
    to_end = [jnp.exp(cum_l[c] - cum[cs[c]]) for c in chunks]
    at_s = [stack(a_t[s]) for s in cs]
    v_s = [stack(v[s]) for s in cs]
    gram = [_dot_nt(jnp.concatenate([at_s[c], stack(r_t[cs[c]])], axis=0),
                    jnp.concatenate([b_t[cs[c]]] * 2 + [k_t[cs[c]]] * 2, axis=0)) for c in chunks]
    a_ab = [jnp.where(strict, gram[c][:W2, :W2], 0.0) for c in chunks]
    a_ak = [jnp.where(strict, gram[c][:W2, W2:], 0.0) for c in chunks]
    a_rb = [jnp.where(incl, gram[c][W2:, :W2], 0.0) for c in chunks]
    a_rk = [jnp.where(incl, gram[c][W2:, W2:], 0.0) for c in chunks]
    inv = [jnp.where(eye, 1.0, 0.0) + a_ab[c] for c in chunks]
    pw = [_dot(a_ab[c], a_ab[c]) for c in chunks]
    w_s = [_dot(a_ak[c], v_s[c]) for c in chunks]
    for _ in range(4):
        both = [_dot(pw[c], jnp.concatenate([pw[c], inv[c]], axis=1)) for c in chunks]
        pw = [both[c][:, :W2] for c in chunks]
        inv = [inv[c] + both[c][:, W2:] for c in chunks]
    inv = [inv[c] + _dot(pw[c], inv[c]) for c in chunks]
    sol = [_dot(inv[c], jnp.concatenate([at_s[c], w_s[c]], axis=1)) for c in chunks]
    au = [jnp.concatenate([fold(sol[c][:, :LANES]), fold(sol[c][:, LANES:])], axis=1) for c in chunks]
    bmn = [_dot_tn(bv[cs[c]] * to_end[c], au[c]) for c in chunks]
    kv = [_dot_tn(k[cs[c]] * to_end[c], v[cs[c]]) for c in chunks]
    m_blk = [jnp.where(head_blk, bmn[c][:, :LANES], 0.0) + jnp.where(eye, jnp.exp(cum_l[c]), 0.0) for c in chunks]
    n_blk = [jnp.where(head_blk, bmn[c][:, LANES:] + kv[c], 0.0) for c in chunks]
    rb = [_dot(a_rb[c], sol[c]) for c in chunks]
    rkv = [_dot(a_rk[c], v_s[c]) for c in chunks]
    r_hat = [r_t[cs[c]] + fold(rb[c][:, :LANES]) for c in chunks]
    y0 = [fold(rb[c][:, LANES:] + rkv[c]) for c in chunks]

    H = state_ref[...]
    ys = []
    for c in chunks:
        ys.append(_dot(r_hat[c], H) + y0[c])
        H = _dot(m_blk[c], H) + n_blk[c]
    state_ref[...] = H
    y = jnp.concatenate(ys, axis=0)

    mean_blk = jnp.where(head_blk, 1.0 / HEAD_DIM, 0.0).astype(BF16)
    mean = _dot_exact_rhs(y, mean_blk)
    yc = y - mean
    var = _dot_exact_rhs(yc * yc, mean_blk)
    yn = yc * lax.rsqrt(var + GN_EPS) * lnw_ref[...] + lnb_ref[...]
    bonus = _dot_exact_rhs(r * k * rk_ref[...], ones_blk) * v
    o_ref[...] = ((yn + bonus) * _silu(g_ref[...].astype(F32))).astype(o_ref.dtype)


def _rwkv(proj, mu, w0, w_up, a0, a_up, k_k, k_a, r_k, ln_w, ln_b, batch, seq):
    M = proj.shape[0]
    TT = RWKV_TT
    nt = seq // TT
    row = lambda b, p, t: b * nt + t
    slab = lambda col: pl.BlockSpec((TT, LANES), lambda b, p, t: (row(b, p, t), col // LANES + p))
    vec = pl.BlockSpec((1, LANES), lambda b, p, t: (0, p))
    vec0 = pl.BlockSpec((1, LANES), lambda b, p, t: (0, 0))
    lora = pl.BlockSpec((LORA, LANES), lambda b, p, t: (0, p))
    mu_r, mu_k, mu_v, mu_wa = (mu[:, :1024], mu[:, 1024:2048], mu[:, 2048:3072], mu[:, 3072:])
    return pl.pallas_call(
        _rwkv_kernel,
        out_shape=jax.ShapeDtypeStruct((M, RWKV_WIDTH), BF16),
        grid=(batch, PAIRS, nt),
        in_specs=[slab(COL_R), slab(COL_RK), slab(COL_RV), slab(COL_RG),
                  pl.BlockSpec((TT, LANES), lambda b, p, t: (row(b, p, t), COL_WA // LANES)),
                  vec, vec, vec, vec0,
                  vec, lora, vec, lora, vec, vec, vec, vec, vec],
        out_specs=pl.BlockSpec((TT, LANES), lambda b, p, t: (row(b, p, t), p)),
        scratch_shapes=[pltpu.VMEM((LANES, LANES), F32),
                        pltpu.VMEM((4, 8, LANES), F32)],
        compiler_params=pltpu.CompilerParams(dimension_semantics=("parallel", "parallel", "arbitrary"),
                                             vmem_limit_bytes=VMEM_LIMIT),
        name="rwkv7_mix",
    )(proj, proj, proj, proj, proj, mu_r, mu_k, mu_v, mu_wa,
      w0, w_up, a0, a_up, k_k, k_a, r_k, ln_w, ln_b)


def _out_proj_kernel(x_ref, ya_ref, yr_ref, wa_ref, wr_ref, g_ref, gate_ref, o_ref):
    mix = (jnp.dot(ya_ref[...], wa_ref[...], preferred_element_type=F32)
           + jnp.dot(yr_ref[...], wr_ref[...], preferred_element_type=F32))
    inv = lax.rsqrt(jnp.mean(mix * mix, axis=-1, keepdims=True) + RMS_EPS)
    o_ref[...] = x_ref[...] + gate_ref[0] * ((mix * inv) * g_ref[...])


def _out_proj(x2, ya, yr, w_a, w_r, g, gate, seq):
    M, D = x2.shape
    tm = 512
    per_b = seq // tm
    half = ya.shape[1]
    return pl.pallas_call(
        _out_proj_kernel,
        out_shape=jax.ShapeDtypeStruct((M, D), F32),
        grid=(M // tm,),
        in_specs=[pl.BlockSpec((tm, D), lambda i: (i, 0)),
                  pl.BlockSpec((tm, half), lambda i: (i, 0)),
                  pl.BlockSpec((tm, half), lambda i: (i, 0)),
                  pl.BlockSpec((half, D), lambda i: (0, 0)),
                  pl.BlockSpec((half, D), lambda i: (0, 0)),
                  pl.BlockSpec((1, D), lambda i: (0, 0)),
                  pl.BlockSpec((1, 1, D), lambda i: (i // per_b, 0, 0))],
        out_specs=pl.BlockSpec((tm, D), lambda i: (i, 0)),
        compiler_params=pltpu.CompilerParams(dimension_semantics=("parallel",),
                                             vmem_limit_bytes=VMEM_LIMIT),
        name="out_proj",
    )(x2, ya, yr, w_a, w_r, g, gate)


def _pack_w_in(w):
    q, ka, va, ga = w[:, 0:1024], w[:, 1024:1280], w[:, 1280:1536], w[:, 1536:2560]
    rkv, wa, gr = w[:, 2560:5632], w[:, 5632:5760], w[:, 5760:6784]
    pad = jnp.zeros((w.shape[0], IN_COLS_PAD - IN_COLS), w.dtype)
    return jnp.concatenate([q, ga, ka, va, rkv, gr, wa, pad], axis=1).astype(BF16)


def kernel(x, c, w_ada, b_ada, pre_norm_g, post_norm_g, w_in, w_out, attn_sinks, rwkv_mu, rwkv_w0,
           rwkv_w_up, rwkv_a0, rwkv_a_up, rwkv_k_k, rwkv_k_a, rwkv_r_k, rwkv_ln_w, rwkv_ln_b):
    B, T, D = x.shape
    depth = w_ada.shape[0]
    x2 = x.reshape(B * T, D)
    for l in range(depth):
        mod = _adaln(c, w_ada[l], b_ada[l][None, :])
        shift, scale, gate = (mod[:, i * D:(i + 1) * D].reshape(B, 1, D) for i in range(3))
        proj = _in_proj(x2, pre_norm_g[l][None, :], scale, shift, _pack_w_in(w_in[l]), T)
        y_attn = _attention(proj, attn_sinks[l], B, T)
        y_rwkv = _rwkv(proj, rwkv_mu[l][None, :], rwkv_w0[l][None, :], rwkv_w_up[l].astype(BF16),
                       rwkv_a0[l][None, :], rwkv_a_up[l].astype(BF16), rwkv_k_k[l][None, :],
                       rwkv_k_a[l][None, :], rwkv_r_k[l].reshape(1, RWKV_WIDTH),
                       rwkv_ln_w[l][None, :], rwkv_ln_b[l][None, :], B, T)
        w_o = w_out[l].astype(BF16)
        x2 = _out_proj(x2, y_attn, y_rwkv, w_o[:ATTN_WIDTH], w_o[ATTN_WIDTH:],
                       post_norm_g[l][None, :], gate, T)
    return x2.reshape(B, T, D)
```

```python
import functools

import jax
import jax.numpy as jnp
from jax import lax
from jax.experimental import pallas as pl
from jax.experimental.pallas import tpu as pltpu

D_MODEL = 2048
HEAD_DIM = 64
ATTN_WIDTH = 1024
ATTN_Q_HEADS = 16
KV_WIDTH = 256
WINDOW = 128
RWKV_WIDTH = 1024
LORA = 64
IN_COLS = 6784
RMS_EPS = 1e-6
GN_EPS = 64e-5
NEG_BIG = -1e30

LANES = 128
PAIRS = RWKV_WIDTH // LANES
CHUNK = 64

COL_Q = 0
COL_GA = 1024
COL_K = 2048
COL_V = 2304
COL_R = 2560
COL_RK = 3584
COL_RV = 4608
COL_RG = 5632
COL_WA = 6656
IN_COLS_PAD = 6912

VMEM_LIMIT = 56 * 1024 * 1024

F32 = jnp.float32
BF16 = jnp.bfloat16


def _dot(a, b):
    return jnp.dot(a.astype(BF16), b.astype(BF16), preferred_element_type=F32)


def _dot_nt(a, b):
    return lax.dot_general(a.astype(BF16), b.astype(BF16), (((1,), (1,)), ((), ())),
                           preferred_element_type=F32)


def _dot_tn(a, b):
    return lax.dot_general(a.astype(BF16), b.astype(BF16), (((0,), (0,)), ((), ())),
                           preferred_element_type=F32)


def _split3(x):
    hi = x.astype(BF16)
    r1 = x - hi.astype(F32)
    mid = r1.astype(BF16)
    lo = (r1 - mid.astype(F32)).astype(BF16)
    return hi, mid, lo


def _split2(x):
    hi = x.astype(BF16)
    return hi, (x - hi.astype(F32)).astype(BF16)


def _dot_exact_rhs(a, b_bf16):
    hi, lo = _split2(a)
    return jnp.dot(jnp.concatenate([hi, lo], axis=1), jnp.concatenate([b_bf16, b_bf16], axis=0),
                   preferred_element_type=F32)


def _dot_exact_lhs(a_bf16, b):
    hi, lo = _split2(b)
    return jnp.dot(a_bf16, hi, preferred_element_type=F32) + jnp.dot(a_bf16, lo, preferred_element_type=F32)


def _sigmoid(x):
    return 1.0 / (1.0 + jnp.exp(-x))


def _silu(x):
    return x * _sigmoid(x)


def _adaln_kernel(c_ref, w_ref, b_ref, o_ref):
    s = _silu(c_ref[...])
    s_hi, s_mid, s_lo = _split3(s)
    w_hi, w_mid, w_lo = _split3(w_ref[...])
    d = lambda p, q: jnp.dot(p, q, preferred_element_type=F32)
    acc = d(s_hi, w_hi) + (d(s_hi, w_mid) + d(s_mid, w_hi)) + (d(s_hi, w_lo) + d(s_mid, w_mid) + d(s_lo, w_hi))
    o_ref[...] = acc + b_ref[...]


def _adaln(c, w, b):
    B, D = c.shape
    N = w.shape[1]
    tn = 512
    return pl.pallas_call(
        _adaln_kernel,
        out_shape=jax.ShapeDtypeStruct((B, N), F32),
        grid=(N // tn,),
        in_specs=[pl.BlockSpec((B, D), lambda j: (0, 0)),
                  pl.BlockSpec((D, tn), lambda j: (0, j)),
                  pl.BlockSpec((1, tn), lambda j: (0, j))],
        out_specs=pl.BlockSpec((B, tn), lambda j: (0, j)),
        compiler_params=pltpu.CompilerParams(dimension_semantics=("arbitrary",),
                                             vmem_limit_bytes=VMEM_LIMIT),
        name="adaln_mod",
    )(c, w, b)


def _in_proj_kernel(x_ref, g_ref, scale_ref, shift_ref, w_ref, o_ref, h_ref):
    @pl.when(pl.program_id(1) == 0)
    def _():
        x = x_ref[...]
        inv = lax.rsqrt(jnp.mean(x * x, axis=-1, keepdims=True) + RMS_EPS)
        h = (x * inv) * g_ref[...] * (1.0 + scale_ref[0]) + shift_ref[0]
        h_ref[...] = h.astype(BF16)

    o_ref[...] = jnp.dot(h_ref[...], w_ref[...], preferred_element_type=F32).astype(o_ref.dtype)


def _in_proj(x2, g, scale, shift, w_packed, seq):
    M, D = x2.shape
    NP = w_packed.shape[1]
    tm, tn = 1024, 1152
    per_b = seq // tm
    return pl.pallas_call(
        _in_proj_kernel,
        out_shape=jax.ShapeDtypeStruct((M, NP), BF16),
        grid=(M // tm, NP // tn),
        in_specs=[pl.BlockSpec((tm, D), lambda i, j: (i, 0)),
                  pl.BlockSpec((1, D), lambda i, j: (0, 0)),
                  pl.BlockSpec((1, 1, D), lambda i, j: (i // per_b, 0, 0)),
                  pl.BlockSpec((1, 1, D), lambda i, j: (i // per_b, 0, 0)),
                  pl.BlockSpec((D, tn), lambda i, j: (0, j))],
        out_specs=pl.BlockSpec((tm, tn), lambda i, j: (i, j)),
        scratch_shapes=[pltpu.VMEM((tm, D), BF16)],
        compiler_params=pltpu.CompilerParams(dimension_semantics=("parallel", "arbitrary"),
                                             vmem_limit_bytes=VMEM_LIMIT),
        name="in_proj",
    )(x2, g, scale, shift, w_packed)


ATTN_GROUP = 4


def _dup_halves(slab, lane_lo):
    swapped = pltpu.roll(slab, HEAD_DIM, axis=1)
    return jnp.where(lane_lo, slab, swapped), jnp.where(lane_lo, swapped, slab)


def _attn_kernel(sink_ref, q_ref, ga_ref, k_ref, v_ref, o_ref, kprev_ref, vprev_ref):
    i = pl.program_id(1)
    blk = WINDOW
    lane_lo = lax.broadcasted_iota(jnp.int32, (blk, LANES), 1) < HEAD_DIM

    @pl.when(i == 0)
    def _():
        kprev_ref[...] = jnp.zeros_like(kprev_ref)
        vprev_ref[...] = jnp.zeros_like(vprev_ref)

    ones = jnp.ones((blk, LANES), BF16)
    kcur, vcur = [], []
    for s in range(KV_WIDTH // LANES):
        ks = k_ref[:, s * LANES:(s + 1) * LANES].astype(F32)
        vs = v_ref[:, s * LANES:(s + 1) * LANES].astype(F32)
        kcur.extend(t.astype(BF16) for t in _dup_halves(ks, lane_lo))
        vcur.extend(t.astype(BF16) for t in _dup_halves(vs, lane_lo))

    qi = lax.broadcasted_iota(jnp.int32, (2 * blk, blk), 0) % blk
    si = lax.broadcasted_iota(jnp.int32, (2 * blk, blk), 1)
    cur_ok = si <= qi
    prev_ok = jnp.logical_and(si > qi, i > 0)
    scale = HEAD_DIM ** -0.5
    group = ATTN_Q_HEADS // (KV_WIDTH // HEAD_DIM)
    row_lo = lax.broadcasted_iota(jnp.int32, (2 * blk, 1), 0) < blk

    def stacked_q(p):
        q = q_ref[:, p * LANES:(p + 1) * LANES].astype(F32) * scale
        zero = jnp.zeros_like(q)
        return jnp.concatenate([jnp.where(lane_lo, q, zero), jnp.where(lane_lo, zero, q)], axis=0).astype(BF16)

    for p0 in range(0, PAIRS, ATTN_GROUP):
        ps = range(p0, p0 + ATTN_GROUP)
        kvh = {p: (2 * p) // group for p in ps}
        qs = {p: stacked_q(p) for p in ps}
        s = {p: jnp.where(cur_ok, _dot_nt(qs[p], kcur[kvh[p]]),
                          jnp.where(prev_ok, _dot_nt(qs[p], kprev_ref[kvh[p]]), NEG_BIG)) for p in ps}
        sink = {p: jnp.where(row_lo, sink_ref[2 * p], sink_ref[2 * p + 1]) for p in ps}
        m = {p: jnp.maximum(jnp.max(s[p], axis=-1, keepdims=True), sink[p]) for p in ps}
        e = {p: jnp.exp(s[p] - m[p]) for p in ps}
        e2 = {p: jnp.concatenate([jnp.where(cur_ok, e[p], 0.0), jnp.where(cur_ok, 0.0, e[p])], axis=1).astype(BF16)
              for p in ps}
        acc = {p: jnp.dot(e2[p], jnp.concatenate([jnp.concatenate([vcur[kvh[p]], ones], axis=1),
                                                  jnp.concatenate([vprev_ref[kvh[p]], ones], axis=1)], axis=0),
                          preferred_element_type=F32) for p in ps}
        for p in ps:
            num = acc[p][:, :LANES]
            den = acc[p][:, LANES:] + jnp.exp(sink[p] - m[p])
            o = jnp.where(lane_lo, num[:blk], num[blk:]) / jnp.where(lane_lo, den[:blk], den[blk:])
            ga = ga_ref[:, p * LANES:(p + 1) * LANES].astype(F32)
            o_ref[:, p * LANES:(p + 1) * LANES] = (o * _silu(ga)).astype(o_ref.dtype)

    for g in range(KV_WIDTH // HEAD_DIM):
        kprev_ref[g] = kcur[g]
        vprev_ref[g] = vcur[g]


def _attention(proj, sinks, batch, seq):
    M = proj.shape[0]
    nb = seq // WINDOW
    row = lambda b, i: b * nb + i
    n_kv = KV_WIDTH // HEAD_DIM
    return pl.pallas_call(
        _attn_kernel,
        out_shape=jax.ShapeDtypeStruct((M, ATTN_WIDTH), BF16),
        grid=(batch, nb),
        in_specs=[pl.BlockSpec(memory_space=pltpu.SMEM),
                  pl.BlockSpec((WINDOW, ATTN_WIDTH), lambda b, i: (row(b, i), COL_Q // ATTN_WIDTH)),
                  pl.BlockSpec((WINDOW, ATTN_WIDTH), lambda b, i: (row(b, i), COL_GA // ATTN_WIDTH)),
                  pl.BlockSpec((WINDOW, KV_WIDTH), lambda b, i: (row(b, i), COL_K // KV_WIDTH)),
                  pl.BlockSpec((WINDOW, KV_WIDTH), lambda b, i: (row(b, i), COL_V // KV_WIDTH))],
        out_specs=pl.BlockSpec((WINDOW, ATTN_WIDTH), lambda b, i: (row(b, i), 0)),
        scratch_shapes=[pltpu.VMEM((n_kv, WINDOW, LANES), BF16),
                        pltpu.VMEM((n_kv, WINDOW, LANES), BF16)],
        compiler_params=pltpu.CompilerParams(dimension_semantics=("parallel", "arbitrary"),
                                             vmem_limit_bytes=VMEM_LIMIT),
        name="swa_attn",
    )(sinks, proj, proj, proj, proj)


RWKV_TT = 512


def _rwkv_kernel(r_ref, k_ref, v_ref, g_ref, wa_ref,
                 mu_r_ref, mu_k_ref, mu_v_ref, mu_wa_ref,
                 w0_ref, wup_ref, a0_ref, aup_ref, kk_ref, ka_ref, rk_ref, lnw_ref, lnb_ref,
                 o_ref, state_ref, last_ref, m_st, n_st, rhat_st, y0_st, post_st, *, nt, total):
    s = pl.program_id(0)
    TT = RWKV_TT
    L = CHUNK
    t_in = jnp.minimum(s, total - 1) % nt
    t_out = jnp.maximum(s - 1, 0) % nt
    cur = s % 2
    prv = 1 - cur

    @pl.when(s == 0)
    def _():
        for ref in (state_ref, last_ref, m_st, n_st, rhat_st, y0_st, post_st):
            ref[...] = jnp.zeros_like(ref)

    chunks = range(TT // L)
    cs = [slice(c * L, (c + 1) * L) for c in chunks]

    carry = {"H": jnp.where(t_out == 0, 0.0, state_ref[...]), "ys": [], "next": 0}

    def advance():
        c = carry["next"]
        if c >= len(chunks):
            return
        H = carry["H"].astype(BF16)
        carry["ys"].append(jnp.dot(rhat_st[prv, cs[c], :], H, preferred_element_type=F32) + y0_st[prv, cs[c], :])
        carry["H"] = jnp.dot(m_st[prv, c], H, preferred_element_type=F32) + n_st[prv, c]
        carry["next"] = c + 1

    row0 = lax.broadcasted_iota(jnp.int32, (TT, LANES), 0) == 0
    first = t_in == 0

    def shifted(ref, slot, mu_ref):
        x = ref[...].astype(F32)
        carried = jnp.where(first, 0.0, last_ref[slot, 7:8, :])
        prev = jnp.where(row0, carried, pltpu.roll(x, 1, axis=0))
        last_ref[slot] = x[TT - 8:, :]
        return x + (prev - x) * mu_ref[...]

    r = shifted(r_ref, 0, mu_r_ref)
    k = shifted(k_ref, 1, mu_k_ref)
    v = shifted(v_ref, 2, mu_v_ref)
    wa = shifted(wa_ref, 3, mu_wa_ref)
    wd = wa[:, :LORA]
    ad = wa[:, LORA:]

    z = -(w0_ref[...] + _dot(jnp.tanh(wd), wup_ref[...]))
    softplus = jnp.maximum(z, 0.0) + jnp.log(1.0 + jnp.exp(-jnp.abs(z)))
    lw = -jnp.exp(-softplus - 0.5)
    a = _sigmoid(a0_ref[...] + _dot(ad, aup_ref[...]))
    advance()

    lane_r = lax.broadcasted_iota(jnp.int32, (LANES, LANES), 0)
    lane_c = lax.broadcasted_iota(jnp.int32, (LANES, LANES), 1)
    head_blk = (lane_r // HEAD_DIM) == (lane_c // HEAD_DIM)
    ones_blk = jnp.where(head_blk, 1.0, 0.0).astype(BF16)

    kk = k * kk_ref[...]
    kk = kk * lax.rsqrt(jnp.maximum(_dot_exact_rhs(kk * kk, ones_blk), 1e-24))
    k = k * (1.0 + (a - 1.0) * ka_ref[...])
    av = -kk
    bv = kk * a
    post_st[cur, 0] = _dot_exact_rhs(r * k * rk_ref[...], ones_blk) * v
    post_st[cur, 1] = _silu(g_ref[...].astype(F32))

    TB = 4 * L
    tr = lax.broadcasted_iota(jnp.int32, (TB, TB), 0)
    tc = lax.broadcasted_iota(jnp.int32, (TB, TB), 1)
    tri = jnp.where(jnp.logical_and(tr // L == tc // L, tc <= tr), 1.0, 0.0).astype(BF16)
    cum = jnp.concatenate([_dot_exact_lhs(tri, lw[i * TB:(i + 1) * TB]) for i in range(TT // TB)], axis=0)
    advance()

    e_pos = jnp.exp(cum)
    e_neg = jnp.exp(-cum)
    r_t = r * e_pos
    k_t = k * e_neg
    b_t = bv * e_neg
    a_t = av * jnp.exp(cum - lw)

    lane_lo = lax.broadcasted_iota(jnp.int32, (L, LANES), 1) < HEAD_DIM
    sr = lax.broadcasted_iota(jnp.int32, (2 * L, 2 * L), 0)
    sc = lax.broadcasted_iota(jnp.int32, (2 * L, 2 * L), 1)
    same = (sr // L) == (sc // L)
    strict = jnp.logical_and(same, sc < sr)
    incl = jnp.logical_and(same, sc <= sr)
    eye = sr == sc

    def stack(x):
        zero = jnp.zeros_like(x)
        return jnp.concatenate([jnp.where(lane_lo, x, zero), jnp.where(lane_lo, zero, x)], axis=0)

    def fold(x):
        return x[:L] + x[L:]

    W2 = 2 * L
    cum_l = [cum[sl][L - 1:L, :] for sl in cs]
    to_end = [jnp.exp(cum_l[c] - cum[cs[c]]) for c in chunks]
    at_s = [stack(a_t[sl]) for sl in cs]
    v_s = [stack(v[sl]) for sl in cs]
    gram = [_dot_nt(jnp.concatenate([at_s[c], stack(r_t[cs[c]])], axis=0),
                    jnp.concatenate([b_t[cs[c]]] * 2 + [k_t[cs[c]]] * 2, axis=0)) for c in chunks]
    advance()
    a_ab = [jnp.where(strict, gram[c][:W2, :W2], 0.0) for c in chunks]
    a_ak = [jnp.where(strict, gram[c][:W2, W2:], 0.0) for c in chunks]
    a_rb = [jnp.where(incl, gram[c][W2:, :W2], 0.0) for c in chunks]
    a_rk = [jnp.where(incl, gram[c][W2:, W2:], 0.0) for c in chunks]
    inv = [jnp.where(eye, 1.0, 0.0) + a_ab[c] for c in chunks]
    pw = [_dot(a_ab[c], a_ab[c]) for c in chunks]
    w_s = [_dot(a_ak[c], v_s[c]) for c in chunks]
    advance()
    for _ in range(4):
        both = [_dot(pw[c], jnp.concatenate([pw[c], inv[c]], axis=1)) for c in chunks]
        pw = [both[c][:, :W2] for c in chunks]
        inv = [inv[c] + both[c][:, W2:] for c in chunks]
        advance()
    inv = [inv[c] + _dot(pw[c], inv[c]) for c in chunks]
    sol = [_dot(inv[c], jnp.concatenate([at_s[c], w_s[c]], axis=1)) for c in chunks]
    au = [jnp.concatenate([fold(sol[c][:, :LANES]), fold(sol[c][:, LANES:])], axis=1) for c in chunks]
    bmn = [_dot_tn(bv[cs[c]] * to_end[c], au[c]) for c in chunks]
    kv = [_dot_tn(k[cs[c]] * to_end[c], v[cs[c]]) for c in chunks]
    rb = [_dot(a_rb[c], sol[c]) for c in chunks]
    rkv = [_dot(a_rk[c], v_s[c]) for c in chunks]
    while carry["next"] < len(chunks):
        advance()

    state_ref[...] = carry["H"]
    y = jnp.concatenate(carry["ys"], axis=0)
    mean_blk = jnp.where(head_blk, 1.0 / HEAD_DIM, 0.0).astype(BF16)
    mean = _dot_exact_rhs(y, mean_blk)
    yc = y - mean
    var = _dot_exact_rhs(yc * yc, mean_blk)
    yn = yc * lax.rsqrt(var + GN_EPS) * lnw_ref[...] + lnb_ref[...]
    o_ref[...] = ((yn + post_st[prv, 0]) * post_st[prv, 1]).astype(o_ref.dtype)

    for c in chunks:
        m_st[cur, c] = (jnp.where(head_blk, bmn[c][:, :LANES], 0.0)
                        + jnp.where(eye, jnp.exp(cum_l[c]), 0.0)).astype(BF16)
        n_st[cur, c] = jnp.where(head_blk, bmn[c][:, LANES:] + kv[c], 0.0)
        rhat_st[cur, cs[c], :] = (r_t[cs[c]] + fold(rb[c][:, :LANES])).astype(BF16)
        y0_st[cur, cs[c], :] = fold(rb[c][:, LANES:] + rkv[c])


def _rwkv(proj, mu, w0, w_up, a0, a_up, k_k, k_a, r_k, ln_w, ln_b, batch, seq):
    M = proj.shape[0]
    TT = RWKV_TT
    nt = seq // TT
    n_chunks = TT // CHUNK
    total = batch * PAIRS * nt

    def where(s):
        b, p, t = s // (PAIRS * nt), (s // nt) % PAIRS, s % nt
        return b * nt + t, p

    src = lambda s: where(jnp.minimum(s, total - 1))
    dst = lambda s: where(jnp.maximum(s - 1, 0))

    slab = lambda col: pl.BlockSpec((TT, LANES), lambda s: (src(s)[0], col // LANES + src(s)[1]))
    vec = pl.BlockSpec((1, LANES), lambda s: (0, src(s)[1]))
    vec0 = pl.BlockSpec((1, LANES), lambda s: (0, 0))
    vec_dst = pl.BlockSpec((1, LANES), lambda s: (0, dst(s)[1]))
    lora = pl.BlockSpec((LORA, LANES), lambda s: (0, src(s)[1]))
    mu_r, mu_k, mu_v, mu_wa = (mu[:, :1024], mu[:, 1024:2048], mu[:, 2048:3072], mu[:, 3072:])
    return pl.pallas_call(
        functools.partial(_rwkv_kernel, nt=nt, total=total),
        out_shape=jax.ShapeDtypeStruct((M, RWKV_WIDTH), BF16),
        grid=(total + 1,),
        in_specs=[slab(COL_R), slab(COL_RK), slab(COL_RV), slab(COL_RG),
                  pl.BlockSpec((TT, LANES), lambda s: (src(s)[0], COL_WA // LANES)),
                  vec, vec, vec, vec0,
                  vec, lora, vec, lora, vec, vec, vec, vec_dst, vec_dst],
        out_specs=pl.BlockSpec((TT, LANES), lambda s: dst(s)),
        scratch_shapes=[pltpu.VMEM((LANES, LANES), F32),
                        pltpu.VMEM((4, 8, LANES), F32),
                        pltpu.VMEM((2, n_chunks, LANES, LANES), BF16),
                        pltpu.VMEM((2, n_chunks, LANES, LANES), F32),
                        pltpu.VMEM((2, TT, LANES), BF16),
                        pltpu.VMEM((2, TT, LANES), F32),
                        pltpu.VMEM((2, 2, TT, LANES), F32)],
        compiler_params=pltpu.CompilerParams(dimension_semantics=("arbitrary",),
                                             vmem_limit_bytes=VMEM_LIMIT),
        name="rwkv7_mix",
    )(proj, proj, proj, proj, proj, mu_r, mu_k, mu_v, mu_wa,
      w0, w_up, a0, a_up, k_k, k_a, r_k, ln_w, ln_b)


def _out_proj_kernel(x_ref, ya_ref, yr_ref, wa_ref, wr_ref, g_ref, gate_ref, o_ref):
    mix = (jnp.dot(ya_ref[...], wa_ref[...], preferred_element_type=F32)
           + jnp.dot(yr_ref[...], wr_ref[...], preferred_element_type=F32))
    inv = lax.rsqrt(jnp.mean(mix * mix, axis=-1, keepdims=True) + RMS_EPS)
    o_ref[...] = x_ref[...] + gate_ref[0] * ((mix * inv) * g_ref[...])


def _out_proj(x2, ya, yr, w_a, w_r, g, gate, seq):
    M, D = x2.shape
    tm = 512
    per_b = seq // tm
    half = ya.shape[1]
    return pl.pallas_call(
        _out_proj_kernel,
        out_shape=jax.ShapeDtypeStruct((M, D), F32),
        grid=(M // tm,),
        in_specs=[pl.BlockSpec((tm, D), lambda i: (i, 0)),
                  pl.BlockSpec((tm, half), lambda i: (i, 0)),
                  pl.BlockSpec((tm, half), lambda i: (i, 0)),
                  pl.BlockSpec((half, D), lambda i: (0, 0)),
                  pl.BlockSpec((half, D), lambda i: (0, 0)),
                  pl.BlockSpec((1, D), lambda i: (0, 0)),
                  pl.BlockSpec((1, 1, D), lambda i: (i // per_b, 0, 0))],
        out_specs=pl.BlockSpec((tm, D), lambda i: (i, 0)),
        compiler_params=pltpu.CompilerParams(dimension_semantics=("parallel",),
                                             vmem_limit_bytes=VMEM_LIMIT),
        name="out_proj",
    )(x2, ya, yr, w_a, w_r, g, gate)


def _pack_w_in(w):
    q, ka, va, ga = w[:, 0:1024], w[:, 1024:1280], w[:, 1280:1536], w[:, 1536:2560]
    rkv, wa, gr = w[:, 2560:5632], w[:, 5632:5760], w[:, 5760:6784]
    pad = jnp.zeros((w.shape[0], IN_COLS_PAD - IN_COLS), w.dtype)
    return jnp.concatenate([q, ga, ka, va, rkv, gr, wa, pad], axis=1).astype(BF16)


def kernel(x, c, w_ada, b_ada, pre_norm_g, post_norm_g, w_in, w_out, attn_sinks, rwkv_mu, rwkv_w0,
           rwkv_w_up, rwkv_a0, rwkv_a_up, rwkv_k_k, rwkv_k_a, rwkv_r_k, rwkv_ln_w, rwkv_ln_b):
    B, T, D = x.shape
    depth = w_ada.shape[0]
    x2 = x.reshape(B * T, D)
    for l in range(depth):
        mod = _adaln(c, w_ada[l], b_ada[l][None, :])
        shift, scale, gate = (mod[:, i * D:(i + 1) * D].reshape(B, 1, D) for i in range(3))
        proj = _in_proj(x2, pre_norm_g[l][None, :], scale, shift, _pack_w_in(w_in[l]), T)
        y_attn = _attention(proj, attn_sinks[l], B, T)
        y_rwkv = _rwkv(proj, rwkv_mu[l][None, :], rwkv_w0[l][None, :], rwkv_w_up[l].astype(BF16),
                       rwkv_a0[l][None, :], rwkv_a_up[l].astype(BF16), rwkv_k_k[l][None, :],
                       rwkv_k_a[l][None, :], rwkv_r_k[l].reshape(1, RWKV_WIDTH),
                       rwkv_ln_w[l][None, :], rwkv_ln_b[l][None, :], B, T)
        w_o = w_out[l].astype(BF16)
        x2 = _out_proj(x2, y_attn, y_rwkv, w_o[:ATTN_WIDTH], w_o[ATTN_WIDTH:],
                       post_norm_g[l][None, :], gate, T)
    return x2.reshape(B, T, D)
```

```python
import functools

import jax
import jax.numpy as jnp
from jax import lax
from jax.experimental import pallas as pl
from jax.experimental.pallas import tpu as pltpu

D_MODEL = 2048
HEAD_DIM = 64
ATTN_WIDTH = 1024
ATTN_Q_HEADS = 16
KV_WIDTH = 256
WINDOW = 128
RWKV_WIDTH = 1024
LORA = 64
IN_COLS = 6784
RMS_EPS = 1e-6
GN_EPS = 64e-5
NEG_BIG = -1e30

LANES = 128
PAIRS = RWKV_WIDTH // LANES
CHUNK = 64

COL_Q = 0
COL_GA = 1024
COL_K = 2048
COL_V = 2304
COL_R = 2560
COL_RK = 3584
COL_RV = 4608
COL_RG = 5632
COL_WA = 6656
IN_COLS_PAD = 6912

VMEM_LIMIT = 56 * 1024 * 1024

F32 = jnp.float32
BF16 = jnp.bfloat16


def _dot(a, b):
    return jnp.dot(a.astype(BF16), b.astype(BF16), preferred_element_type=F32)


def _dot_nt(a, b):
    return lax.dot_general(a.astype(BF16), b.astype(BF16), (((1,), (1,)), ((), ())),
                           preferred_element_type=F32)


def _dot_tn(a, b):
    return lax.dot_general(a.astype(BF16), b.astype(BF16), (((0,), (0,)), ((), ())),
                           preferred_element_type=F32)


def _split3(x):
    hi = x.astype(BF16)
    r1 = x - hi.astype(F32)
    mid = r1.astype(BF16)
    lo = (r1 - mid.astype(F32)).astype(BF16)
    return hi, mid, lo


def _split2(x):
    hi = x.astype(BF16)
    return hi, (x - hi.astype(F32)).astype(BF16)


def _dot_exact_rhs(a, b_bf16):
    hi, lo = _split2(a)
    return jnp.dot(jnp.concatenate([hi, lo], axis=1), jnp.concatenate([b_bf16, b_bf16], axis=0),
                   preferred_element_type=F32)


def _dot_exact_lhs(a_bf16, b):
    hi, lo = _split2(b)
    return jnp.dot(a_bf16, hi, preferred_element_type=F32) + jnp.dot(a_bf16, lo, preferred_element_type=F32)


def _sigmoid(x):
    return 1.0 / (1.0 + jnp.exp(-x))


def _silu(x):
    return x * _sigmoid(x)


def _adaln_kernel(c_ref, w_ref, b_ref, o_ref):
    s = _silu(c_ref[...])
    s_hi, s_mid, s_lo = _split3(s)
    w_hi, w_mid, w_lo = _split3(w_ref[...])
    d = lambda p, q: jnp.dot(p, q, preferred_element_type=F32)
    acc = d(s_hi, w_hi) + (d(s_hi, w_mid) + d(s_mid, w_hi)) + (d(s_hi, w_lo) + d(s_mid, w_mid) + d(s_lo, w_hi))
    o_ref[...] = acc + b_ref[...]


def _adaln(c, w, b):
    B, D = c.shape
    N = w.shape[1]
    tn = 512
    return pl.pallas_call(
        _adaln_kernel,
        out_shape=jax.ShapeDtypeStruct((B, N), F32),
        grid=(N // tn,),
        in_specs=[pl.BlockSpec((B, D), lambda j: (0, 0)),
                  pl.BlockSpec((D, tn), lambda j: (0, j)),
                  pl.BlockSpec((1, tn), lambda j: (0, j))],
        out_specs=pl.BlockSpec((B, tn), lambda j: (0, j)),
        compiler_params=pltpu.CompilerParams(dimension_semantics=("arbitrary",),
                                             vmem_limit_bytes=VMEM_LIMIT),
        name="adaln_mod",
    )(c, w, b)


def _in_proj_kernel(x_ref, g_ref, scale_ref, shift_ref, w_ref, o_ref, h_ref):
    @pl.when(pl.program_id(1) == 0)
    def _():
        x = x_ref[...]
        inv = lax.rsqrt(jnp.mean(x * x, axis=-1, keepdims=True) + RMS_EPS)
        h = (x * inv) * g_ref[...] * (1.0 + scale_ref[0]) + shift_ref[0]
        h_ref[...] = h.astype(BF16)

    o_ref[...] = jnp.dot(h_ref[...], w_ref[...], preferred_element_type=F32).astype(o_ref.dtype)


def _in_proj(x2, g, scale, shift, w_packed, seq):
    M, D = x2.shape
    NP = w_packed.shape[1]
    tm, tn = 1024, 768
    per_b = seq // tm
    return pl.pallas_call(
        _in_proj_kernel,
        out_shape=jax.ShapeDtypeStruct((M, NP), BF16),
        grid=(M // tm, NP // tn),
        in_specs=[pl.BlockSpec((tm, D), lambda i, j: (i, 0)),
                  pl.BlockSpec((1, D), lambda i, j: (0, 0)),
                  pl.BlockSpec((1, 1, D), lambda i, j: (i // per_b, 0, 0)),
                  pl.BlockSpec((1, 1, D), lambda i, j: (i // per_b, 0, 0)),
                  pl.BlockSpec((D, tn), lambda i, j: (0, j))],
        out_specs=pl.BlockSpec((tm, tn), lambda i, j: (i, j)),
        scratch_shapes=[pltpu.VMEM((tm, D), BF16)],
        compiler_params=pltpu.CompilerParams(dimension_semantics=("parallel", "arbitrary"),
                                             vmem_limit_bytes=VMEM_LIMIT),
        name="in_proj",
    )(x2, g, scale, shift, w_packed)


ATTN_GROUP = 4


def _dup_halves(slab, lane_lo):
    swapped = pltpu.roll(slab, HEAD_DIM, axis=1)
    return jnp.where(lane_lo, slab, swapped), jnp.where(lane_lo, swapped, slab)


def _attn_kernel(sink_ref, q_ref, ga_ref, k_ref, v_ref, o_ref, kprev_ref, vprev_ref):
    i = pl.program_id(1)
    blk = WINDOW
    lane_lo = lax.broadcasted_iota(jnp.int32, (blk, LANES), 1) < HEAD_DIM

    @pl.when(i == 0)
    def _():
        kprev_ref[...] = jnp.zeros_like(kprev_ref)
        vprev_ref[...] = jnp.zeros_like(vprev_ref)

    ones = jnp.ones((blk, LANES), BF16)
    kcur, vcur = [], []
    for s in range(KV_WIDTH // LANES):
        ks = k_ref[:, s * LANES:(s + 1) * LANES].astype(F32)
        vs = v_ref[:, s * LANES:(s + 1) * LANES].astype(F32)
        kcur.extend(t.astype(BF16) for t in _dup_halves(ks, lane_lo))
        vcur.extend(t.astype(BF16) for t in _dup_halves(vs, lane_lo))

    qi = lax.broadcasted_iota(jnp.int32, (2 * blk, blk), 0) % blk
    si = lax.broadcasted_iota(jnp.int32, (2 * blk, blk), 1)
    cur_ok = si <= qi
    prev_ok = jnp.logical_and(si > qi, i > 0)
    scale = HEAD_DIM ** -0.5
    group = ATTN_Q_HEADS // (KV_WIDTH // HEAD_DIM)
    row_lo = lax.broadcasted_iota(jnp.int32, (2 * blk, 1), 0) < blk

    def stacked_q(p):
        q = q_ref[:, p * LANES:(p + 1) * LANES].astype(F32) * scale
        zero = jnp.zeros_like(q)
        return jnp.concatenate([jnp.where(lane_lo, q, zero), jnp.where(lane_lo, zero, q)], axis=0).astype(BF16)

    for p0 in range(0, PAIRS, ATTN_GROUP):
        ps = range(p0, p0 + ATTN_GROUP)
        kvh = {p: (2 * p) // group for p in ps}
        qs = {p: stacked_q(p) for p in ps}
        s = {p: jnp.where(cur_ok, _dot_nt(qs[p], kcur[kvh[p]]),
                          jnp.where(prev_ok, _dot_nt(qs[p], kprev_ref[kvh[p]]), NEG_BIG)) for p in ps}
        sink = {p: jnp.where(row_lo, sink_ref[2 * p], sink_ref[2 * p + 1]) for p in ps}
        m = {p: jnp.maximum(jnp.max(s[p], axis=-1, keepdims=True), sink[p]) for p in ps}
        e = {p: jnp.exp(s[p] - m[p]) for p in ps}
        e2 = {p: jnp.concatenate([jnp.where(cur_ok, e[p], 0.0), jnp.where(cur_ok, 0.0, e[p])], axis=1).astype(BF16)
              for p in ps}
        acc = {p: jnp.dot(e2[p], jnp.concatenate([jnp.concatenate([vcur[kvh[p]], ones], axis=1),
                                                  jnp.concatenate([vprev_ref[kvh[p]], ones], axis=1)], axis=0),
                          preferred_element_type=F32) for p in ps}
        for p in ps:
            num = acc[p][:, :LANES]
            den = acc[p][:, LANES:] + jnp.exp(sink[p] - m[p])
            o = jnp.where(lane_lo, num[:blk], num[blk:]) / jnp.where(lane_lo, den[:blk], den[blk:])
            ga = ga_ref[:, p * LANES:(p + 1) * LANES].astype(F32)
            o_ref[:, p * LANES:(p + 1) * LANES] = (o * _silu(ga)).astype(o_ref.dtype)

    for g in range(KV_WIDTH // HEAD_DIM):
        kprev_ref[g] = kcur[g]
        vprev_ref[g] = vcur[g]


def _attention(proj, sinks, batch, seq):
    M = proj.shape[0]
    nb = seq // WINDOW
    row = lambda b, i: b * nb + i
    n_kv = KV_WIDTH // HEAD_DIM
    return pl.pallas_call(
        _attn_kernel,
        out_shape=jax.ShapeDtypeStruct((M, ATTN_WIDTH), BF16),
        grid=(batch, nb),
        in_specs=[pl.BlockSpec(memory_space=pltpu.SMEM),
                  pl.BlockSpec((WINDOW, ATTN_WIDTH), lambda b, i: (row(b, i), COL_Q // ATTN_WIDTH)),
                  pl.BlockSpec((WINDOW, ATTN_WIDTH), lambda b, i: (row(b, i), COL_GA // ATTN_WIDTH)),
                  pl.BlockSpec((WINDOW, KV_WIDTH), lambda b, i: (row(b, i), COL_K // KV_WIDTH)),
                  pl.BlockSpec((WINDOW, KV_WIDTH), lambda b, i: (row(b, i), COL_V // KV_WIDTH))],
        out_specs=pl.BlockSpec((WINDOW, ATTN_WIDTH), lambda b, i: (row(b, i), 0)),
        scratch_shapes=[pltpu.VMEM((n_kv, WINDOW, LANES), BF16),
                        pltpu.VMEM((n_kv, WINDOW, LANES), BF16)],
        compiler_params=pltpu.CompilerParams(dimension_semantics=("parallel", "arbitrary"),
                                             vmem_limit_bytes=VMEM_LIMIT),
        name="swa_attn",
    )(sinks, proj, proj, proj, proj)


RWKV_TT = 512


def _rwkv_kernel(r_ref, k_ref, v_ref, g_ref, wa_ref,
                 mu_r_ref, mu_k_ref, mu_v_ref, mu_wa_ref,
                 w0_ref, wup_ref, a0_ref, aup_ref, kk_ref, ka_ref, rk_ref, lnw_ref, lnb_ref,
                 o_ref, state_ref, last_ref, m_st, n_st, rhat_st, y0_st, post_st, *, nt, total):
    s = pl.program_id(0)
    TT = RWKV_TT
    L = CHUNK
    t_in = jnp.minimum(s, total - 1) % nt
    t_out = jnp.maximum(s - 1, 0) % nt
    cur = s % 2
    prv = 1 - cur

    @pl.when(s == 0)
    def _():
        for ref in (state_ref, last_ref, m_st, n_st, rhat_st, y0_st, post_st):
            ref[...] = jnp.zeros_like(ref)

    chunks = range(TT // L)
    cs = [slice(c * L, (c + 1) * L) for c in chunks]

    lane_lo = lax.broadcasted_iota(jnp.int32, (L, LANES), 1) < HEAD_DIM

    def diag2(x):
        zero = jnp.zeros_like(x)
        return jnp.concatenate([jnp.where(lane_lo, x, zero), jnp.where(lane_lo, zero, x)], axis=0).astype(BF16)

    def fold(x):
        return x[:L] + x[L:]

    carry = {"H": jnp.where(t_out == 0, 0.0, state_ref[...]), "ys": [], "next": 0}

    def advance():
        c = carry["next"]
        if c >= len(chunks):
            return
        H = diag2(carry["H"])
        carry["ys"].append(jnp.dot(rhat_st[prv, cs[c], :], H, preferred_element_type=F32) + y0_st[prv, cs[c], :])
        carry["H"] = jnp.dot(m_st[prv, c], H, preferred_element_type=F32) + n_st[prv, c]
        carry["next"] = c + 1

    row0 = lax.broadcasted_iota(jnp.int32, (TT, LANES), 0) == 0
    first = t_in == 0

    def shifted(ref, slot, mu_ref):
        x = ref[...].astype(F32)
        carried = jnp.where(first, 0.0, last_ref[slot, 7:8, :])
        prev = jnp.where(row0, carried, pltpu.roll(x, 1, axis=0))
        last_ref[slot] = x[TT - 8:, :]
        return x + (prev - x) * mu_ref[...]

    r = shifted(r_ref, 0, mu_r_ref)
    k = shifted(k_ref, 1, mu_k_ref)
    v = shifted(v_ref, 2, mu_v_ref)
    wa = shifted(wa_ref, 3, mu_wa_ref)
    wd = wa[:, :LORA]
    ad = wa[:, LORA:]

    z = -(w0_ref[...] + _dot(jnp.tanh(wd), wup_ref[...]))
    softplus = jnp.maximum(z, 0.0) + jnp.log(1.0 + jnp.exp(-jnp.abs(z)))
    lw = -jnp.exp(-softplus - 0.5)
    a = _sigmoid(a0_ref[...] + _dot(ad, aup_ref[...]))
    advance()

    lane_r = lax.broadcasted_iota(jnp.int32, (LANES, LANES), 0)
    lane_c = lax.broadcasted_iota(jnp.int32, (LANES, LANES), 1)
    head_blk = (lane_r // HEAD_DIM) == (lane_c // HEAD_DIM)
    ones_blk = jnp.where(head_blk, 1.0, 0.0).astype(BF16)

    kk = k * kk_ref[...]
    kk = kk * lax.rsqrt(jnp.maximum(_dot(kk * kk, ones_blk), 1e-24))
    k = k * (1.0 + (a - 1.0) * ka_ref[...])
    av = -kk
    bv = kk * a
    post_st[cur, 0] = _dot(r * k * rk_ref[...], ones_blk) * v
    post_st[cur, 1] = _silu(g_ref[...].astype(F32))

    TB = 4 * L
    tr = lax.broadcasted_iota(jnp.int32, (TB, TB), 0)
    tc = lax.broadcasted_iota(jnp.int32, (TB, TB), 1)
    tri = jnp.where(jnp.logical_and(tr // L == tc // L, tc <= tr), 1.0, 0.0).astype(BF16)
    cum = jnp.concatenate([_dot_exact_lhs(tri, lw[i * TB:(i + 1) * TB]) for i in range(TT // TB)], axis=0)
    advance()

    e_pos = jnp.exp(cum)
    e_neg = jnp.exp(-cum)
    r_t = r * e_pos
    k_t = k * e_neg
    b_t = bv * e_neg
    a_t = av * jnp.exp(cum - lw)

    row = lax.broadcasted_iota(jnp.int32, (L, LANES), 0)
    col = lax.broadcasted_iota(jnp.int32, (L, LANES), 1) % HEAD_DIM
    strict = col < row
    incl = col <= row
    eye = col == row

    cum_l = [cum[sl][L - 1:L, :] for sl in cs]
    to_end = [jnp.exp(cum_l[c] - cum[cs[c]]) for c in chunks]
    bk_d = [jnp.concatenate([diag2(b_t[sl]), diag2(k_t[sl])], axis=0) for sl in cs]
    v_d = [diag2(v[sl]) for sl in cs]
    gram_a = [_dot_nt(a_t[cs[c]], bk_d[c]) for c in chunks]
    gram_r = [_dot_nt(r_t[cs[c]], bk_d[c]) for c in chunks]
    advance()
    a_ab = [jnp.where(strict, gram_a[c][:, :LANES], 0.0) for c in chunks]
    a_ak = [jnp.where(strict, gram_a[c][:, LANES:], 0.0) for c in chunks]
    a_rb = [jnp.where(incl, gram_r[c][:, :LANES], 0.0) for c in chunks]
    a_rk = [jnp.where(incl, gram_r[c][:, LANES:], 0.0) for c in chunks]
    inv = [jnp.where(eye, 1.0, 0.0) + a_ab[c] for c in chunks]
    pw = [_dot(a_ab[c], diag2(a_ab[c])) for c in chunks]
    w_s = [_dot(a_ak[c], v_d[c]) for c in chunks]
    advance()
    for _ in range(4):
        both = [_dot(pw[c], jnp.concatenate([diag2(pw[c]), diag2(inv[c])], axis=1)) for c in chunks]
        pw = [both[c][:, :LANES] for c in chunks]
        inv = [inv[c] + both[c][:, LANES:] for c in chunks]
        advance()
    inv = [inv[c] + _dot(pw[c], diag2(inv[c])) for c in chunks]
    sol = [_dot(inv[c], jnp.concatenate([diag2(a_t[cs[c]]), diag2(w_s[c])], axis=1)) for c in chunks]
    bmn = [_dot_tn(bv[cs[c]] * to_end[c], sol[c]) for c in chunks]
    kv = [_dot_tn(k[cs[c]] * to_end[c], v[cs[c]]) for c in chunks]
    rb = [_dot(a_rb[c], jnp.concatenate([diag2(sol[c][:, :LANES]), diag2(sol[c][:, LANES:])], axis=1))
          for c in chunks]
    rkv = [_dot(a_rk[c], v_d[c]) for c in chunks]
    while carry["next"] < len(chunks):
        advance()

    state_ref[...] = carry["H"]
    y = jnp.concatenate(carry["ys"], axis=0)
    mean_blk = jnp.where(head_blk, 1.0 / HEAD_DIM, 0.0).astype(BF16)
    mean = _dot(y, mean_blk)
    yc = y - mean
    var = _dot(yc * yc, mean_blk)
    yn = yc * lax.rsqrt(var + GN_EPS) * lnw_ref[...] + lnb_ref[...]
    o_ref[...] = ((yn + post_st[prv, 0]) * post_st[prv, 1]).astype(o_ref.dtype)

    for c in chunks:
        m_st[cur, c] = (fold(jnp.where(head_blk, bmn[c][:, :LANES], 0.0))
                        + jnp.where(eye, jnp.exp(cum_l[c]), 0.0)).astype(BF16)
        n_st[cur, c] = fold(jnp.where(head_blk, bmn[c][:, LANES:] + kv[c], 0.0))
        rhat_st[cur, cs[c], :] = (r_t[cs[c]] + rb[c][:, :LANES]).astype(BF16)
        y0_st[cur, cs[c], :] = rb[c][:, LANES:] + rkv[c]


def _rwkv(proj, mu, w0, w_up, a0, a_up, k_k, k_a, r_k, ln_w, ln_b, batch, seq):
    M = proj.shape[0]
    TT = RWKV_TT
    nt = seq // TT
    n_chunks = TT // CHUNK
    total = batch * PAIRS * nt

    def where(s):
        b, p, t = s // (PAIRS * nt), (s // nt) % PAIRS, s % nt
        return b * nt + t, p

    src = lambda s: where(jnp.minimum(s, total - 1))
    dst = lambda s: where(jnp.maximum(s - 1, 0))

    slab = lambda col: pl.BlockSpec((TT, LANES), lambda s: (src(s)[0], col // LANES + src(s)[1]))
    vec = pl.BlockSpec((1, LANES), lambda s: (0, src(s)[1]))
    vec0 = pl.BlockSpec((1, LANES), lambda s: (0, 0))
    vec_dst = pl.BlockSpec((1, LANES), lambda s: (0, dst(s)[1]))
    lora = pl.BlockSpec((LORA, LANES), lambda s: (0, src(s)[1]))
    mu_r, mu_k, mu_v, mu_wa = (mu[:, :1024], mu[:, 1024:2048], mu[:, 2048:3072], mu[:, 3072:])
    return pl.pallas_call(
        functools.partial(_rwkv_kernel, nt=nt, total=total),
        out_shape=jax.ShapeDtypeStruct((M, RWKV_WIDTH), BF16),
        grid=(total + 1,),
        in_specs=[slab(COL_R), slab(COL_RK), slab(COL_RV), slab(COL_RG),
                  pl.BlockSpec((TT, LANES), lambda s: (src(s)[0], COL_WA // LANES)),
                  vec, vec, vec, vec0,
                  vec, lora, vec, lora, vec, vec, vec, vec_dst, vec_dst],
        out_specs=pl.BlockSpec((TT, LANES), lambda s: dst(s)),
        scratch_shapes=[pltpu.VMEM((CHUNK, LANES), F32),
                        pltpu.VMEM((4, 8, LANES), F32),
                        pltpu.VMEM((2, n_chunks, CHUNK, LANES), BF16),
                        pltpu.VMEM((2, n_chunks, CHUNK, LANES), F32),
                        pltpu.VMEM((2, TT, LANES), BF16),
                        pltpu.VMEM((2, TT, LANES), F32),
                        pltpu.VMEM((2, 2, TT, LANES), F32)],
        compiler_params=pltpu.CompilerParams(dimension_semantics=("arbitrary",),
                                             vmem_limit_bytes=VMEM_LIMIT),
        name="rwkv7_mix",
    )(proj, proj, proj, proj, proj, mu_r, mu_k, mu_v, mu_wa,
      w0, w_up, a0, a_up, k_k, k_a, r_k, ln_w, ln_b)


def _out_proj_kernel(x_ref, ya_ref, yr_ref, wa_ref, wr_ref, g_ref, gate_ref, o_ref):
    mix = (jnp.dot(ya_ref[...], wa_ref[...], preferred_element_type=F32)
           + jnp.dot(yr_ref[...], wr_ref[...], preferred_element_type=F32))
    inv = lax.rsqrt(jnp.mean(mix * mix, axis=-1, keepdims=True) + RMS_EPS)
    o_ref[...] = x_ref[...] + gate_ref[0] * ((mix * inv) * g_ref[...])


def _out_proj(x2, ya, yr, w_a, w_r, g, gate, seq):
    M, D = x2.shape
    tm = 512
    per_b = seq // tm
    half = ya.shape[1]
    return pl.pallas_call(
        _out_proj_kernel,
        out_shape=jax.ShapeDtypeStruct((M, D), F32),
        grid=(M // tm,),
        in_specs=[pl.BlockSpec((tm, D), lambda i: (i, 0)),
                  pl.BlockSpec((tm, half), lambda i: (i, 0)),
                  pl.BlockSpec((tm, half), lambda i: (i, 0)),
                  pl.BlockSpec((half, D), lambda i: (0, 0)),
                  pl.BlockSpec((half, D), lambda i: (0, 0)),
                  pl.BlockSpec((1, D), lambda i: (0, 0)),
                  pl.BlockSpec((1, 1, D), lambda i: (i // per_b, 0, 0))],
        out_specs=pl.BlockSpec((tm, D), lambda i: (i, 0)),
        compiler_params=pltpu.CompilerParams(dimension_semantics=("parallel",),
                                             vmem_limit_bytes=VMEM_LIMIT),
        name="out_proj",
    )(x2, ya, yr, w_a, w_r, g, gate)


def _pack_w_in(w):
    q, ka, va, ga = w[:, 0:1024], w[:, 1024:1280], w[:, 1280:1536], w[:, 1536:2560]
    rkv, wa, gr = w[:, 2560:5632], w[:, 5632:5760], w[:, 5760:6784]
    pad = jnp.zeros((w.shape[0], IN_COLS_PAD - IN_COLS), w.dtype)
    return jnp.concatenate([q, ga, ka, va, rkv, gr, wa, pad], axis=1).astype(BF16)


def kernel(x, c, w_ada, b_ada, pre_norm_g, post_norm_g, w_in, w_out, attn_sinks, rwkv_mu, rwkv_w0,
           rwkv_w_up, rwkv_a0, rwkv_a_up, rwkv_k_k, rwkv_k_a, rwkv_r_k, rwkv_ln_w, rwkv_ln_b):
    B, T, D = x.shape
    depth = w_ada.shape[0]
    x2 = x.reshape(B * T, D)
    for l in range(depth):
        mod = _adaln(c, w_ada[l], b_ada[l][None, :])
        shift, scale, gate = (mod[:, i * D:(i + 1) * D].reshape(B, 1, D) for i in range(3))
        proj = _in_proj(x2, pre_norm_g[l][None, :], scale, shift, _pack_w_in(w_in[l]), T)
        y_attn = _attention(proj, attn_sinks[l], B, T)
        y_rwkv = _rwkv(proj, rwkv_mu[l][None, :], rwkv_w0[l][None, :], rwkv_w_up[l].astype(BF16),
                       rwkv_a0[l][None, :], rwkv_a_up[l].astype(BF16), rwkv_k_k[l][None, :],
                       rwkv_k_a[l][None, :], rwkv_r_k[l].reshape(1, RWKV_WIDTH),
                       rwkv_ln_w[l][None, :], rwkv_ln_b[l][None, :], B, T)
        w_o = w_out[l].astype(BF16)
        x2 = _out_proj(x2, y_attn, y_rwkv, w_o[:ATTN_WIDTH], w_o[ATTN_WIDTH:],
                       post_norm_g[l][None, :], gate, T)
    return x2.reshape(B, T, D)
```

```python
import functools

import jax
import jax.numpy as jnp
from jax import lax
from jax.experimental import pallas as pl
from jax.experimental.pallas import tpu as pltpu

D_MODEL = 2048
HEAD_DIM = 64
ATTN_WIDTH = 1024
ATTN_Q_HEADS = 16
KV_WIDTH = 256
WINDOW = 128
RWKV_WIDTH = 1024
LORA = 64
IN_COLS = 6784
RMS_EPS = 1e-6
GN_EPS = 64e-5
NEG_BIG = -1e30

LANES = 128
PAIRS = RWKV_WIDTH // LANES
CHUNK = 64

COL_Q = 0
COL_GA = 1024
COL_K = 2048
COL_V = 2304
COL_R = 2560
COL_RK = 3584
COL_RV = 4608
COL_RG = 5632
COL_WA = 6656
IN_COLS_PAD = 6912

VMEM_LIMIT = 56 * 1024 * 1024

F32 = jnp.float32
BF16 = jnp.bfloat16


def _dot(a, b):
    return jnp.dot(a.astype(BF16), b.astype(BF16), preferred_element_type=F32)


def _dot_nt(a, b):
    return lax.dot_general(a.astype(BF16), b.astype(BF16), (((1,), (1,)), ((), ())),
                           preferred_element_type=F32)


def _dot_tn(a, b):
    return lax.dot_general(a.astype(BF16), b.astype(BF16), (((0,), (0,)), ((), ())),
                           preferred_element_type=F32)


def _split3(x):
    hi = x.astype(BF16)
    r1 = x - hi.astype(F32)
    mid = r1.astype(BF16)
    lo = (r1 - mid.astype(F32)).astype(BF16)
    return hi, mid, lo


def _split2(x):
    hi = x.astype(BF16)
    return hi, (x - hi.astype(F32)).astype(BF16)


def _dot_exact_rhs(a, b_bf16):
    hi, lo = _split2(a)
    return jnp.dot(jnp.concatenate([hi, lo], axis=1), jnp.concatenate([b_bf16, b_bf16], axis=0),
                   preferred_element_type=F32)


def _dot_exact_lhs(a_bf16, b):
    hi, lo = _split2(b)
    return jnp.dot(a_bf16, hi, preferred_element_type=F32) + jnp.dot(a_bf16, lo, preferred_element_type=F32)


def _sigmoid(x):
    return 1.0 / (1.0 + jnp.exp(-x))


def _silu(x):
    return x * _sigmoid(x)


def _adaln_kernel(c_ref, w_ref, b_ref, o_ref):
    s = _silu(c_ref[...])
    s_hi, s_lo = _split2(s)
    w_hi, w_lo = _split2(w_ref[...])
    d = lambda p, q: jnp.dot(p, q, preferred_element_type=F32)
    o_ref[...] = d(s_hi, w_hi) + (d(s_hi, w_lo) + d(s_lo, w_hi)) + b_ref[...]


def _adaln(c, w, b):
    B, D = c.shape
    N = w.shape[1]
    tn = 512
    return pl.pallas_call(
        _adaln_kernel,
        out_shape=jax.ShapeDtypeStruct((B, N), F32),
        grid=(N // tn,),
        in_specs=[pl.BlockSpec((B, D), lambda j: (0, 0)),
                  pl.BlockSpec((D, tn), lambda j: (0, j)),
                  pl.BlockSpec((1, tn), lambda j: (0, j))],
        out_specs=pl.BlockSpec((B, tn), lambda j: (0, j)),
        compiler_params=pltpu.CompilerParams(dimension_semantics=("arbitrary",),
                                             vmem_limit_bytes=VMEM_LIMIT),
        name="adaln_mod",
    )(c, w, b)


NORM_ROWS = 16


def _in_proj_kernel(x_ref, g_ref, scale_ref, shift_ref, w_ref, o_ref, h_ref):
    @pl.when(pl.program_id(1) == 0)
    def _():
        gain = g_ref[...] * (1.0 + scale_ref[0])
        shift = shift_ref[0]

        def rows(c, carry):
            sl = pl.ds(pl.multiple_of(c * NORM_ROWS, NORM_ROWS), NORM_ROWS)
            x = x_ref[sl, :]
            inv = lax.rsqrt(jnp.mean(x * x, axis=-1, keepdims=True) + RMS_EPS)
            h_ref[sl, :] = ((x * inv) * gain + shift).astype(BF16)
            return carry

        lax.fori_loop(0, x_ref.shape[0] // NORM_ROWS, rows, 0, unroll=8)

    o_ref[...] = jnp.dot(h_ref[...], w_ref[...], preferred_element_type=F32).astype(o_ref.dtype)


def _in_proj(x2, g, scale, shift, w_packed, seq):
    M, D = x2.shape
    NP = w_packed.shape[1]
    tm, tn = 1024, 768
    per_b = seq // tm
    return pl.pallas_call(
        _in_proj_kernel,
        out_shape=jax.ShapeDtypeStruct((M, NP), BF16),
        grid=(M // tm, NP // tn),
        in_specs=[pl.BlockSpec((tm, D), lambda i, j: (i, 0)),
                  pl.BlockSpec((1, D), lambda i, j: (0, 0)),
                  pl.BlockSpec((1, 1, D), lambda i, j: (i // per_b, 0, 0)),
                  pl.BlockSpec((1, 1, D), lambda i, j: (i // per_b, 0, 0)),
                  pl.BlockSpec((D, tn), lambda i, j: (0, j))],
        out_specs=pl.BlockSpec((tm, tn), lambda i, j: (i, j)),
        scratch_shapes=[pltpu.VMEM((tm, D), BF16)],
        compiler_params=pltpu.CompilerParams(dimension_semantics=("parallel", "arbitrary"),
                                             vmem_limit_bytes=VMEM_LIMIT),
        name="in_proj",
    )(x2, g, scale, shift, w_packed)


ATTN_GROUP = 4
ATTN_ROWS = 2 * WINDOW


def _dup_halves(slab, lane_lo):
    swapped = pltpu.roll(slab, HEAD_DIM, axis=1)
    return jnp.where(lane_lo, slab, swapped), jnp.where(lane_lo, swapped, slab)


def _attn_kernel(sink_ref, q_ref, ga_ref, k_ref, v_ref, o_ref, kprev_ref, vprev_ref):
    i = pl.program_id(1)
    blk = WINDOW
    lane_lo = lax.broadcasted_iota(jnp.int32, (blk, LANES), 1) < HEAD_DIM

    @pl.when(i == 0)
    def _():
        kprev_ref[...] = jnp.zeros_like(kprev_ref)
        vprev_ref[...] = jnp.zeros_like(vprev_ref)

    ones = jnp.ones((blk, LANES), BF16)
    qi = lax.broadcasted_iota(jnp.int32, (2 * blk, blk), 0) % blk
    si = lax.broadcasted_iota(jnp.int32, (2 * blk, blk), 1)
    cur_ok = si <= qi
    scale = HEAD_DIM ** -0.5
    group = ATTN_Q_HEADS // (KV_WIDTH // HEAD_DIM)
    n_kv = KV_WIDTH // HEAD_DIM
    row_lo = lax.broadcasted_iota(jnp.int32, (2 * blk, 1), 0) < blk

    kprev = [kprev_ref[g] for g in range(n_kv)]
    vprev = [vprev_ref[g] for g in range(n_kv)]
    for w in range(ATTN_ROWS // blk):
        rows = slice(w * blk, (w + 1) * blk)
        prev_ok = jnp.logical_and(si > qi, i > 0) if w == 0 else si > qi

        kcur, vcur = [], []
        for s in range(KV_WIDTH // LANES):
            ks = k_ref[rows, s * LANES:(s + 1) * LANES].astype(F32)
            vs = v_ref[rows, s * LANES:(s + 1) * LANES].astype(F32)
            kcur.extend(t.astype(BF16) for t in _dup_halves(ks, lane_lo))
            vcur.extend(t.astype(BF16) for t in _dup_halves(vs, lane_lo))

        def stacked_q(p):
            q = q_ref[rows, p * LANES:(p + 1) * LANES].astype(F32) * scale
            zero = jnp.zeros_like(q)
            return jnp.concatenate([jnp.where(lane_lo, q, zero), jnp.where(lane_lo, zero, q)], axis=0).astype(BF16)

        for p0 in range(0, PAIRS, ATTN_GROUP):
            ps = range(p0, p0 + ATTN_GROUP)
            kvh = {p: (2 * p) // group for p in ps}
            qs = {p: stacked_q(p) for p in ps}
            s = {p: jnp.where(cur_ok, _dot_nt(qs[p], kcur[kvh[p]]),
                              jnp.where(prev_ok, _dot_nt(qs[p], kprev[kvh[p]]), NEG_BIG)) for p in ps}
            sink = {p: jnp.where(row_lo, sink_ref[2 * p], sink_ref[2 * p + 1]) for p in ps}
            m = {p: jnp.maximum(jnp.max(s[p], axis=-1, keepdims=True), sink[p]) for p in ps}
            e = {p: jnp.exp(s[p] - m[p]) for p in ps}
            e2 = {p: jnp.concatenate([jnp.where(cur_ok, e[p], 0.0), jnp.where(cur_ok, 0.0, e[p])],
                                     axis=1).astype(BF16) for p in ps}
            acc = {p: jnp.dot(e2[p], jnp.concatenate([jnp.concatenate([vcur[kvh[p]], ones], axis=1),
                                                      jnp.concatenate([vprev[kvh[p]], ones], axis=1)], axis=0),
                              preferred_element_type=F32) for p in ps}
            for p in ps:
                num = acc[p][:, :LANES]
                den = acc[p][:, LANES:] + jnp.exp(sink[p] - m[p])
                o = jnp.where(lane_lo, num[:blk], num[blk:]) / jnp.where(lane_lo, den[:blk], den[blk:])
                ga = ga_ref[rows, p * LANES:(p + 1) * LANES].astype(F32)
                o_ref[rows, p * LANES:(p + 1) * LANES] = (o * _silu(ga)).astype(o_ref.dtype)
        kprev, vprev = kcur, vcur

    for g in range(n_kv):
        kprev_ref[g] = kprev[g]
        vprev_ref[g] = vprev[g]


def _attention(proj, sinks, batch, seq):
    M = proj.shape[0]
    nb = seq // ATTN_ROWS
    row = lambda b, i: b * nb + i
    n_kv = KV_WIDTH // HEAD_DIM
    return pl.pallas_call(
        _attn_kernel,
        out_shape=jax.ShapeDtypeStruct((M, ATTN_WIDTH), BF16),
        grid=(batch, nb),
        in_specs=[pl.BlockSpec(memory_space=pltpu.SMEM),
                  pl.BlockSpec((ATTN_ROWS, ATTN_WIDTH), lambda b, i: (row(b, i), COL_Q // ATTN_WIDTH)),
                  pl.BlockSpec((ATTN_ROWS, ATTN_WIDTH), lambda b, i: (row(b, i), COL_GA // ATTN_WIDTH)),
                  pl.BlockSpec((ATTN_ROWS, KV_WIDTH), lambda b, i: (row(b, i), COL_K // KV_WIDTH)),
                  pl.BlockSpec((ATTN_ROWS, KV_WIDTH), lambda b, i: (row(b, i), COL_V // KV_WIDTH))],
        out_specs=pl.BlockSpec((ATTN_ROWS, ATTN_WIDTH), lambda b, i: (row(b, i), 0)),
        scratch_shapes=[pltpu.VMEM((n_kv, WINDOW, LANES), BF16),
                        pltpu.VMEM((n_kv, WINDOW, LANES), BF16)],
        compiler_params=pltpu.CompilerParams(dimension_semantics=("parallel", "arbitrary"),
                                             vmem_limit_bytes=VMEM_LIMIT),
        name="swa_attn",
    )(sinks, proj, proj, proj, proj)


RWKV_TT = 512


def _rwkv_kernel(r_ref, k_ref, v_ref, g_ref, wa_ref,
                 mu_r_ref, mu_k_ref, mu_v_ref, mu_wa_ref,
                 w0_ref, wup_ref, a0_ref, aup_ref, kk_ref, ka_ref, rk_ref, lnw_ref, lnb_ref,
                 o_ref, state_ref, last_ref, m_st, n_st, rhat_st, y0_st, post_st, *, nt, total):
    s = pl.program_id(0)
    TT = RWKV_TT
    L = CHUNK
    t_in = jnp.minimum(s, total - 1) % nt
    t_out = jnp.maximum(s - 1, 0) % nt
    cur = s % 2
    prv = 1 - cur

    @pl.when(s == 0)
    def _():
        for ref in (state_ref, last_ref, m_st, n_st, rhat_st, y0_st, post_st):
            ref[...] = jnp.zeros_like(ref)

    chunks = range(TT // L)
    cs = [slice(c * L, (c + 1) * L) for c in chunks]

    lane_lo = lax.broadcasted_iota(jnp.int32, (L, LANES), 1) < HEAD_DIM

    def diag2(x):
        zero = jnp.zeros_like(x)
        return jnp.concatenate([jnp.where(lane_lo, x, zero), jnp.where(lane_lo, zero, x)], axis=0).astype(BF16)

    def fold(x):
        return x[:L] + x[L:]

    carry = {"H": jnp.where(t_out == 0, 0.0, state_ref[...]), "ys": [], "next": 0}

    def advance():
        c = carry["next"]
        if c >= len(chunks):
            return
        H = diag2(carry["H"])
        carry["ys"].append(jnp.dot(rhat_st[prv, cs[c], :], H, preferred_element_type=F32) + y0_st[prv, cs[c], :])
        carry["H"] = jnp.dot(m_st[prv, c], H, preferred_element_type=F32) + n_st[prv, c]
        carry["next"] = c + 1

    row0 = lax.broadcasted_iota(jnp.int32, (TT, LANES), 0) == 0
    first = t_in == 0

    def shifted(ref, slot, mu_ref):
        x = ref[...].astype(F32)
        carried = jnp.where(first, 0.0, last_ref[slot, 7:8, :])
        prev = jnp.where(row0, carried, pltpu.roll(x, 1, axis=0))
        last_ref[slot] = x[TT - 8:, :]
        return x + (prev - x) * mu_ref[...]

    r = shifted(r_ref, 0, mu_r_ref)
    k = shifted(k_ref, 1, mu_k_ref)
    v = shifted(v_ref, 2, mu_v_ref)
    wa = shifted(wa_ref, 3, mu_wa_ref)
    wd = wa[:, :LORA]
    ad = wa[:, LORA:]

    z = -(w0_ref[...] + _dot(jnp.tanh(wd), wup_ref[...]))
    softplus = jnp.maximum(z, 0.0) + jnp.log(1.0 + jnp.exp(-jnp.abs(z)))
    lw = -jnp.exp(-softplus - 0.5)
    a = _sigmoid(a0_ref[...] + _dot(ad, aup_ref[...]))
    advance()

    lane_r = lax.broadcasted_iota(jnp.int32, (LANES, LANES), 0)
    lane_c = lax.broadcasted_iota(jnp.int32, (LANES, LANES), 1)
    head_blk = (lane_r // HEAD_DIM) == (lane_c // HEAD_DIM)
    ones_blk = jnp.where(head_blk, 1.0, 0.0).astype(BF16)

    kk = k * kk_ref[...]
    kk = kk * lax.rsqrt(jnp.maximum(_dot(kk * kk, ones_blk), 1e-24))
    k = k * (1.0 + (a - 1.0) * ka_ref[...])
    av = -kk
    bv = kk * a
    post_st[cur, 0] = _dot(r * k * rk_ref[...], ones_blk) * v
    post_st[cur, 1] = _silu(g_ref[...].astype(F32))

    TB = 4 * L
    tr = lax.broadcasted_iota(jnp.int32, (TB, TB), 0)
    tc = lax.broadcasted_iota(jnp.int32, (TB, TB), 1)
    tri = jnp.where(jnp.logical_and(tr // L == tc // L, tc <= tr), 1.0, 0.0).astype(BF16)
    cum = jnp.concatenate([_dot_exact_lhs(tri, lw[i * TB:(i + 1) * TB]) for i in range(TT // TB)], axis=0)
    advance()

    e_pos = jnp.exp(cum)
    e_neg = jnp.exp(-cum)
    r_t = r * e_pos
    k_t = k * e_neg
    b_t = bv * e_neg
    a_t = av * jnp.exp(cum - lw)

    row = lax.broadcasted_iota(jnp.int32, (L, LANES), 0)
    col = lax.broadcasted_iota(jnp.int32, (L, LANES), 1) % HEAD_DIM
    strict = col < row
    incl = col <= row
    eye = col == row

    cum_l = [cum[sl][L - 1:L, :] for sl in cs]
    p_l = [jnp.exp(cum_l[c]) for c in chunks]
    bk_d = [jnp.concatenate([diag2(b_t[sl]), diag2(k_t[sl])], axis=0) for sl in cs]
    v_d = [diag2(v[sl]) for sl in cs]
    gram_a = [_dot_nt(a_t[cs[c]], bk_d[c]) for c in chunks]
    gram_r = [_dot_nt(r_t[cs[c]], bk_d[c]) for c in chunks]
    advance()
    a_ab = [jnp.where(strict, gram_a[c][:, :LANES], 0.0) for c in chunks]
    a_ak = [jnp.where(strict, gram_a[c][:, LANES:], 0.0) for c in chunks]
    a_rb = [jnp.where(incl, gram_r[c][:, :LANES], 0.0) for c in chunks]
    a_rk = [jnp.where(incl, gram_r[c][:, LANES:], 0.0) for c in chunks]
    inv = [jnp.where(eye, 1.0, 0.0) + a_ab[c] for c in chunks]
    pw = [_dot(a_ab[c], diag2(a_ab[c])) for c in chunks]
    w_s = [_dot(a_ak[c], v_d[c]) for c in chunks]
    advance()
    for _ in range(4):
        both = [_dot(pw[c], jnp.concatenate([diag2(pw[c]), diag2(inv[c])], axis=1)) for c in chunks]
        pw = [both[c][:, :LANES] for c in chunks]
        inv = [inv[c] + both[c][:, LANES:] for c in chunks]
        advance()
    inv = [inv[c] + _dot(pw[c], diag2(inv[c])) for c in chunks]
    sol = [_dot(inv[c], jnp.concatenate([diag2(a_t[cs[c]]), diag2(w_s[c])], axis=1)) for c in chunks]
    bmn = [_dot_tn(b_t[cs[c]] * p_l[c], sol[c]) for c in chunks]
    kv = [_dot_tn(k_t[cs[c]] * p_l[c], v[cs[c]]) for c in chunks]
    rb = [_dot(a_rb[c], jnp.concatenate([diag2(sol[c][:, :LANES]), diag2(sol[c][:, LANES:])], axis=1))
          for c in chunks]
    rkv = [_dot(a_rk[c], v_d[c]) for c in chunks]
    while carry["next"] < len(chunks):
        advance()

    state_ref[...] = carry["H"]
    y = jnp.concatenate(carry["ys"], axis=0)
    mean_blk = jnp.where(head_blk, 1.0 / HEAD_DIM, 0.0).astype(BF16)
    mean = _dot(y, mean_blk)
    yc = y - mean
    var = _dot(yc * yc, mean_blk)
    yn = yc * lax.rsqrt(var + GN_EPS) * lnw_ref[...] + lnb_ref[...]
    o_ref[...] = ((yn + post_st[prv, 0]) * post_st[prv, 1]).astype(o_ref.dtype)

    for c in chunks:
        m_st[cur, c] = (fold(jnp.where(head_blk, bmn[c][:, :LANES], 0.0))
                        + jnp.where(eye, p_l[c], 0.0)).astype(BF16)
        n_st[cur, c] = fold(jnp.where(head_blk, bmn[c][:, LANES:] + kv[c], 0.0))
        rhat_st[cur, cs[c], :] = (r_t[cs[c]] + rb[c][:, :LANES]).astype(BF16)
        y0_st[cur, cs[c], :] = rb[c][:, LANES:] + rkv[c]


def _rwkv(proj, mu, w0, w_up, a0, a_up, k_k, k_a, r_k, ln_w, ln_b, batch, seq):
    M = proj.shape[0]
    TT = RWKV_TT
    nt = seq // TT
    n_chunks = TT // CHUNK
    total = batch * PAIRS * nt

    def where(s):
        b, p, t = s // (PAIRS * nt), (s // nt) % PAIRS, s % nt
        return b * nt + t, p

    src = lambda s: where(jnp.minimum(s, total - 1))
    dst = lambda s: where(jnp.maximum(s - 1, 0))

    slab = lambda col: pl.BlockSpec((TT, LANES), lambda s: (src(s)[0], col // LANES + src(s)[1]))
    vec = pl.BlockSpec((1, LANES), lambda s: (0, src(s)[1]))
    vec0 = pl.BlockSpec((1, LANES), lambda s: (0, 0))
    vec_dst = pl.BlockSpec((1, LANES), lambda s: (0, dst(s)[1]))
    lora = pl.BlockSpec((LORA, LANES), lambda s: (0, src(s)[1]))
    mu_r, mu_k, mu_v, mu_wa = (mu[:, :1024], mu[:, 1024:2048], mu[:, 2048:3072], mu[:, 3072:])
    return pl.pallas_call(
        functools.partial(_rwkv_kernel, nt=nt, total=total),
        out_shape=jax.ShapeDtypeStruct((M, RWKV_WIDTH), BF16),
        grid=(total + 1,),
        in_specs=[slab(COL_R), slab(COL_RK), slab(COL_RV), slab(COL_RG),
                  pl.BlockSpec((TT, LANES), lambda s: (src(s)[0], COL_WA // LANES)),
                  vec, vec, vec, vec0,
                  vec, lora, vec, lora, vec, vec, vec, vec_dst, vec_dst],
        out_specs=pl.BlockSpec((TT, LANES), lambda s: dst(s)),
        scratch_shapes=[pltpu.VMEM((CHUNK, LANES), F32),
                        pltpu.VMEM((4, 8, LANES), F32),
                        pltpu.VMEM((2, n_chunks, CHUNK, LANES), BF16),
                        pltpu.VMEM((2, n_chunks, CHUNK, LANES), F32),
                        pltpu.VMEM((2, TT, LANES), BF16),
                        pltpu.VMEM((2, TT, LANES), F32),
                        pltpu.VMEM((2, 2, TT, LANES), F32)],
        compiler_params=pltpu.CompilerParams(dimension_semantics=("arbitrary",),
                                             vmem_limit_bytes=VMEM_LIMIT),
        name="rwkv7_mix",
    )(proj, proj, proj, proj, proj, mu_r, mu_k, mu_v, mu_wa,
      w0, w_up, a0, a_up, k_k, k_a, r_k, ln_w, ln_b)


RWKV_PIECE = 2 * CHUNK


def _rwkv3_kernel(r_ref, k_ref, v_ref, g_ref, wa_ref,
                  mu_r_ref, mu_k_ref, mu_v_ref, mu_wa_ref,
                  w0_ref, wup_ref, a0_ref, aup_ref, kk_ref, ka_ref, rk_ref, lnw_ref, lnb_ref,
                  o_ref, state_ref, last_ref,
                  pa_st, pr_st, pb_st, pk_st, pv_st, pbb_st, pkb_st, ppl_st,
                  m_st, n_st, rhat_st, y0_st, post_st, *, nt, total):
    s = pl.program_id(0)
    TT, L, P = RWKV_TT, CHUNK, RWKV_PIECE
    t_in = jnp.minimum(s, total - 1) % nt
    t_out = jnp.maximum(s - 2, 0) % nt
    cur = s % 2
    prv = 1 - cur
    post_w = s % 3
    post_r = (s + 1) % 3

    @pl.when(s == 0)
    def _():
        for ref in (state_ref, last_ref, pa_st, pr_st, pb_st, pk_st, pv_st, pbb_st, pkb_st, ppl_st,
                    m_st, n_st, rhat_st, y0_st, post_st):
            ref[...] = jnp.zeros_like(ref)

    chunks = range(TT // L)
    cs = [slice(c * L, (c + 1) * L) for c in chunks]

    lane_r = lax.broadcasted_iota(jnp.int32, (LANES, LANES), 0)
    lane_c = lax.broadcasted_iota(jnp.int32, (LANES, LANES), 1)
    head_blk = (lane_r // HEAD_DIM) == (lane_c // HEAD_DIM)
    ones_blk = jnp.where(head_blk, 1.0, 0.0).astype(BF16)

    lane_lo = lax.broadcasted_iota(jnp.int32, (L, LANES), 1) < HEAD_DIM

    def diag2(x):
        zero = jnp.zeros_like(x)
        return jnp.concatenate([jnp.where(lane_lo, x, zero), jnp.where(lane_lo, zero, x)], axis=0).astype(BF16)

    def fold(x):
        return x[:L] + x[L:]

    carry = {"H": jnp.where(t_out == 0, 0.0, state_ref[...]), "ys": [], "next": 0}

    def link():
        c = carry["next"]
        if c >= len(chunks):
            return
        H = diag2(carry["H"])
        carry["ys"].append(jnp.dot(rhat_st[prv, cs[c], :], H, preferred_element_type=F32) + y0_st[prv, cs[c], :])
        carry["H"] = jnp.dot(m_st[prv, c], H, preferred_element_type=F32) + n_st[prv, c]
        carry["next"] = c + 1

    n_pieces = TT // P
    first = t_in == 0
    row0 = lax.broadcasted_iota(jnp.int32, (P, LANES), 0) == 0
    pr_i = lax.broadcasted_iota(jnp.int32, (P, P), 0)
    pc_i = lax.broadcasted_iota(jnp.int32, (P, P), 1)
    tri = jnp.where(jnp.logical_and(pr_i // L == pc_i // L, pc_i <= pr_i), 1.0, 0.0).astype(BF16)
    todo = {"piece": 0}

    def prepare():
        j = todo["piece"]
        if j >= n_pieces:
            return
        todo["piece"] = j + 1
        rows = slice(j * P, (j + 1) * P)

        def shifted(ref, slot, mu_ref):
            x = ref[rows, :].astype(F32)
            if j == 0:
                before = jnp.where(first, 0.0, last_ref[slot, 7:8, :])
            else:
                before = ref[j * P - 16:j * P, :].astype(F32)[15:16, :]
            prev = jnp.where(row0, before, pltpu.roll(x, 1, axis=0))
            if j == n_pieces - 1:
                last_ref[slot] = x[P - 8:, :]
            return x + (prev - x) * mu_ref[...]

        r = shifted(r_ref, 0, mu_r_ref)
        k = shifted(k_ref, 1, mu_k_ref)
        v = shifted(v_ref, 2, mu_v_ref)
        wa = shifted(wa_ref, 3, mu_wa_ref)
        z = -(w0_ref[...] + _dot(jnp.tanh(wa[:, :LORA]), wup_ref[...]))
        softplus = jnp.maximum(z, 0.0) + jnp.log(1.0 + jnp.exp(-jnp.abs(z)))
        lw = -jnp.exp(-softplus - 0.5)
        a = _sigmoid(a0_ref[...] + _dot(wa[:, LORA:], aup_ref[...]))
        kk = k * kk_ref[...]
        kk = kk * lax.rsqrt(jnp.maximum(_dot(kk * kk, ones_blk), 1e-24))
        k = k * (1.0 + (a - 1.0) * ka_ref[...])
        bv = kk * a
        post_st[post_w, 0, rows, :] = _dot(r * k * rk_ref[...], ones_blk) * v
        post_st[post_w, 1, rows, :] = _silu(g_ref[rows, :].astype(F32))
        cum = _dot_exact_lhs(tri, lw)
        e_neg = jnp.exp(-cum)
        pr_st[cur, rows, :] = r * jnp.exp(cum)
        pa_st[cur, rows, :] = (-kk * jnp.exp(cum - lw)).astype(BF16)
        pb_st[cur, rows, :] = (bv * e_neg).astype(BF16)
        pk_st[cur, rows, :] = (k * e_neg).astype(BF16)
        pv_st[cur, rows, :] = v.astype(BF16)
        for h in range(P // L):
            sub = slice(h * L, (h + 1) * L)
            dst_rows = cs[j * (P // L) + h]
            cum_l = cum[(h + 1) * L - 1:(h + 1) * L, :]
            to_end = jnp.exp(cum_l - cum[sub])
            pbb_st[cur, dst_rows, :] = (bv[sub] * to_end).astype(BF16)
            pkb_st[cur, dst_rows, :] = (k[sub] * to_end).astype(BF16)
            ppl_st[cur, j * (P // L) + h] = jnp.broadcast_to(jnp.exp(cum_l), (8, LANES))

    row = lax.broadcasted_iota(jnp.int32, (L, LANES), 0)
    col = lax.broadcasted_iota(jnp.int32, (L, LANES), 1) % HEAD_DIM
    strict = col < row
    incl = col <= row
    eye = col == row

    a_t = [pa_st[prv, sl, :] for sl in cs]
    r_t = [pr_st[prv, sl, :] for sl in cs]
    v_c = [pv_st[prv, sl, :] for sl in cs]
    bk_d = [jnp.concatenate([diag2(pb_st[prv, sl, :]), diag2(pk_st[prv, sl, :])], axis=0) for sl in cs]
    v_d = [diag2(v_c[c]) for c in chunks]
    gram_a = [_dot_nt(a_t[c], bk_d[c]) for c in chunks]
    gram_r = [_dot_nt(r_t[c], bk_d[c]) for c in chunks]
    prepare()
    link()
    a_ab = [jnp.where(strict, gram_a[c][:, :LANES], 0.0) for c in chunks]
    a_ak = [jnp.where(strict, gram_a[c][:, LANES:], 0.0) for c in chunks]
    a_rb = [jnp.where(incl, gram_r[c][:, :LANES], 0.0) for c in chunks]
    a_rk = [jnp.where(incl, gram_r[c][:, LANES:], 0.0) for c in chunks]
    inv = [jnp.where(eye, 1.0, 0.0) + a_ab[c] for c in chunks]
    pw = [_dot(a_ab[c], diag2(a_ab[c])) for c in chunks]
    w_s = [_dot(a_ak[c], v_d[c]) for c in chunks]
    prepare()
    link()
    for _ in range(4):
        both = [_dot(pw[c], jnp.concatenate([diag2(pw[c]), diag2(inv[c])], axis=1)) for c in chunks]
        pw = [both[c][:, :LANES] for c in chunks]
        inv = [inv[c] + both[c][:, LANES:] for c in chunks]
        prepare()
        link()
    inv = [inv[c] + _dot(pw[c], diag2(inv[c])) for c in chunks]
    sol = [_dot(inv[c], jnp.concatenate([diag2(a_t[c]), diag2(w_s[c])], axis=1)) for c in chunks]
    link()
    bmn = [_dot_tn(pbb_st[prv, cs[c], :], sol[c]) for c in chunks]
    kv = [_dot_tn(pkb_st[prv, cs[c], :], v_c[c]) for c in chunks]
    rb = [_dot(a_rb[c], jnp.concatenate([diag2(sol[c][:, :LANES]), diag2(sol[c][:, LANES:])], axis=1))
          for c in chunks]
    rkv = [_dot(a_rk[c], v_d[c]) for c in chunks]
    while todo["piece"] < n_pieces:
        prepare()
    while carry["next"] < len(chunks):
        link()

    state_ref[...] = carry["H"]
    y = jnp.concatenate(carry["ys"], axis=0)
    mean_blk = jnp.where(head_blk, 1.0 / HEAD_DIM, 0.0).astype(BF16)
    mean = _dot(y, mean_blk)
    yc = y - mean
    var = _dot(yc * yc, mean_blk)
    yn = yc * lax.rsqrt(var + GN_EPS) * lnw_ref[...] + lnb_ref[...]
    o_ref[...] = ((yn + post_st[post_r, 0]) * post_st[post_r, 1]).astype(o_ref.dtype)

    for c in chunks:
        m_st[cur, c] = (fold(jnp.where(head_blk, bmn[c][:, :LANES], 0.0))
                        + jnp.where(eye, ppl_st[prv, c, 0:1, :], 0.0)).astype(BF16)
        n_st[cur, c] = fold(jnp.where(head_blk, bmn[c][:, LANES:] + kv[c], 0.0))
        rhat_st[cur, cs[c], :] = (r_t[c] + rb[c][:, :LANES]).astype(BF16)
        y0_st[cur, cs[c], :] = rb[c][:, LANES:] + rkv[c]


def _rwkv3(proj, mu, w0, w_up, a0, a_up, k_k, k_a, r_k, ln_w, ln_b, batch, seq):
    M = proj.shape[0]
    TT = RWKV_TT
    nt = seq // TT
    n_chunks = TT // CHUNK
    total = batch * PAIRS * nt

    def where(s):
        b, p, t = s // (PAIRS * nt), (s // nt) % PAIRS, s % nt
        return b * nt + t, p

    src = lambda s: where(jnp.minimum(s, total - 1))
    dst = lambda s: where(jnp.maximum(s - 2, 0))

    slab = lambda col: pl.BlockSpec((TT, LANES), lambda s: (src(s)[0], col // LANES + src(s)[1]))
    vec = pl.BlockSpec((1, LANES), lambda s: (0, src(s)[1]))
    vec0 = pl.BlockSpec((1, LANES), lambda s: (0, 0))
    vec_dst = pl.BlockSpec((1, LANES), lambda s: (0, dst(s)[1]))
    lora = pl.BlockSpec((LORA, LANES), lambda s: (0, src(s)[1]))
    mu_r, mu_k, mu_v, mu_wa = (mu[:, :1024], mu[:, 1024:2048], mu[:, 2048:3072], mu[:, 3072:])
    operand = lambda dt: pltpu.VMEM((2, TT, LANES), dt)
    return pl.pallas_call(
        functools.partial(_rwkv3_kernel, nt=nt, total=total),
        out_shape=jax.ShapeDtypeStruct((M, RWKV_WIDTH), BF16),
        grid=(total + 2,),
        in_specs=[slab(COL_R), slab(COL_RK), slab(COL_RV), slab(COL_RG),
                  pl.BlockSpec((TT, LANES), lambda s: (src(s)[0], COL_WA // LANES)),
                  vec, vec, vec, vec0,
                  vec, lora, vec, lora, vec, vec, vec, vec_dst, vec_dst],
        out_specs=pl.BlockSpec((TT, LANES), lambda s: dst(s)),
        scratch_shapes=[pltpu.VMEM((CHUNK, LANES), F32),
                        pltpu.VMEM((4, 8, LANES), F32),
                        operand(BF16), operand(F32), operand(BF16), operand(BF16), operand(BF16),
                        operand(BF16), operand(BF16),
                        pltpu.VMEM((2, n_chunks, 8, LANES), F32),
                        pltpu.VMEM((2, n_chunks, CHUNK, LANES), BF16),
                        pltpu.VMEM((2, n_chunks, CHUNK, LANES), F32),
                        pltpu.VMEM((2, TT, LANES), BF16),
                        pltpu.VMEM((2, TT, LANES), F32),
                        pltpu.VMEM((3, 2, TT, LANES), F32)],
        compiler_params=pltpu.CompilerParams(dimension_semantics=("arbitrary",),
                                             vmem_limit_bytes=VMEM_LIMIT),
        name="rwkv7_mix",
    )(proj, proj, proj, proj, proj, mu_r, mu_k, mu_v, mu_wa,
      w0, w_up, a0, a_up, k_k, k_a, r_k, ln_w, ln_b)


OUT_SUB_ROWS = 128


def _out_proj_kernel(x_ref, ya_ref, yr_ref, wa_ref, wr_ref, g_ref, gate_ref, o_ref):
    scale = gate_ref[0] * g_ref[...]
    for r0 in range(0, x_ref.shape[0], OUT_SUB_ROWS):
        sl = slice(r0, r0 + OUT_SUB_ROWS)
        mix = (jnp.dot(ya_ref[sl, :], wa_ref[...], preferred_element_type=F32)
               + jnp.dot(yr_ref[sl, :], wr_ref[...], preferred_element_type=F32))
        inv = lax.rsqrt(jnp.mean(mix * mix, axis=-1, keepdims=True) + RMS_EPS)
        o_ref[sl, :] = x_ref[sl, :] + (mix * inv) * scale


def _out_proj(x2, ya, yr, w_a, w_r, g, gate, seq):
    M, D = x2.shape
    tm = 512
    per_b = seq // tm
    half = ya.shape[1]
    return pl.pallas_call(
        _out_proj_kernel,
        out_shape=jax.ShapeDtypeStruct((M, D), F32),
        grid=(M // tm,),
        in_specs=[pl.BlockSpec((tm, D), lambda i: (i, 0)),
                  pl.BlockSpec((tm, half), lambda i: (i, 0)),
                  pl.BlockSpec((tm, half), lambda i: (i, 0)),
                  pl.BlockSpec((half, D), lambda i: (0, 0)),
                  pl.BlockSpec((half, D), lambda i: (0, 0)),
                  pl.BlockSpec((1, D), lambda i: (0, 0)),
                  pl.BlockSpec((1, 1, D), lambda i: (i // per_b, 0, 0))],
        out_specs=pl.BlockSpec((tm, D), lambda i: (i, 0)),
        compiler_params=pltpu.CompilerParams(dimension_semantics=("parallel",),
                                             vmem_limit_bytes=VMEM_LIMIT),
        name="out_proj",
    )(x2, ya, yr, w_a, w_r, g, gate)


def _pack_w_in(w):
    q, ka, va, ga = w[:, 0:1024], w[:, 1024:1280], w[:, 1280:1536], w[:, 1536:2560]
    rkv, wa, gr = w[:, 2560:5632], w[:, 5632:5760], w[:, 5760:6784]
    pad = jnp.zeros((w.shape[0], IN_COLS_PAD - IN_COLS), w.dtype)
    return jnp.concatenate([q, ga, ka, va, rkv, gr, wa, pad], axis=1).astype(BF16)


def kernel(x, c, w_ada, b_ada, pre_norm_g, post_norm_g, w_in, w_out, attn_sinks, rwkv_mu, rwkv_w0,
           rwkv_w_up, rwkv_a0, rwkv_a_up, rwkv_k_k, rwkv_k_a, rwkv_r_k, rwkv_ln_w, rwkv_ln_b):
    B, T, D = x.shape
    depth = w_ada.shape[0]
    x2 = x.reshape(B * T, D)
    for l in range(depth):
        mod = _adaln(c, w_ada[l], b_ada[l][None, :])
        shift, scale, gate = (mod[:, i * D:(i + 1) * D].reshape(B, 1, D) for i in range(3))
        proj = _in_proj(x2, pre_norm_g[l][None, :], scale, shift, _pack_w_in(w_in[l]), T)
        y_attn = _attention(proj, attn_sinks[l], B, T)
        y_rwkv = _rwkv(proj, rwkv_mu[l][None, :], rwkv_w0[l][None, :], rwkv_w_up[l].astype(BF16),
                       rwkv_a0[l][None, :], rwkv_a_up[l].astype(BF16), rwkv_k_k[l][None, :],
                       rwkv_k_a[l][None, :], rwkv_r_k[l].reshape(1, RWKV_WIDTH),
                       rwkv_ln_w[l][None, :], rwkv_ln_b[l][None, :], B, T)
        w_o = w_out[l].astype(BF16)
        x2 = _out_proj(x2, y_attn, y_rwkv, w_o[:ATTN_WIDTH], w_o[ATTN_WIDTH:],
                       post_norm_g[l][None, :], gate, T)
    return x2.reshape(B, T, D)
```

```python
import functools
import math

import jax
import jax.numpy as jnp
from jax import lax
from jax.experimental import pallas as pl
from jax.experimental.pallas import tpu as pltpu

D_MODEL = 2048
HEAD_DIM = 64
ATTN_WIDTH = 1024
ATTN_Q_HEADS = 16
KV_WIDTH = 256
WINDOW = 128
RWKV_WIDTH = 1024
LORA = 64
IN_COLS = 6784
RMS_EPS = 1e-6
GN_EPS = 64e-5
NEG_BIG = -1e30

LANES = 128
PAIRS = RWKV_WIDTH // LANES
CHUNK = 64

COL_Q = 0
COL_GA = 1024
COL_K = 2048
COL_V = 2304
COL_R = 2560
COL_RK = 3584
COL_RV = 4608
COL_RG = 5632
COL_WA = 6656
IN_COLS_PAD = 6912

VMEM_LIMIT = 56 * 1024 * 1024

F32 = jnp.float32
BF16 = jnp.bfloat16


def _dot(a, b):
    return jnp.dot(a.astype(BF16), b.astype(BF16), preferred_element_type=F32)


def _dot_nt(a, b):
    return lax.dot_general(a.astype(BF16), b.astype(BF16), (((1,), (1,)), ((), ())),
                           preferred_element_type=F32)


def _dot_tn(a, b):
    return lax.dot_general(a.astype(BF16), b.astype(BF16), (((0,), (0,)), ((), ())),
                           preferred_element_type=F32)


def _split3(x):
    hi = x.astype(BF16)
    r1 = x - hi.astype(F32)
    mid = r1.astype(BF16)
    lo = (r1 - mid.astype(F32)).astype(BF16)
    return hi, mid, lo


def _split2(x):
    hi = x.astype(BF16)
    return hi, (x - hi.astype(F32)).astype(BF16)


def _dot_exact_rhs(a, b_bf16):
    hi, lo = _split2(a)
    return jnp.dot(jnp.concatenate([hi, lo], axis=1), jnp.concatenate([b_bf16, b_bf16], axis=0),
                   preferred_element_type=F32)


def _dot_exact_lhs(a_bf16, b):
    hi, lo = _split2(b)
    return jnp.dot(a_bf16, hi, preferred_element_type=F32) + jnp.dot(a_bf16, lo, preferred_element_type=F32)


def _sigmoid(x):
    return 1.0 / (1.0 + jnp.exp(-x))


def _silu(x):
    return x * _sigmoid(x)


def _adaln_kernel(c_ref, w_ref, b_ref, o_ref):
    s = _silu(c_ref[...])
    s_hi, s_lo = _split2(s)
    w_hi, w_lo = _split2(w_ref[...])
    d = lambda p, q: jnp.dot(p, q, preferred_element_type=F32)
    o_ref[...] = d(s_hi, w_hi) + (d(s_hi, w_lo) + d(s_lo, w_hi)) + b_ref[...]


def _adaln(c, w, b):
    B, D = c.shape
    N = w.shape[1]
    tn = 512
    return pl.pallas_call(
        _adaln_kernel,
        out_shape=jax.ShapeDtypeStruct((B, N), F32),
        grid=(N // tn,),
        in_specs=[pl.BlockSpec((B, D), lambda j: (0, 0)),
                  pl.BlockSpec((D, tn), lambda j: (0, j)),
                  pl.BlockSpec((1, tn), lambda j: (0, j))],
        out_specs=pl.BlockSpec((B, tn), lambda j: (0, j)),
        compiler_params=pltpu.CompilerParams(dimension_semantics=("arbitrary",),
                                             vmem_limit_bytes=VMEM_LIMIT),
        name="adaln_mod",
    )(c, w, b)


NORM_ROWS = 16


def _in_proj_kernel(x_ref, g_ref, scale_ref, shift_ref, w_ref, o_ref, h_ref):
    @pl.when(pl.program_id(1) == 0)
    def _():
        gain = g_ref[...] * (1.0 + scale_ref[0])
        shift = shift_ref[0]

        def rows(c, carry):
            sl = pl.ds(pl.multiple_of(c * NORM_ROWS, NORM_ROWS), NORM_ROWS)
            x = x_ref[sl, :]
            inv = lax.rsqrt(jnp.mean(x * x, axis=-1, keepdims=True) + RMS_EPS)
            h_ref[sl, :] = ((x * inv) * gain + shift).astype(BF16)
            return carry

        lax.fori_loop(0, x_ref.shape[0] // NORM_ROWS, rows, 0, unroll=8)

    o_ref[...] = jnp.dot(h_ref[...], w_ref[...], preferred_element_type=F32).astype(o_ref.dtype)


def _in_proj(x2, g, scale, shift, w_packed, seq):
    M, D = x2.shape
    NP = w_packed.shape[1]
    tm, tn = 1024, 768
    per_b = seq // tm
    return pl.pallas_call(
        _in_proj_kernel,
        out_shape=jax.ShapeDtypeStruct((M, NP), BF16),
        grid=(M // tm, NP // tn),
        in_specs=[pl.BlockSpec((tm, D), lambda i, j: (i, 0)),
                  pl.BlockSpec((1, D), lambda i, j: (0, 0)),
                  pl.BlockSpec((1, 1, D), lambda i, j: (i // per_b, 0, 0)),
                  pl.BlockSpec((1, 1, D), lambda i, j: (i // per_b, 0, 0)),
                  pl.BlockSpec((D, tn), lambda i, j: (0, j))],
        out_specs=pl.BlockSpec((tm, tn), lambda i, j: (i, j)),
        scratch_shapes=[pltpu.VMEM((tm, D), BF16)],
        compiler_params=pltpu.CompilerParams(dimension_semantics=("parallel", "arbitrary"),
                                             vmem_limit_bytes=VMEM_LIMIT),
        name="in_proj",
    )(x2, g, scale, shift, w_packed)


ATTN_GROUP = 4
ATTN_ROWS = 2 * WINDOW


def _dup_halves(slab, lane_lo):
    swapped = pltpu.roll(slab, HEAD_DIM, axis=1)
    return jnp.where(lane_lo, slab, swapped), jnp.where(lane_lo, swapped, slab)


def _attn_kernel(sink_ref, q_ref, ga_ref, k_ref, v_ref, o_ref, kprev_ref, vprev_ref):
    i = pl.program_id(1)
    blk = WINDOW
    lane_lo = lax.broadcasted_iota(jnp.int32, (blk, LANES), 1) < HEAD_DIM

    @pl.when(i == 0)
    def _():
        kprev_ref[...] = jnp.zeros_like(kprev_ref)
        vprev_ref[...] = jnp.zeros_like(vprev_ref)

    ones = jnp.ones((blk, LANES), BF16)
    qi = lax.broadcasted_iota(jnp.int32, (2 * blk, blk), 0) % blk
    si = lax.broadcasted_iota(jnp.int32, (2 * blk, blk), 1)
    cur_ok = si <= qi
    scale = HEAD_DIM ** -0.5
    group = ATTN_Q_HEADS // (KV_WIDTH // HEAD_DIM)
    n_kv = KV_WIDTH // HEAD_DIM
    row_lo = lax.broadcasted_iota(jnp.int32, (2 * blk, 1), 0) < blk

    kprev = [kprev_ref[g] for g in range(n_kv)]
    vprev = [vprev_ref[g] for g in range(n_kv)]
    for w in range(ATTN_ROWS // blk):
        rows = slice(w * blk, (w + 1) * blk)
        prev_ok = jnp.logical_and(si > qi, i > 0) if w == 0 else si > qi

        kcur, vcur = [], []
        for s in range(KV_WIDTH // LANES):
            ks = k_ref[rows, s * LANES:(s + 1) * LANES].astype(F32)
            vs = v_ref[rows, s * LANES:(s + 1) * LANES].astype(F32)
            kcur.extend(t.astype(BF16) for t in _dup_halves(ks, lane_lo))
            vcur.extend(t.astype(BF16) for t in _dup_halves(vs, lane_lo))

        def stacked_q(p):
            q = q_ref[rows, p * LANES:(p + 1) * LANES].astype(F32) * scale
            zero = jnp.zeros_like(q)
            return jnp.concatenate([jnp.where(lane_lo, q, zero), jnp.where(lane_lo, zero, q)], axis=0).astype(BF16)

        for p0 in range(0, PAIRS, ATTN_GROUP):
            ps = range(p0, p0 + ATTN_GROUP)
            kvh = {p: (2 * p) // group for p in ps}
            qs = {p: stacked_q(p) for p in ps}
            s = {p: jnp.where(cur_ok, _dot_nt(qs[p], kcur[kvh[p]]),
                              jnp.where(prev_ok, _dot_nt(qs[p], kprev[kvh[p]]), NEG_BIG)) for p in ps}
            sink = {p: jnp.where(row_lo, sink_ref[2 * p], sink_ref[2 * p + 1]) for p in ps}
            m = {p: jnp.maximum(jnp.max(s[p], axis=-1, keepdims=True), sink[p]) for p in ps}
            e = {p: jnp.exp(s[p] - m[p]) for p in ps}
            e2 = {p: jnp.concatenate([jnp.where(cur_ok, e[p], 0.0), jnp.where(cur_ok, 0.0, e[p])],
                                     axis=1).astype(BF16) for p in ps}
            acc = {p: jnp.dot(e2[p], jnp.concatenate([jnp.concatenate([vcur[kvh[p]], ones], axis=1),
                                                      jnp.concatenate([vprev[kvh[p]], ones], axis=1)], axis=0),
                              preferred_element_type=F32) for p in ps}
            for p in ps:
                num = acc[p][:, :LANES]
                den = acc[p][:, LANES:] + jnp.exp(sink[p] - m[p])
                o = jnp.where(lane_lo, num[:blk], num[blk:]) / jnp.where(lane_lo, den[:blk], den[blk:])
                ga = ga_ref[rows, p * LANES:(p + 1) * LANES].astype(F32)
                o_ref[rows, p * LANES:(p + 1) * LANES] = (o * _silu(ga)).astype(o_ref.dtype)
        kprev, vprev = kcur, vcur

    for g in range(n_kv):
        kprev_ref[g] = kprev[g]
        vprev_ref[g] = vprev[g]


def _attention(proj, sinks, batch, seq):
    M = proj.shape[0]
    nb = seq // ATTN_ROWS
    row = lambda b, i: b * nb + i
    n_kv = KV_WIDTH // HEAD_DIM
    return pl.pallas_call(
        _attn_kernel,
        out_shape=jax.ShapeDtypeStruct((M, ATTN_WIDTH), BF16),
        grid=(batch, nb),
        in_specs=[pl.BlockSpec(memory_space=pltpu.SMEM),
                  pl.BlockSpec((ATTN_ROWS, ATTN_WIDTH), lambda b, i: (row(b, i), COL_Q // ATTN_WIDTH)),
                  pl.BlockSpec((ATTN_ROWS, ATTN_WIDTH), lambda b, i: (row(b, i), COL_GA // ATTN_WIDTH)),
                  pl.BlockSpec((ATTN_ROWS, KV_WIDTH), lambda b, i: (row(b, i), COL_K // KV_WIDTH)),
                  pl.BlockSpec((ATTN_ROWS, KV_WIDTH), lambda b, i: (row(b, i), COL_V // KV_WIDTH))],
        out_specs=pl.BlockSpec((ATTN_ROWS, ATTN_WIDTH), lambda b, i: (row(b, i), 0)),
        scratch_shapes=[pltpu.VMEM((n_kv, WINDOW, LANES), BF16),
                        pltpu.VMEM((n_kv, WINDOW, LANES), BF16)],
        compiler_params=pltpu.CompilerParams(dimension_semantics=("parallel", "arbitrary"),
                                             vmem_limit_bytes=VMEM_LIMIT),
        name="swa_attn",
    )(sinks, proj, proj, proj, proj)


DECAY_SCALE = math.exp(-0.5)
RWKV_TT = 512


def _rwkv_kernel(r_ref, k_ref, v_ref, g_ref, wa_ref,
                 mu_r_ref, mu_k_ref, mu_v_ref, mu_wa_ref,
                 w0_ref, wup_ref, a0_ref, aup_ref, kk_ref, ka_ref, rk_ref, lnw_ref, lnb_ref,
                 o_ref, state_ref, last_ref, m_st, n_st, rhat_st, y0_st, post_st, *, nt, total):
    s = pl.program_id(0)
    TT = RWKV_TT
    L = CHUNK
    t_in = jnp.minimum(s, total - 1) % nt
    t_out = jnp.maximum(s - 1, 0) % nt
    cur = s % 2
    prv = 1 - cur

    @pl.when(s == 0)
    def _():
        for ref in (state_ref, last_ref, m_st, n_st, rhat_st, y0_st, post_st):
            ref[...] = jnp.zeros_like(ref)

    chunks = range(TT // L)
    cs = [slice(c * L, (c + 1) * L) for c in chunks]

    lane_lo = lax.broadcasted_iota(jnp.int32, (L, LANES), 1) < HEAD_DIM

    def diag2(x):
        zero = jnp.zeros_like(x)
        return jnp.concatenate([jnp.where(lane_lo, x, zero), jnp.where(lane_lo, zero, x)], axis=0).astype(BF16)

    def fold(x):
        return x[:L] + x[L:]

    carry = {"H": jnp.where(t_out == 0, 0.0, state_ref[...]), "ys": [], "next": 0}

    def advance():
        c = carry["next"]
        if c >= len(chunks):
            return
        H = diag2(carry["H"])
        carry["ys"].append(jnp.dot(rhat_st[prv, cs[c], :], H, preferred_element_type=F32) + y0_st[prv, cs[c], :])
        carry["H"] = jnp.dot(m_st[prv, c], H, preferred_element_type=F32) + n_st[prv, c]
        carry["next"] = c + 1

    row0 = lax.broadcasted_iota(jnp.int32, (8, LANES), 0) == 0
    first = t_in == 0

    def shifted(ref, slot, mu_ref):
        x = ref[...].astype(F32)
        carried = jnp.where(first, 0.0, last_ref[slot, 7:8, :])
        rolled = pltpu.roll(x, 1, axis=0)
        prev = jnp.concatenate([jnp.where(row0, carried, rolled[:8]), rolled[8:]], axis=0)
        last_ref[slot] = x[TT - 8:, :]
        return x + (prev - x) * mu_ref[...]

    r = shifted(r_ref, 0, mu_r_ref)
    k = shifted(k_ref, 1, mu_k_ref)
    v = shifted(v_ref, 2, mu_v_ref)
    wa = shifted(wa_ref, 3, mu_wa_ref)
    wd = wa[:, :LORA]
    ad = wa[:, LORA:]

    u = w0_ref[...] + _dot(jnp.tanh(wd), wup_ref[...])
    lw = -DECAY_SCALE / (1.0 + jnp.exp(-u))
    a = _sigmoid(a0_ref[...] + _dot(ad, aup_ref[...]))
    advance()

    lane_r = lax.broadcasted_iota(jnp.int32, (LANES, LANES), 0)
    lane_c = lax.broadcasted_iota(jnp.int32, (LANES, LANES), 1)
    head_blk = (lane_r // HEAD_DIM) == (lane_c // HEAD_DIM)
    ones_blk = jnp.where(head_blk, 1.0, 0.0).astype(BF16)

    kk = k * kk_ref[...]
    kk = kk * lax.rsqrt(jnp.maximum(_dot(kk * kk, ones_blk), 1e-24))
    k = k * ((1.0 - ka_ref[...]) + a * ka_ref[...])
    av = -kk
    bv = kk * a
    post_st[cur, 0] = _dot(r * k * rk_ref[...], ones_blk) * v
    post_st[cur, 1] = _silu(g_ref[...].astype(F32))

    TB = 4 * L
    tr = lax.broadcasted_iota(jnp.int32, (TB, TB), 0)
    tc = lax.broadcasted_iota(jnp.int32, (TB, TB), 1)
    tri = jnp.where(jnp.logical_and(tr // L == tc // L, tc <= tr), 1.0, 0.0).astype(BF16)
    cum = jnp.concatenate([_dot_exact_lhs(tri, lw[i * TB:(i + 1) * TB]) for i in range(TT // TB)], axis=0)
    advance()

    e_pos = jnp.exp(cum)
    e_neg = 1.0 / e_pos
    r_t = r * e_pos
    k_t = k * e_neg
    b_t = bv * e_neg
    a_t = av * jnp.exp(cum - lw)

    row = lax.broadcasted_iota(jnp.int32, (L, LANES), 0)
    col = lax.broadcasted_iota(jnp.int32, (L, LANES), 1) % HEAD_DIM
    strict = col < row
    incl = col <= row
    eye = col == row

    cum_l = [cum[sl][L - 1:L, :] for sl in cs]
    p_l = [jnp.exp(cum_l[c]) for c in chunks]
    bk_d = [jnp.concatenate([diag2(b_t[sl]), diag2(k_t[sl])], axis=0) for sl in cs]
    v_d = [diag2(v[sl]) for sl in cs]
    gram_a = [_dot_nt(a_t[cs[c]], bk_d[c]) for c in chunks]
    gram_r = [_dot_nt(r_t[cs[c]], bk_d[c]) for c in chunks]
    advance()
    a_ab = [jnp.where(strict, gram_a[c][:, :LANES], 0.0) for c in chunks]
    a_ak = [jnp.where(strict, gram_a[c][:, LANES:], 0.0) for c in chunks]
    a_rb = [jnp.where(incl, gram_r[c][:, :LANES], 0.0) for c in chunks]
    a_rk = [jnp.where(incl, gram_r[c][:, LANES:], 0.0) for c in chunks]
    inv = [jnp.where(eye, 1.0, 0.0) + a_ab[c] for c in chunks]
    pw = [_dot(a_ab[c], diag2(a_ab[c])) for c in chunks]
    w_s = [_dot(a_ak[c], v_d[c]) for c in chunks]
    advance()
    for _ in range(4):
        both = [_dot(pw[c], jnp.concatenate([diag2(pw[c]), diag2(inv[c])], axis=1)) for c in chunks]
        pw = [both[c][:, :LANES] for c in chunks]
        inv = [inv[c] + both[c][:, LANES:] for c in chunks]
        advance()
    inv = [inv[c] + _dot(pw[c], diag2(inv[c])) for c in chunks]
    sol = [_dot(inv[c], jnp.concatenate([diag2(a_t[cs[c]]), diag2(w_s[c])], axis=1)) for c in chunks]
    bmn = [_dot_tn(b_t[cs[c]] * p_l[c], sol[c]) for c in chunks]
    kv = [_dot_tn(k_t[cs[c]] * p_l[c], v[cs[c]]) for c in chunks]
    rb = [_dot(a_rb[c], jnp.concatenate([diag2(sol[c][:, :LANES]), diag2(sol[c][:, LANES:])], axis=1))
          for c in chunks]
    rkv = [_dot(a_rk[c], v_d[c]) for c in chunks]
    while carry["next"] < len(chunks):
        advance()

    state_ref[...] = carry["H"]
    y = jnp.concatenate(carry["ys"], axis=0)
    mean_blk = jnp.where(head_blk, 1.0 / HEAD_DIM, 0.0).astype(BF16)
    mean = _dot(y, mean_blk)
    yc = y - mean
    var = _dot(yc * yc, mean_blk)
    yn = yc * lax.rsqrt(var + GN_EPS) * lnw_ref[...] + lnb_ref[...]
    o_ref[...] = ((yn + post_st[prv, 0]) * post_st[prv, 1]).astype(o_ref.dtype)

    for c in chunks:
        m_st[cur, c] = (fold(jnp.where(head_blk, bmn[c][:, :LANES], 0.0))
                        + jnp.where(eye, p_l[c], 0.0)).astype(BF16)
        n_st[cur, c] = fold(jnp.where(head_blk, bmn[c][:, LANES:] + kv[c], 0.0))
        rhat_st[cur, cs[c], :] = (r_t[cs[c]] + rb[c][:, :LANES]).astype(BF16)
        y0_st[cur, cs[c], :] = rb[c][:, LANES:] + rkv[c]


def _rwkv(proj, mu, w0, w_up, a0, a_up, k_k, k_a, r_k, ln_w, ln_b, batch, seq):
    M = proj.shape[0]
    TT = RWKV_TT
    nt = seq // TT
    n_chunks = TT // CHUNK
    total = batch * PAIRS * nt

    def where(s):
        b, p, t = s // (PAIRS * nt), (s // nt) % PAIRS, s % nt
        return b * nt + t, p

    src = lambda s: where(jnp.minimum(s, total - 1))
    dst = lambda s: where(jnp.maximum(s - 1, 0))

    slab = lambda col: pl.BlockSpec((TT, LANES), lambda s: (src(s)[0], col // LANES + src(s)[1]))
    vec = pl.BlockSpec((1, LANES), lambda s: (0, src(s)[1]))
    vec0 = pl.BlockSpec((1, LANES), lambda s: (0, 0))
    vec_dst = pl.BlockSpec((1, LANES), lambda s: (0, dst(s)[1]))
    lora = pl.BlockSpec((LORA, LANES), lambda s: (0, src(s)[1]))
    mu_r, mu_k, mu_v, mu_wa = (mu[:, :1024], mu[:, 1024:2048], mu[:, 2048:3072], mu[:, 3072:])
    return pl.pallas_call(
        functools.partial(_rwkv_kernel, nt=nt, total=total),
        out_shape=jax.ShapeDtypeStruct((M, RWKV_WIDTH), BF16),
        grid=(total + 1,),
        in_specs=[slab(COL_R), slab(COL_RK), slab(COL_RV), slab(COL_RG),
                  pl.BlockSpec((TT, LANES), lambda s: (src(s)[0], COL_WA // LANES)),
                  vec, vec, vec, vec0,
                  vec, lora, vec, lora, vec, vec, vec, vec_dst, vec_dst],
        out_specs=pl.BlockSpec((TT, LANES), lambda s: dst(s)),
        scratch_shapes=[pltpu.VMEM((CHUNK, LANES), F32),
                        pltpu.VMEM((4, 8, LANES), F32),
                        pltpu.VMEM((2, n_chunks, CHUNK, LANES), BF16),
                        pltpu.VMEM((2, n_chunks, CHUNK, LANES), F32),
                        pltpu.VMEM((2, TT, LANES), BF16),
                        pltpu.VMEM((2, TT, LANES), F32),
                        pltpu.VMEM((2, 2, TT, LANES), F32)],
        compiler_params=pltpu.CompilerParams(dimension_semantics=("arbitrary",),
                                             vmem_limit_bytes=VMEM_LIMIT),
        name="rwkv7_mix",
    )(proj, proj, proj, proj, proj, mu_r, mu_k, mu_v, mu_wa,
      w0, w_up, a0, a_up, k_k, k_a, r_k, ln_w, ln_b)


RWKV_PIECE = 2 * CHUNK
PREP_FIRST = 4


def _rwkv3_kernel(r_ref, k_ref, v_ref, g_ref, wa_ref,
                  mu_r_ref, mu_k_ref, mu_v_ref, mu_wa_ref,
                  w0_ref, wup_ref, a0_ref, aup_ref, kk_ref, ka_ref, rk_ref, lnw_ref, lnb_ref,
                  o_ref, state_ref, last_ref,
                  pa_st, pr_st, pb_st, pk_st, pv_st, pbb_st, pkb_st, ppl_st,
                  m_st, n_st, rhat_st, y0_st, post_st, *, nt, total):
    s = pl.program_id(0)
    TT, L, P = RWKV_TT, CHUNK, RWKV_PIECE
    t_in = jnp.minimum(s, total - 1) % nt
    t_out = jnp.maximum(s - 2, 0) % nt
    cur = s % 2
    prv = 1 - cur
    post_w = s % 3
    post_r = (s + 1) % 3

    @pl.when(s == 0)
    def _():
        for ref in (state_ref, last_ref, pa_st, pr_st, pb_st, pk_st, pv_st, pbb_st, pkb_st, ppl_st,
                    m_st, n_st, rhat_st, y0_st, post_st):
            ref[...] = jnp.zeros_like(ref)

    chunks = range(TT // L)
    cs = [slice(c * L, (c + 1) * L) for c in chunks]

    lane_r = lax.broadcasted_iota(jnp.int32, (LANES, LANES), 0)
    lane_c = lax.broadcasted_iota(jnp.int32, (LANES, LANES), 1)
    head_blk = (lane_r // HEAD_DIM) == (lane_c // HEAD_DIM)
    ones_blk = jnp.where(head_blk, 1.0, 0.0).astype(BF16)

    lane_lo = lax.broadcasted_iota(jnp.int32, (L, LANES), 1) < HEAD_DIM

    def diag2(x):
        zero = jnp.zeros_like(x)
        return jnp.concatenate([jnp.where(lane_lo, x, zero), jnp.where(lane_lo, zero, x)], axis=0).astype(BF16)

    def fold(x):
        return x[:L] + x[L:]

    carry = {"H": jnp.where(t_out == 0, 0.0, state_ref[...]), "ys": [], "next": 0}

    def link():
        c = carry["next"]
        if c >= len(chunks):
            return
        H = diag2(carry["H"])
        carry["ys"].append(jnp.dot(rhat_st[prv, cs[c], :], H, preferred_element_type=F32) + y0_st[prv, cs[c], :])
        carry["H"] = jnp.dot(m_st[prv, c], H, preferred_element_type=F32) + n_st[prv, c]
        carry["next"] = c + 1

    n_pieces = TT // P
    first = t_in == 0
    row0 = lax.broadcasted_iota(jnp.int32, (P, LANES), 0) == 0
    pr_i = lax.broadcasted_iota(jnp.int32, (P, P), 0)
    pc_i = lax.broadcasted_iota(jnp.int32, (P, P), 1)
    tri = jnp.where(jnp.logical_and(pr_i // L == pc_i // L, pc_i <= pr_i), 1.0, 0.0).astype(BF16)
    todo = {"piece": 0}

    def prepare():
        j = todo["piece"]
        if j >= n_pieces:
            return
        todo["piece"] = j + 1
        rows = slice(j * P, (j + 1) * P)

        def shifted(ref, slot, mu_ref):
            x = ref[rows, :].astype(F32)
            if j == 0:
                before = jnp.where(first, 0.0, last_ref[slot, 7:8, :])
            else:
                before = ref[j * P - 16:j * P, :].astype(F32)[15:16, :]
            prev = jnp.where(row0, before, pltpu.roll(x, 1, axis=0))
            if j == n_pieces - 1:
                last_ref[slot] = x[P - 8:, :]
            return x + (prev - x) * mu_ref[...]

        r = shifted(r_ref, 0, mu_r_ref)
        k = shifted(k_ref, 1, mu_k_ref)
        v = shifted(v_ref, 2, mu_v_ref)
        wa = shifted(wa_ref, 3, mu_wa_ref)
        z = -(w0_ref[...] + _dot(jnp.tanh(wa[:, :LORA]), wup_ref[...]))
        softplus = jnp.maximum(z, 0.0) + jnp.log(1.0 + jnp.exp(-jnp.abs(z)))
        lw = -jnp.exp(-softplus - 0.5)
        a = _sigmoid(a0_ref[...] + _dot(wa[:, LORA:], aup_ref[...]))
        kk = k * kk_ref[...]
        kk = kk * lax.rsqrt(jnp.maximum(_dot(kk * kk, ones_blk), 1e-24))
        k = k * (1.0 + (a - 1.0) * ka_ref[...])
        bv = kk * a
        post_st[post_w, 0, rows, :] = _dot(r * k * rk_ref[...], ones_blk) * v
        post_st[post_w, 1, rows, :] = _silu(g_ref[rows, :].astype(F32))
        cum = _dot_exact_lhs(tri, lw)
        e_neg = jnp.exp(-cum)
        pr_st[cur, rows, :] = r * jnp.exp(cum)
        pa_st[cur, rows, :] = (-kk * jnp.exp(cum - lw)).astype(BF16)
        pb_st[cur, rows, :] = (bv * e_neg).astype(BF16)
        pk_st[cur, rows, :] = (k * e_neg).astype(BF16)
        pv_st[cur, rows, :] = v.astype(BF16)
        for h in range(P // L):
            sub = slice(h * L, (h + 1) * L)
            dst_rows = cs[j * (P // L) + h]
            cum_l = cum[(h + 1) * L - 1:(h + 1) * L, :]
            to_end = jnp.exp(cum_l - cum[sub])
            pbb_st[cur, dst_rows, :] = (bv[sub] * to_end).astype(BF16)
            pkb_st[cur, dst_rows, :] = (k[sub] * to_end).astype(BF16)
            ppl_st[cur, j * (P // L) + h] = jnp.broadcast_to(jnp.exp(cum_l), (8, LANES))

    row = lax.broadcasted_iota(jnp.int32, (L, LANES), 0)
    col = lax.broadcasted_iota(jnp.int32, (L, LANES), 1) % HEAD_DIM
    strict = col < row
    incl = col <= row
    eye = col == row

    for _ in range(PREP_FIRST):
        prepare()
    a_t = [pa_st[prv, sl, :] for sl in cs]
    r_t = [pr_st[prv, sl, :] for sl in cs]
    v_c = [pv_st[prv, sl, :] for sl in cs]
    bk_d = [jnp.concatenate([diag2(pb_st[prv, sl, :]), diag2(pk_st[prv, sl, :])], axis=0) for sl in cs]
    v_d = [diag2(v_c[c]) for c in chunks]
    gram_a = [_dot_nt(a_t[c], bk_d[c]) for c in chunks]
    gram_r = [_dot_nt(r_t[c], bk_d[c]) for c in chunks]
    prepare()
    link()
    a_ab = [jnp.where(strict, gram_a[c][:, :LANES], 0.0) for c in chunks]
    a_ak = [jnp.where(strict, gram_a[c][:, LANES:], 0.0) for c in chunks]
    a_rb = [jnp.where(incl, gram_r[c][:, :LANES], 0.0) for c in chunks]
    a_rk = [jnp.where(incl, gram_r[c][:, LANES:], 0.0) for c in chunks]
    inv = [jnp.where(eye, 1.0, 0.0) + a_ab[c] for c in chunks]
    pw = [_dot(a_ab[c], diag2(a_ab[c])) for c in chunks]
    w_s = [_dot(a_ak[c], v_d[c]) for c in chunks]
    prepare()
    link()
    for _ in range(4):
        both = [_dot(pw[c], jnp.concatenate([diag2(pw[c]), diag2(inv[c])], axis=1)) for c in chunks]
        pw = [both[c][:, :LANES] for c in chunks]
        inv = [inv[c] + both[c][:, LANES:] for c in chunks]
        prepare()
        link()
    inv = [inv[c] + _dot(pw[c], diag2(inv[c])) for c in chunks]
    sol = [_dot(inv[c], jnp.concatenate([diag2(a_t[c]), diag2(w_s[c])], axis=1)) for c in chunks]
    link()
    bmn = [_dot_tn(pbb_st[prv, cs[c], :], sol[c]) for c in chunks]
    kv = [_dot_tn(pkb_st[prv, cs[c], :], v_c[c]) for c in chunks]
    rb = [_dot(a_rb[c], jnp.concatenate([diag2(sol[c][:, :LANES]), diag2(sol[c][:, LANES:])], axis=1))
          for c in chunks]
    rkv = [_dot(a_rk[c], v_d[c]) for c in chunks]
    while todo["piece"] < n_pieces:
        prepare()
    while carry["next"] < len(chunks):
        link()

    state_ref[...] = carry["H"]
    y = jnp.concatenate(carry["ys"], axis=0)
    mean_blk = jnp.where(head_blk, 1.0 / HEAD_DIM, 0.0).astype(BF16)
    mean = _dot(y, mean_blk)
    yc = y - mean
    var = _dot(yc * yc, mean_blk)
    yn = yc * lax.rsqrt(var + GN_EPS) * lnw_ref[...] + lnb_ref[...]
    o_ref[...] = ((yn + post_st[post_r, 0]) * post_st[post_r, 1]).astype(o_ref.dtype)

    for c in chunks:
        m_st[cur, c] = (fold(jnp.where(head_blk, bmn[c][:, :LANES], 0.0))
                        + jnp.where(eye, ppl_st[prv, c, 0:1, :], 0.0)).astype(BF16)
        n_st[cur, c] = fold(jnp.where(head_blk, bmn[c][:, LANES:] + kv[c], 0.0))
        rhat_st[cur, cs[c], :] = (r_t[c] + rb[c][:, :LANES]).astype(BF16)
        y0_st[cur, cs[c], :] = rb[c][:, LANES:] + rkv[c]


def _rwkv3(proj, mu, w0, w_up, a0, a_up, k_k, k_a, r_k, ln_w, ln_b, batch, seq):
    M = proj.shape[0]
    TT = RWKV_TT
    nt = seq // TT
    n_chunks = TT // CHUNK
    total = batch * PAIRS * nt

    def where(s):
        b, p, t = s // (PAIRS * nt), (s // nt) % PAIRS, s % nt
        return b * nt + t, p

    src = lambda s: where(jnp.minimum(s, total - 1))
    dst = lambda s: where(jnp.maximum(s - 2, 0))

    slab = lambda col: pl.BlockSpec((TT, LANES), lambda s: (src(s)[0], col // LANES + src(s)[1]))
    vec = pl.BlockSpec((1, LANES), lambda s: (0, src(s)[1]))
    vec0 = pl.BlockSpec((1, LANES), lambda s: (0, 0))
    vec_dst = pl.BlockSpec((1, LANES), lambda s: (0, dst(s)[1]))
    lora = pl.BlockSpec((LORA, LANES), lambda s: (0, src(s)[1]))
    mu_r, mu_k, mu_v, mu_wa = (mu[:, :1024], mu[:, 1024:2048], mu[:, 2048:3072], mu[:, 3072:])
    operand = lambda dt: pltpu.VMEM((2, TT, LANES), dt)
    return pl.pallas_call(
        functools.partial(_rwkv3_kernel, nt=nt, total=total),
        out_shape=jax.ShapeDtypeStruct((M, RWKV_WIDTH), BF16),
        grid=(total + 2,),
        in_specs=[slab(COL_R), slab(COL_RK), slab(COL_RV), slab(COL_RG),
                  pl.BlockSpec((TT, LANES), lambda s: (src(s)[0], COL_WA // LANES)),
                  vec, vec, vec, vec0,
                  vec, lora, vec, lora, vec, vec, vec, vec_dst, vec_dst],
        out_specs=pl.BlockSpec((TT, LANES), lambda s: dst(s)),
        scratch_shapes=[pltpu.VMEM((CHUNK, LANES), F32),
                        pltpu.VMEM((4, 8, LANES), F32),
                        operand(BF16), operand(F32), operand(BF16), operand(BF16), operand(BF16),
                        operand(BF16), operand(BF16),
                        pltpu.VMEM((2, n_chunks, 8, LANES), F32),
                        pltpu.VMEM((2, n_chunks, CHUNK, LANES), BF16),
                        pltpu.VMEM((2, n_chunks, CHUNK, LANES), F32),
                        pltpu.VMEM((2, TT, LANES), BF16),
                        pltpu.VMEM((2, TT, LANES), F32),
                        pltpu.VMEM((3, 2, TT, LANES), F32)],
        compiler_params=pltpu.CompilerParams(dimension_semantics=("arbitrary",),
                                             vmem_limit_bytes=VMEM_LIMIT),
        name="rwkv7_mix",
    )(proj, proj, proj, proj, proj, mu_r, mu_k, mu_v, mu_wa,
      w0, w_up, a0, a_up, k_k, k_a, r_k, ln_w, ln_b)


def _out_proj_kernel(x_ref, ya_ref, yr_ref, wa_ref, wr_ref, g_ref, gate_ref, o_ref):
    mix = (jnp.dot(ya_ref[...], wa_ref[...], preferred_element_type=F32)
           + jnp.dot(yr_ref[...], wr_ref[...], preferred_element_type=F32))
    inv = lax.rsqrt(jnp.mean(mix * mix, axis=-1, keepdims=True) + RMS_EPS)
    o_ref[...] = x_ref[...] + gate_ref[0] * ((mix * inv) * g_ref[...])


def _out_proj(x2, ya, yr, w_a, w_r, g, gate, seq):
    M, D = x2.shape
    tm = 512
    per_b = seq // tm
    half = ya.shape[1]
    return pl.pallas_call(
        _out_proj_kernel,
        out_shape=jax.ShapeDtypeStruct((M, D), F32),
        grid=(M // tm,),
        in_specs=[pl.BlockSpec((tm, D), lambda i: (i, 0)),
                  pl.BlockSpec((tm, half), lambda i: (i, 0)),
                  pl.BlockSpec((tm, half), lambda i: (i, 0)),
                  pl.BlockSpec((half, D), lambda i: (0, 0)),
                  pl.BlockSpec((half, D), lambda i: (0, 0)),
                  pl.BlockSpec((1, D), lambda i: (0, 0)),
                  pl.BlockSpec((1, 1, D), lambda i: (i // per_b, 0, 0))],
        out_specs=pl.BlockSpec((tm, D), lambda i: (i, 0)),
        compiler_params=pltpu.CompilerParams(dimension_semantics=("parallel",),
                                             vmem_limit_bytes=VMEM_LIMIT),
        name="out_proj",
    )(x2, ya, yr, w_a, w_r, g, gate)


def _pack_w_in(w):
    q, ka, va, ga = w[:, 0:1024], w[:, 1024:1280], w[:, 1280:1536], w[:, 1536:2560]
    rkv, wa, gr = w[:, 2560:5632], w[:, 5632:5760], w[:, 5760:6784]
    pad = jnp.zeros((w.shape[0], IN_COLS_PAD - IN_COLS), w.dtype)
    return jnp.concatenate([q, ga, ka, va, rkv, gr, wa, pad], axis=1).astype(BF16)


def kernel(x, c, w_ada, b_ada, pre_norm_g, post_norm_g, w_in, w_out, attn_sinks, rwkv_mu, rwkv_w0,
           rwkv_w_up, rwkv_a0, rwkv_a_up, rwkv_k_k, rwkv_k_a, rwkv_r_k, rwkv_ln_w, rwkv_ln_b):
    B, T, D = x.shape
    depth = w_ada.shape[0]
    x2 = x.reshape(B * T, D)
    for l in range(depth):
        mod = _adaln(c, w_ada[l], b_ada[l][None, :])
        shift, scale, gate = (mod[:, i * D:(i + 1) * D].reshape(B, 1, D) for i in range(3))
        proj = _in_proj(x2, pre_norm_g[l][None, :], scale, shift, _pack_w_in(w_in[l]), T)
        y_attn = _attention(proj, attn_sinks[l], B, T)
        y_rwkv = _rwkv(proj, rwkv_mu[l][None, :], rwkv_w0[l][None, :], rwkv_w_up[l].astype(BF16),
                       rwkv_a0[l][None, :], rwkv_a_up[l].astype(BF16), rwkv_k_k[l][None, :],
                       rwkv_k_a[l][None, :], rwkv_r_k[l].reshape(1, RWKV_WIDTH),
                       rwkv_ln_w[l][None, :], rwkv_ln_b[l][None, :], B, T)
        w_o = w_out[l].astype(BF16)
        x2 = _out_proj(x2, y_attn, y_rwkv, w_o[:ATTN_WIDTH], w_o[ATTN_WIDTH:],
                       post_norm_g[l][None, :], gate, T)
    return x2.reshape(B, T, D)
```

```python
import functools
import math

import jax
import jax.numpy as jnp
from jax import lax
from jax.experimental import pallas as pl
from jax.experimental.pallas import tpu as pltpu

D_MODEL = 2048
HEAD_DIM = 64
ATTN_WIDTH = 1024
ATTN_Q_HEADS = 16
KV_WIDTH = 256
WINDOW = 128
RWKV_WIDTH = 1024
LORA = 64
IN_COLS = 6784
RMS_EPS = 1e-6
GN_EPS = 64e-5
NEG_BIG = -1e30

LANES = 128
PAIRS = RWKV_WIDTH // LANES
CHUNK = 64

COL_Q = 0
COL_K = 1024
COL_V = 1280
COL_GA = 1536
COL_R = 2560
COL_RK = 3584
COL_RV = 4608
COL_WA = 5632
COL_RG = 5760
GA_BLOCK = 512
IN_COLS_PAD = 6912

VMEM_LIMIT = 56 * 1024 * 1024

F32 = jnp.float32
BF16 = jnp.bfloat16


def _dot(a, b):
    return jnp.dot(a.astype(BF16), b.astype(BF16), preferred_element_type=F32)


def _dot_nt(a, b):
    return lax.dot_general(a.astype(BF16), b.astype(BF16), (((1,), (1,)), ((), ())),
                           preferred_element_type=F32)


def _dot_tn(a, b):
    return lax.dot_general(a.astype(BF16), b.astype(BF16), (((0,), (0,)), ((), ())),
                           preferred_element_type=F32)


def _split3(x):
    hi = x.astype(BF16)
    r1 = x - hi.astype(F32)
    mid = r1.astype(BF16)
    lo = (r1 - mid.astype(F32)).astype(BF16)
    return hi, mid, lo


def _split2(x):
    hi = x.astype(BF16)
    return hi, (x - hi.astype(F32)).astype(BF16)


def _dot_exact_rhs(a, b_bf16):
    hi, lo = _split2(a)
    return jnp.dot(jnp.concatenate([hi, lo], axis=1), jnp.concatenate([b_bf16, b_bf16], axis=0),
                   preferred_element_type=F32)


def _dot_exact_lhs(a_bf16, b):
    hi, lo = _split2(b)
    return jnp.dot(a_bf16, hi, preferred_element_type=F32) + jnp.dot(a_bf16, lo, preferred_element_type=F32)


def _sigmoid(x):
    return 1.0 / (1.0 + jnp.exp(-x))


def _silu(x):
    return x * _sigmoid(x)


def _adaln_kernel(c_ref, w_ref, b_ref, o_ref):
    s = _silu(c_ref[...])
    s_hi, s_lo = _split2(s)
    w_hi, w_lo = _split2(w_ref[...])
    d = lambda p, q: jnp.dot(p, q, preferred_element_type=F32)
    o_ref[...] = d(s_hi, w_hi) + (d(s_hi, w_lo) + d(s_lo, w_hi)) + b_ref[...]


def _adaln(c, w, b):
    B, D = c.shape
    N = w.shape[1]
    tn = 512
    return pl.pallas_call(
        _adaln_kernel,
        out_shape=jax.ShapeDtypeStruct((B, N), F32),
        grid=(N // tn,),
        in_specs=[pl.BlockSpec((B, D), lambda j: (0, 0)),
                  pl.BlockSpec((D, tn), lambda j: (0, j)),
                  pl.BlockSpec((1, tn), lambda j: (0, j))],
        out_specs=pl.BlockSpec((B, tn), lambda j: (0, j)),
        compiler_params=pltpu.CompilerParams(dimension_semantics=("arbitrary",),
                                             vmem_limit_bytes=VMEM_LIMIT),
        name="adaln_mod",
    )(c, w, b)


NORM_ROWS = 16


def _in_proj_kernel(x_ref, g_ref, scale_ref, shift_ref, w_ref, o_ref, h_ref):
    @pl.when(pl.program_id(1) == 0)
    def _():
        gain = g_ref[...] * (1.0 + scale_ref[0])
        shift = shift_ref[0]

        def rows(c, carry):
            sl = pl.ds(pl.multiple_of(c * NORM_ROWS, NORM_ROWS), NORM_ROWS)
            x = x_ref[sl, :]
            inv = lax.rsqrt(jnp.mean(x * x, axis=-1, keepdims=True) + RMS_EPS)
            h_ref[sl, :] = ((x * inv) * gain + shift).astype(BF16)
            return carry

        lax.fori_loop(0, x_ref.shape[0] // NORM_ROWS, rows, 0, unroll=8)

    o_ref[...] = jnp.dot(h_ref[...], w_ref[...], preferred_element_type=F32).astype(o_ref.dtype)


def _in_proj(x2, g, scale, shift, w_packed, seq):
    M, D = x2.shape
    NP = w_packed.shape[1]
    tm, tn = 1024, 768
    per_b = seq // tm
    return pl.pallas_call(
        _in_proj_kernel,
        out_shape=jax.ShapeDtypeStruct((M, NP), BF16),
        grid=(M // tm, NP // tn),
        in_specs=[pl.BlockSpec((tm, D), lambda i, j: (i, 0)),
                  pl.BlockSpec((1, D), lambda i, j: (0, 0)),
                  pl.BlockSpec((1, 1, D), lambda i, j: (i // per_b, 0, 0)),
                  pl.BlockSpec((1, 1, D), lambda i, j: (i // per_b, 0, 0)),
                  pl.BlockSpec((D, tn), lambda i, j: (0, j))],
        out_specs=pl.BlockSpec((tm, tn), lambda i, j: (i, j)),
        scratch_shapes=[pltpu.VMEM((tm, D), BF16)],
        compiler_params=pltpu.CompilerParams(dimension_semantics=("parallel", "arbitrary"),
                                             vmem_limit_bytes=VMEM_LIMIT),
        name="in_proj",
    )(x2, g, scale, shift, w_packed)


ATTN_GROUP = 4
ATTN_ROWS = 2 * WINDOW


def _dup_halves(slab, lane_lo):
    swapped = pltpu.roll(slab, HEAD_DIM, axis=1)
    return jnp.where(lane_lo, slab, swapped), jnp.where(lane_lo, swapped, slab)


def _attn_kernel(sink_ref, q_ref, ga_lo_ref, ga_hi_ref, k_ref, v_ref, o_ref, kprev_ref, vprev_ref):
    i = pl.program_id(1)
    blk = WINDOW
    lane_lo = lax.broadcasted_iota(jnp.int32, (blk, LANES), 1) < HEAD_DIM

    @pl.when(i == 0)
    def _():
        kprev_ref[...] = jnp.zeros_like(kprev_ref)
        vprev_ref[...] = jnp.zeros_like(vprev_ref)

    ones = jnp.ones((blk, LANES), BF16)
    qi = lax.broadcasted_iota(jnp.int32, (2 * blk, blk), 0) % blk
    si = lax.broadcasted_iota(jnp.int32, (2 * blk, blk), 1)
    cur_ok = si <= qi
    scale = HEAD_DIM ** -0.5
    group = ATTN_Q_HEADS // (KV_WIDTH // HEAD_DIM)
    n_kv = KV_WIDTH // HEAD_DIM
    row_lo = lax.broadcasted_iota(jnp.int32, (2 * blk, 1), 0) < blk

    kprev = [kprev_ref[g] for g in range(n_kv)]
    vprev = [vprev_ref[g] for g in range(n_kv)]
    for w in range(ATTN_ROWS // blk):
        rows = slice(w * blk, (w + 1) * blk)
        prev_ok = jnp.logical_and(si > qi, i > 0) if w == 0 else si > qi

        kcur, vcur = [], []
        for s in range(KV_WIDTH // LANES):
            ks = k_ref[rows, s * LANES:(s + 1) * LANES].astype(F32)
            vs = v_ref[rows, s * LANES:(s + 1) * LANES].astype(F32)
            kcur.extend(t.astype(BF16) for t in _dup_halves(ks, lane_lo))
            vcur.extend(t.astype(BF16) for t in _dup_halves(vs, lane_lo))

        def stacked_q(p):
            q = q_ref[rows, p * LANES:(p + 1) * LANES].astype(F32) * scale
            zero = jnp.zeros_like(q)
            return jnp.concatenate([jnp.where(lane_lo, q, zero), jnp.where(lane_lo, zero, q)], axis=0).astype(BF16)

        for p0 in range(0, PAIRS, ATTN_GROUP):
            ps = range(p0, p0 + ATTN_GROUP)
            kvh = {p: (2 * p) // group for p in ps}
            qs = {p: stacked_q(p) for p in ps}
            s = {p: jnp.where(cur_ok, _dot_nt(qs[p], kcur[kvh[p]]),
                              jnp.where(prev_ok, _dot_nt(qs[p], kprev[kvh[p]]), NEG_BIG)) for p in ps}
            sink = {p: jnp.where(row_lo, sink_ref[2 * p], sink_ref[2 * p + 1]) for p in ps}
            m = {p: jnp.maximum(jnp.max(s[p], axis=-1, keepdims=True), sink[p]) for p in ps}
            e = {p: jnp.exp(s[p] - m[p]) for p in ps}
            e2 = {p: jnp.concatenate([jnp.where(cur_ok, e[p], 0.0), jnp.where(cur_ok, 0.0, e[p])],
                                     axis=1).astype(BF16) for p in ps}
            acc = {p: jnp.dot(e2[p], jnp.concatenate([jnp.concatenate([vcur[kvh[p]], ones], axis=1),
                                                      jnp.concatenate([vprev[kvh[p]], ones], axis=1)], axis=0),
                              preferred_element_type=F32) for p in ps}
            for p in ps:
                num = acc[p][:, :LANES]
                den = acc[p][:, LANES:] + jnp.exp(sink[p] - m[p])
                o = jnp.where(lane_lo, num[:blk], num[blk:]) / jnp.where(lane_lo, den[:blk], den[blk:])
                ga_ref, gp = (ga_lo_ref, p) if p < PAIRS // 2 else (ga_hi_ref, p - PAIRS // 2)
                ga = ga_ref[rows, gp * LANES:(gp + 1) * LANES].astype(F32)
                o_ref[rows, p * LANES:(p + 1) * LANES] = (o * _silu(ga)).astype(o_ref.dtype)
        kprev, vprev = kcur, vcur

    for g in range(n_kv):
        kprev_ref[g] = kprev[g]
        vprev_ref[g] = vprev[g]


def _attention(proj, sinks, batch, seq):
    M = proj.shape[0]
    nb = seq // ATTN_ROWS
    row = lambda b, i: b * nb + i
    n_kv = KV_WIDTH // HEAD_DIM
    return pl.pallas_call(
        _attn_kernel,
        out_shape=jax.ShapeDtypeStruct((M, ATTN_WIDTH), BF16),
        grid=(batch, nb),
        in_specs=[pl.BlockSpec(memory_space=pltpu.SMEM),
                  pl.BlockSpec((ATTN_ROWS, ATTN_WIDTH), lambda b, i: (row(b, i), COL_Q // ATTN_WIDTH)),
                  pl.BlockSpec((ATTN_ROWS, GA_BLOCK), lambda b, i: (row(b, i), COL_GA // GA_BLOCK)),
                  pl.BlockSpec((ATTN_ROWS, GA_BLOCK), lambda b, i: (row(b, i), COL_GA // GA_BLOCK + 1)),
                  pl.BlockSpec((ATTN_ROWS, KV_WIDTH), lambda b, i: (row(b, i), COL_K // KV_WIDTH)),
                  pl.BlockSpec((ATTN_ROWS, KV_WIDTH), lambda b, i: (row(b, i), COL_V // KV_WIDTH))],
        out_specs=pl.BlockSpec((ATTN_ROWS, ATTN_WIDTH), lambda b, i: (row(b, i), 0)),
        scratch_shapes=[pltpu.VMEM((n_kv, WINDOW, LANES), BF16),
                        pltpu.VMEM((n_kv, WINDOW, LANES), BF16)],
        compiler_params=pltpu.CompilerParams(dimension_semantics=("parallel", "arbitrary"),
                                             vmem_limit_bytes=VMEM_LIMIT),
        name="swa_attn",
    )(sinks, proj, proj, proj, proj, proj)


DECAY_SCALE = math.exp(-0.5)
RWKV_TT = 1024
LINK_PERIOD = 13


def _rwkv_kernel(r_ref, k_ref, v_ref, g_ref, wa_ref,
                 mu_r_ref, mu_k_ref, mu_v_ref, mu_wa_ref,
                 w0_ref, wup_ref, a0_ref, aup_ref, kk_ref, ka_ref, rk_ref, lnw_ref, lnb_ref,
                 o_ref, state_ref, last_ref, m_st, n_st, rhat_st, y0_st, post_st, *, nt, total):
    s = pl.program_id(0)
    TT = RWKV_TT
    L = CHUNK
    t_in = jnp.minimum(s, total - 1) % nt
    t_out = jnp.maximum(s - 1, 0) % nt
    cur = s % 2
    prv = 1 - cur

    @pl.when(s == 0)
    def _():
        for ref in (state_ref, last_ref, m_st, n_st, rhat_st, y0_st, post_st):
            ref[...] = jnp.zeros_like(ref)

    chunks = range(TT // L)
    cs = [slice(c * L, (c + 1) * L) for c in chunks]

    lane_lo = lax.broadcasted_iota(jnp.int32, (L, LANES), 1) < HEAD_DIM

    def diag2(x):
        zero = jnp.zeros_like(x)
        return jnp.concatenate([jnp.where(lane_lo, x, zero), jnp.where(lane_lo, zero, x)], axis=0).astype(BF16)

    def fold(x):
        return x[:L] + x[L:]

    carry = {"H": jnp.where(t_out == 0, 0.0, state_ref[...]), "ys": [], "next": 0}

    def advance():
        c = carry["next"]
        if c >= len(chunks):
            return
        H = diag2(carry["H"])
        carry["ys"].append(jnp.dot(rhat_st[prv, cs[c], :], H, preferred_element_type=F32) + y0_st[prv, cs[c], :])
        carry["H"] = jnp.dot(m_st[prv, c], H, preferred_element_type=F32) + n_st[prv, c]
        carry["next"] = c + 1

    row0 = lax.broadcasted_iota(jnp.int32, (8, LANES), 0) == 0
    first = t_in == 0

    def shifted(ref, slot, mu_ref):
        x = ref[...].astype(F32)
        carried = jnp.where(first, 0.0, last_ref[slot, 7:8, :])
        rolled = pltpu.roll(x, 1, axis=0)
        prev = jnp.concatenate([jnp.where(row0, carried, rolled[:8]), rolled[8:]], axis=0)
        last_ref[slot] = x[TT - 8:, :]
        return x + (prev - x) * mu_ref[...]

    r = shifted(r_ref, 0, mu_r_ref)
    k = shifted(k_ref, 1, mu_k_ref)
    v = shifted(v_ref, 2, mu_v_ref)
    wa = shifted(wa_ref, 3, mu_wa_ref)
    wd = wa[:, :LORA]
    ad = wa[:, LORA:]

    u = w0_ref[...] + _dot(jnp.tanh(wd), wup_ref[...])
    lw = -DECAY_SCALE / (1.0 + jnp.exp(-u))
    a = _sigmoid(a0_ref[...] + _dot(ad, aup_ref[...]))
    advance()

    lane_r = lax.broadcasted_iota(jnp.int32, (LANES, LANES), 0)
    lane_c = lax.broadcasted_iota(jnp.int32, (LANES, LANES), 1)
    head_blk = (lane_r // HEAD_DIM) == (lane_c // HEAD_DIM)
    ones_blk = jnp.where(head_blk, 1.0, 0.0).astype(BF16)

    kk = k * kk_ref[...]
    kk = kk * lax.rsqrt(jnp.maximum(_dot(kk * kk, ones_blk), 1e-24))
    k = k * ((1.0 - ka_ref[...]) + a * ka_ref[...])
    av = -kk
    bv = kk * a
    post_st[cur, 0] = _dot(r * k * rk_ref[...], ones_blk) * v
    post_st[cur, 1] = _silu(g_ref[...].astype(F32))

    TB = 4 * L
    tr = lax.broadcasted_iota(jnp.int32, (TB, TB), 0)
    tc = lax.broadcasted_iota(jnp.int32, (TB, TB), 1)
    tri = jnp.where(jnp.logical_and(tr // L == tc // L, tc <= tr), 1.0, 0.0).astype(BF16)
    cum = jnp.concatenate([_dot_exact_lhs(tri, lw[i * TB:(i + 1) * TB]) for i in range(TT // TB)], axis=0)
    advance()

    e_pos = jnp.exp(cum)
    e_neg = 1.0 / e_pos
    r_t = r * e_pos
    k_t = k * e_neg
    b_t = bv * e_neg
    a_t = av * jnp.exp(cum - lw)

    row = lax.broadcasted_iota(jnp.int32, (L, LANES), 0)
    col = lax.broadcasted_iota(jnp.int32, (L, LANES), 1) % HEAD_DIM
    strict = col < row
    incl = col <= row
    eye = col == row

    ticks = {"n": 0}

    def each(fn):
        out = []
        for c in chunks:
            out.append(fn(c))
            ticks["n"] += 1
            if ticks["n"] % LINK_PERIOD == 0:
                advance()
        return out

    cum_l = [cum[sl][L - 1:L, :] for sl in cs]
    p_l = [jnp.exp(cum_l[c]) for c in chunks]
    bk_d = [jnp.concatenate([diag2(b_t[sl]), diag2(k_t[sl])], axis=0) for sl in cs]
    v_d = [diag2(v[sl]) for sl in cs]
    gram = each(lambda c: _dot_nt(jnp.concatenate([a_t[cs[c]], r_t[cs[c]]], axis=0), bk_d[c]))
    a_ab = [jnp.where(strict, gram[c][:L, :LANES], 0.0) for c in chunks]
    a_ak = [jnp.where(strict, gram[c][:L, LANES:], 0.0) for c in chunks]
    a_rb = [jnp.where(incl, gram[c][L:, :LANES], 0.0) for c in chunks]
    a_rk = [jnp.where(incl, gram[c][L:, LANES:], 0.0) for c in chunks]
    inv = [jnp.where(eye, 1.0, 0.0) + a_ab[c] for c in chunks]
    pw = each(lambda c: _dot(a_ab[c], diag2(a_ab[c])))
    w_s = each(lambda c: _dot(a_ak[c], v_d[c]))
    for _ in range(4):
        both = each(lambda c: _dot(pw[c], jnp.concatenate([diag2(pw[c]), diag2(inv[c])], axis=1)))
        pw = [both[c][:, :LANES] for c in chunks]
        inv = [inv[c] + both[c][:, LANES:] for c in chunks]
    last = each(lambda c: _dot(pw[c], diag2(inv[c])))
    inv = [inv[c] + last[c] for c in chunks]
    sol = each(lambda c: _dot(inv[c], jnp.concatenate([diag2(a_t[cs[c]]), diag2(w_s[c])], axis=1)))
    ahat_d = [diag2(sol[c][:, :LANES]) for c in chunks]
    u0_d = [diag2(sol[c][:, LANES:]) for c in chunks]

    def heads_down(x):
        return jnp.concatenate([x[:, :HEAD_DIM], x[:, HEAD_DIM:]], axis=0)

    bmn = each(lambda c: _dot_tn(heads_down(b_t[cs[c]] * p_l[c]),
                                 jnp.concatenate([ahat_d[c], u0_d[c]], axis=1)))
    kv = each(lambda c: _dot_tn(heads_down(k_t[cs[c]] * p_l[c]), v_d[c]))
    rb = each(lambda c: _dot(a_rb[c], ahat_d[c]))
    y0 = each(lambda c: _dot(jnp.concatenate([a_rb[c], a_rk[c]], axis=1),
                             jnp.concatenate([u0_d[c], v_d[c]], axis=0)))
    while carry["next"] < len(chunks):
        advance()

    state_ref[...] = carry["H"]
    y = jnp.concatenate(carry["ys"], axis=0)
    mean_blk = jnp.where(head_blk, 1.0 / HEAD_DIM, 0.0).astype(BF16)
    mean = _dot(y, mean_blk)
    yc = y - mean
    var = _dot(yc * yc, mean_blk)
    yn = yc * lax.rsqrt(var + GN_EPS) * lnw_ref[...] + lnb_ref[...]
    o_ref[...] = ((yn + post_st[prv, 0]) * post_st[prv, 1]).astype(o_ref.dtype)

    for c in chunks:
        m_st[cur, c] = (bmn[c][:, :LANES] + jnp.where(eye, p_l[c], 0.0)).astype(BF16)
        n_st[cur, c] = bmn[c][:, LANES:] + kv[c]
        rhat_st[cur, cs[c], :] = (r_t[cs[c]] + rb[c]).astype(BF16)
        y0_st[cur, cs[c], :] = y0[c]


def _rwkv(proj, mu, w0, w_up, a0, a_up, k_k, k_a, r_k, ln_w, ln_b, batch, seq):
    M = proj.shape[0]
    TT = RWKV_TT
    nt = seq // TT
    n_chunks = TT // CHUNK
    total = batch * PAIRS * nt

    def where(s):
        b, p, t = s // (PAIRS * nt), (s // nt) % PAIRS, s % nt
        return b * nt + t, p

    src = lambda s: where(jnp.minimum(s, total - 1))
    dst = lambda s: where(jnp.maximum(s - 1, 0))

    slab = lambda col: pl.BlockSpec((TT, LANES), lambda s: (src(s)[0], col // LANES + src(s)[1]))
    vec = pl.BlockSpec((1, LANES), lambda s: (0, src(s)[1]))
    vec0 = pl.BlockSpec((1, LANES), lambda s: (0, 0))
    vec_dst = pl.BlockSpec((1, LANES), lambda s: (0, dst(s)[1]))
    lora = pl.BlockSpec((LORA, LANES), lambda s: (0, src(s)[1]))
    mu_r, mu_k, mu_v, mu_wa = (mu[:, :1024], mu[:, 1024:2048], mu[:, 2048:3072], mu[:, 3072:])
    return pl.pallas_call(
        functools.partial(_rwkv_kernel, nt=nt, total=total),
        out_shape=jax.ShapeDtypeStruct((M, RWKV_WIDTH), BF16),
        grid=(total + 1,),
        in_specs=[slab(COL_R), slab(COL_RK), slab(COL_RV), slab(COL_RG),
                  pl.BlockSpec((TT, LANES), lambda s: (src(s)[0], COL_WA // LANES)),
                  vec, vec, vec, vec0,
                  vec, lora, vec, lora, vec, vec, vec, vec_dst, vec_dst],
        out_specs=pl.BlockSpec((TT, LANES), lambda s: dst(s)),
        scratch_shapes=[pltpu.VMEM((CHUNK, LANES), F32),
                        pltpu.VMEM((4, 8, LANES), F32),
                        pltpu.VMEM((2, n_chunks, CHUNK, LANES), BF16),
                        pltpu.VMEM((2, n_chunks, CHUNK, LANES), F32),
                        pltpu.VMEM((2, TT, LANES), BF16),
                        pltpu.VMEM((2, TT, LANES), F32),
                        pltpu.VMEM((2, 2, TT, LANES), F32)],
        compiler_params=pltpu.CompilerParams(dimension_semantics=("arbitrary",),
                                             vmem_limit_bytes=VMEM_LIMIT),
        name="rwkv7_mix",
    )(proj, proj, proj, proj, proj, mu_r, mu_k, mu_v, mu_wa,
      w0, w_up, a0, a_up, k_k, k_a, r_k, ln_w, ln_b)


RWKV_PIECE = 2 * CHUNK
PREP_FIRST = 4


def _rwkv3_kernel(r_ref, k_ref, v_ref, g_ref, wa_ref,
                  mu_r_ref, mu_k_ref, mu_v_ref, mu_wa_ref,
                  w0_ref, wup_ref, a0_ref, aup_ref, kk_ref, ka_ref, rk_ref, lnw_ref, lnb_ref,
                  o_ref, state_ref, last_ref,
                  pa_st, pr_st, pb_st, pk_st, pv_st, pbb_st, pkb_st, ppl_st,
                  m_st, n_st, rhat_st, y0_st, post_st, *, nt, total):
    s = pl.program_id(0)
    TT, L, P = RWKV_TT, CHUNK, RWKV_PIECE
    t_in = jnp.minimum(s, total - 1) % nt
    t_out = jnp.maximum(s - 2, 0) % nt
    cur = s % 2
    prv = 1 - cur
    post_w = s % 3
    post_r = (s + 1) % 3

    @pl.when(s == 0)
    def _():
        for ref in (state_ref, last_ref, pa_st, pr_st, pb_st, pk_st, pv_st, pbb_st, pkb_st, ppl_st,
                    m_st, n_st, rhat_st, y0_st, post_st):
            ref[...] = jnp.zeros_like(ref)

    chunks = range(TT // L)
    cs = [slice(c * L, (c + 1) * L) for c in chunks]

    lane_r = lax.broadcasted_iota(jnp.int32, (LANES, LANES), 0)
    lane_c = lax.broadcasted_iota(jnp.int32, (LANES, LANES), 1)
    head_blk = (lane_r // HEAD_DIM) == (lane_c // HEAD_DIM)
    ones_blk = jnp.where(head_blk, 1.0, 0.0).astype(BF16)

    lane_lo = lax.broadcasted_iota(jnp.int32, (L, LANES), 1) < HEAD_DIM

    def diag2(x):
        zero = jnp.zeros_like(x)
        return jnp.concatenate([jnp.where(lane_lo, x, zero), jnp.where(lane_lo, zero, x)], axis=0).astype(BF16)

    def fold(x):
        return x[:L] + x[L:]

    carry = {"H": jnp.where(t_out == 0, 0.0, state_ref[...]), "ys": [], "next": 0}

    def link():
        c = carry["next"]
        if c >= len(chunks):
            return
        H = diag2(carry["H"])
        carry["ys"].append(jnp.dot(rhat_st[prv, cs[c], :], H, preferred_element_type=F32) + y0_st[prv, cs[c], :])
        carry["H"] = jnp.dot(m_st[prv, c], H, preferred_element_type=F32) + n_st[prv, c]
        carry["next"] = c + 1

    n_pieces = TT // P
    first = t_in == 0
    row0 = lax.broadcasted_iota(jnp.int32, (P, LANES), 0) == 0
    pr_i = lax.broadcasted_iota(jnp.int32, (P, P), 0)
    pc_i = lax.broadcasted_iota(jnp.int32, (P, P), 1)
    tri = jnp.where(jnp.logical_and(pr_i // L == pc_i // L, pc_i <= pr_i), 1.0, 0.0).astype(BF16)
    todo = {"piece": 0}

    def prepare():
        j = todo["piece"]
        if j >= n_pieces:
            return
        todo["piece"] = j + 1
        rows = slice(j * P, (j + 1) * P)

        def shifted(ref, slot, mu_ref):
            x = ref[rows, :].astype(F32)
            if j == 0:
                before = jnp.where(first, 0.0, last_ref[slot, 7:8, :])
            else:
                before = ref[j * P - 16:j * P, :].astype(F32)[15:16, :]
            prev = jnp.where(row0, before, pltpu.roll(x, 1, axis=0))
            if j == n_pieces - 1:
                last_ref[slot] = x[P - 8:, :]
            return x + (prev - x) * mu_ref[...]

        r = shifted(r_ref, 0, mu_r_ref)
        k = shifted(k_ref, 1, mu_k_ref)
        v = shifted(v_ref, 2, mu_v_ref)
        wa = shifted(wa_ref, 3, mu_wa_ref)
        z = -(w0_ref[...] + _dot(jnp.tanh(wa[:, :LORA]), wup_ref[...]))
        softplus = jnp.maximum(z, 0.0) + jnp.log(1.0 + jnp.exp(-jnp.abs(z)))
        lw = -jnp.exp(-softplus - 0.5)
        a = _sigmoid(a0_ref[...] + _dot(wa[:, LORA:], aup_ref[...]))
        kk = k * kk_ref[...]
        kk = kk * lax.rsqrt(jnp.maximum(_dot(kk * kk, ones_blk), 1e-24))
        k = k * (1.0 + (a - 1.0) * ka_ref[...])
        bv = kk * a
        post_st[post_w, 0, rows, :] = _dot(r * k * rk_ref[...], ones_blk) * v
        post_st[post_w, 1, rows, :] = _silu(g_ref[rows, :].astype(F32))
        cum = _dot_exact_lhs(tri, lw)
        e_neg = jnp.exp(-cum)
        pr_st[cur, rows, :] = r * jnp.exp(cum)
        pa_st[cur, rows, :] = (-kk * jnp.exp(cum - lw)).astype(BF16)
        pb_st[cur, rows, :] = (bv * e_neg).astype(BF16)
        pk_st[cur, rows, :] = (k * e_neg).astype(BF16)
        pv_st[cur, rows, :] = v.astype(BF16)
        for h in range(P // L):
            sub = slice(h * L, (h + 1) * L)
            dst_rows = cs[j * (P // L) + h]
            cum_l = cum[(h + 1) * L - 1:(h + 1) * L, :]
            to_end = jnp.exp(cum_l - cum[sub])
            pbb_st[cur, dst_rows, :] = (bv[sub] * to_end).astype(BF16)
            pkb_st[cur, dst_rows, :] = (k[sub] * to_end).astype(BF16)
            ppl_st[cur, j * (P // L) + h] = jnp.broadcast_to(jnp.exp(cum_l), (8, LANES))

    row = lax.broadcasted_iota(jnp.int32, (L, LANES), 0)
    col = lax.broadcasted_iota(jnp.int32, (L, LANES), 1) % HEAD_DIM
    strict = col < row
    incl = col <= row
    eye = col == row

    for _ in range(PREP_FIRST):
        prepare()
    a_t = [pa_st[prv, sl, :] for sl in cs]
    r_t = [pr_st[prv, sl, :] for sl in cs]
    v_c = [pv_st[prv, sl, :] for sl in cs]
    bk_d = [jnp.concatenate([diag2(pb_st[prv, sl, :]), diag2(pk_st[prv, sl, :])], axis=0) for sl in cs]
    v_d = [diag2(v_c[c]) for c in chunks]
    gram_a = [_dot_nt(a_t[c], bk_d[c]) for c in chunks]
    gram_r = [_dot_nt(r_t[c], bk_d[c]) for c in chunks]
    prepare()
    link()
    a_ab = [jnp.where(strict, gram_a[c][:, :LANES], 0.0) for c in chunks]
    a_ak = [jnp.where(strict, gram_a[c][:, LANES:], 0.0) for c in chunks]
    a_rb = [jnp.where(incl, gram_r[c][:, :LANES], 0.0) for c in chunks]
    a_rk = [jnp.where(incl, gram_r[c][:, LANES:], 0.0) for c in chunks]
    inv = [jnp.where(eye, 1.0, 0.0) + a_ab[c] for c in chunks]
    pw = [_dot(a_ab[c], diag2(a_ab[c])) for c in chunks]
    w_s = [_dot(a_ak[c], v_d[c]) for c in chunks]
    prepare()
    link()
    for _ in range(4):
        both = [_dot(pw[c], jnp.concatenate([diag2(pw[c]), diag2(inv[c])], axis=1)) for c in chunks]
        pw = [both[c][:, :LANES] for c in chunks]
        inv = [inv[c] + both[c][:, LANES:] for c in chunks]
        prepare()
        link()
    inv = [inv[c] + _dot(pw[c], diag2(inv[c])) for c in chunks]
    sol = [_dot(inv[c], jnp.concatenate([diag2(a_t[c]), diag2(w_s[c])], axis=1)) for c in chunks]
    link()
    bmn = [_dot_tn(pbb_st[prv, cs[c], :], sol[c]) for c in chunks]
    kv = [_dot_tn(pkb_st[prv, cs[c], :], v_c[c]) for c in chunks]
    rb = [_dot(a_rb[c], jnp.concatenate([diag2(sol[c][:, :LANES]), diag2(sol[c][:, LANES:])], axis=1))
          for c in chunks]
    rkv = [_dot(a_rk[c], v_d[c]) for c in chunks]
    while todo["piece"] < n_pieces:
        prepare()
    while carry["next"] < len(chunks):
        link()

    state_ref[...] = carry["H"]
    y = jnp.concatenate(carry["ys"], axis=0)
    mean_blk = jnp.where(head_blk, 1.0 / HEAD_DIM, 0.0).astype(BF16)
    mean = _dot(y, mean_blk)
    yc = y - mean
    var = _dot(yc * yc, mean_blk)
    yn = yc * lax.rsqrt(var + GN_EPS) * lnw_ref[...] + lnb_ref[...]
    o_ref[...] = ((yn + post_st[post_r, 0]) * post_st[post_r, 1]).astype(o_ref.dtype)

    for c in chunks:
        m_st[cur, c] = (fold(jnp.where(head_blk, bmn[c][:, :LANES], 0.0))
                        + jnp.where(eye, ppl_st[prv, c, 0:1, :], 0.0)).astype(BF16)
        n_st[cur, c] = fold(jnp.where(head_blk, bmn[c][:, LANES:] + kv[c], 0.0))
        rhat_st[cur, cs[c], :] = (r_t[c] + rb[c][:, :LANES]).astype(BF16)
        y0_st[cur, cs[c], :] = rb[c][:, LANES:] + rkv[c]


def _rwkv3(proj, mu, w0, w_up, a0, a_up, k_k, k_a, r_k, ln_w, ln_b, batch, seq):
    M = proj.shape[0]
    TT = RWKV_TT
    nt = seq // TT
    n_chunks = TT // CHUNK
    total = batch * PAIRS * nt

    def where(s):
        b, p, t = s // (PAIRS * nt), (s // nt) % PAIRS, s % nt
        return b * nt + t, p

    src = lambda s: where(jnp.minimum(s, total - 1))
    dst = lambda s: where(jnp.maximum(s - 2, 0))

    slab = lambda col: pl.BlockSpec((TT, LANES), lambda s: (src(s)[0], col // LANES + src(s)[1]))
    vec = pl.BlockSpec((1, LANES), lambda s: (0, src(s)[1]))
    vec0 = pl.BlockSpec((1, LANES), lambda s: (0, 0))
    vec_dst = pl.BlockSpec((1, LANES), lambda s: (0, dst(s)[1]))
    lora = pl.BlockSpec((LORA, LANES), lambda s: (0, src(s)[1]))
    mu_r, mu_k, mu_v, mu_wa = (mu[:, :1024], mu[:, 1024:2048], mu[:, 2048:3072], mu[:, 3072:])
    operand = lambda dt: pltpu.VMEM((2, TT, LANES), dt)
    return pl.pallas_call(
        functools.partial(_rwkv3_kernel, nt=nt, total=total),
        out_shape=jax.ShapeDtypeStruct((M, RWKV_WIDTH), BF16),
        grid=(total + 2,),
        in_specs=[slab(COL_R), slab(COL_RK), slab(COL_RV), slab(COL_RG),
                  pl.BlockSpec((TT, LANES), lambda s: (src(s)[0], COL_WA // LANES)),
                  vec, vec, vec, vec0,
                  vec, lora, vec, lora, vec, vec, vec, vec_dst, vec_dst],
        out_specs=pl.BlockSpec((TT, LANES), lambda s: dst(s)),
        scratch_shapes=[pltpu.VMEM((CHUNK, LANES), F32),
                        pltpu.VMEM((4, 8, LANES), F32),
                        operand(BF16), operand(F32), operand(BF16), operand(BF16), operand(BF16),
                        operand(BF16), operand(BF16),
                        pltpu.VMEM((2, n_chunks, 8, LANES), F32),
                        pltpu.VMEM((2, n_chunks, CHUNK, LANES), BF16),
                        pltpu.VMEM((2, n_chunks, CHUNK, LANES), F32),
                        pltpu.VMEM((2, TT, LANES), BF16),
                        pltpu.VMEM((2, TT, LANES), F32),
                        pltpu.VMEM((3, 2, TT, LANES), F32)],
        compiler_params=pltpu.CompilerParams(dimension_semantics=("arbitrary",),
                                             vmem_limit_bytes=VMEM_LIMIT),
        name="rwkv7_mix",
    )(proj, proj, proj, proj, proj, mu_r, mu_k, mu_v, mu_wa,
      w0, w_up, a0, a_up, k_k, k_a, r_k, ln_w, ln_b)


def _out_proj_kernel(x_ref, ya_ref, yr_ref, wa_ref, wr_ref, g_ref, gate_ref, o_ref):
    mix = (jnp.dot(ya_ref[...], wa_ref[...], preferred_element_type=F32)
           + jnp.dot(yr_ref[...], wr_ref[...], preferred_element_type=F32))
    inv = lax.rsqrt(jnp.mean(mix * mix, axis=-1, keepdims=True) + RMS_EPS)
    o_ref[...] = x_ref[...] + gate_ref[0] * ((mix * inv) * g_ref[...])


def _out_proj(x2, ya, yr, w, g, gate, seq):
    M, D = x2.shape
    tm = 512
    per_b = seq // tm
    half = ya.shape[1]
    return pl.pallas_call(
        _out_proj_kernel,
        out_shape=jax.ShapeDtypeStruct((M, D), F32),
        grid=(M // tm,),
        in_specs=[pl.BlockSpec((tm, D), lambda i: (i, 0)),
                  pl.BlockSpec((tm, half), lambda i: (i, 0)),
                  pl.BlockSpec((tm, half), lambda i: (i, 0)),
                  pl.BlockSpec((half, D), lambda i: (0, 0)),
                  pl.BlockSpec((half, D), lambda i: (1, 0)),
                  pl.BlockSpec((1, D), lambda i: (0, 0)),
                  pl.BlockSpec((1, 1, D), lambda i: (i // per_b, 0, 0))],
        out_specs=pl.BlockSpec((tm, D), lambda i: (i, 0)),
        compiler_params=pltpu.CompilerParams(dimension_semantics=("parallel",),
                                             vmem_limit_bytes=VMEM_LIMIT),
        name="out_proj",
    )(x2, ya, yr, w, w, g, gate)


def _pack_w_in(w):
    return jnp.pad(w.astype(BF16), ((0, 0), (0, IN_COLS_PAD - IN_COLS)))


def kernel(x, c, w_ada, b_ada, pre_norm_g, post_norm_g, w_in, w_out, attn_sinks, rwkv_mu, rwkv_w0,
           rwkv_w_up, rwkv_a0, rwkv_a_up, rwkv_k_k, rwkv_k_a, rwkv_r_k, rwkv_ln_w, rwkv_ln_b):
    B, T, D = x.shape
    depth = w_ada.shape[0]
    x2 = x.reshape(B * T, D)
    for l in range(depth):
        mod = _adaln(c, w_ada[l], b_ada[l][None, :])
        shift, scale, gate = (mod[:, i * D:(i + 1) * D].reshape(B, 1, D) for i in range(3))
        proj = _in_proj(x2, pre_norm_g[l][None, :], scale, shift, _pack_w_in(w_in[l]), T)
        y_attn = _attention(proj, attn_sinks[l], B, T)
        y_rwkv = _rwkv(proj, rwkv_mu[l][None, :], rwkv_w0[l][None, :], rwkv_w_up[l].astype(BF16),
                       rwkv_a0[l][None, :], rwkv_a_up[l].astype(BF16), rwkv_k_k[l][None, :],
                       rwkv_k_a[l][None, :], rwkv_r_k[l].reshape(1, RWKV_WIDTH),
                       rwkv_ln_w[l][None, :], rwkv_ln_b[l][None, :], B, T)
        x2 = _out_proj(x2, y_attn, y_rwkv, w_out[l].astype(BF16), post_norm_g[l][None, :], gate, T)
    return x2.reshape(B, T, D)
```

```python
import functools
import math

import jax
import jax.numpy as jnp
from jax import lax
from jax.experimental import pallas as pl
from jax.experimental.pallas import tpu as pltpu

D_MODEL = 2048
HEAD_DIM = 64
ATTN_WIDTH = 1024
ATTN_Q_HEADS = 16
KV_WIDTH = 256
WINDOW = 128
RWKV_WIDTH = 1024
LORA = 64
IN_COLS = 6784
RMS_EPS = 1e-6
GN_EPS = 64e-5
NEG_BIG = -1e30

LANES = 128
PAIRS = RWKV_WIDTH // LANES
CHUNK = 64

COL_Q = 0
COL_K = 1024
COL_V = 1280
COL_GA = 1536
COL_R = 2560
COL_RK = 3584
COL_RV = 4608
COL_WA = 5632
COL_RG = 5760
GA_BLOCK = 512
IN_COLS_PAD = 6912

VMEM_LIMIT = 56 * 1024 * 1024

F32 = jnp.float32
BF16 = jnp.bfloat16


def _dot(a, b):
    return jnp.dot(a.astype(BF16), b.astype(BF16), preferred_element_type=F32)


def _dot_nt(a, b):
    return lax.dot_general(a.astype(BF16), b.astype(BF16), (((1,), (1,)), ((), ())),
                           preferred_element_type=F32)


def _dot_tn(a, b):
    return lax.dot_general(a.astype(BF16), b.astype(BF16), (((0,), (0,)), ((), ())),
                           preferred_element_type=F32)


def _split2(x):
    hi = x.astype(BF16)
    return hi, (x - hi.astype(F32)).astype(BF16)


def _dot_exact_lhs(a_bf16, b):
    hi, lo = _split2(b)
    return jnp.dot(a_bf16, hi, preferred_element_type=F32) + jnp.dot(a_bf16, lo, preferred_element_type=F32)


def _sigmoid(x):
    return 1.0 / (1.0 + jnp.exp(-x))


def _silu(x):
    return x * _sigmoid(x)


def _adaln_kernel(c_ref, w_ref, b_ref, o_ref):
    s = _silu(c_ref[...])
    s_hi, s_lo = _split2(s)
    w_hi, w_lo = _split2(w_ref[...])
    d = lambda p, q: jnp.dot(p, q, preferred_element_type=F32)
    o_ref[...] = d(s_hi, w_hi) + (d(s_hi, w_lo) + d(s_lo, w_hi)) + b_ref[...]


def _adaln(c, w, b):
    B, D = c.shape
    N = w.shape[1]
    tn = 512
    return pl.pallas_call(
        _adaln_kernel,
        out_shape=jax.ShapeDtypeStruct((B, N), F32),
        grid=(N // tn,),
        in_specs=[pl.BlockSpec((B, D), lambda j: (0, 0)),
                  pl.BlockSpec((D, tn), lambda j: (0, j)),
                  pl.BlockSpec((1, tn), lambda j: (0, j))],
        out_specs=pl.BlockSpec((B, tn), lambda j: (0, j)),
        compiler_params=pltpu.CompilerParams(dimension_semantics=("arbitrary",),
                                             vmem_limit_bytes=VMEM_LIMIT),
        name="adaln_mod",
    )(c, w, b)


NORM_ROWS = 16


def _in_proj_kernel(x_ref, g_ref, scale_ref, shift_ref, w_ref, o_ref, h_ref):
    @pl.when(pl.program_id(1) == 0)
    def _():
        gain = g_ref[...] * (1.0 + scale_ref[0])
        shift = shift_ref[0]

        def rows(c, carry):
            sl = pl.ds(pl.multiple_of(c * NORM_ROWS, NORM_ROWS), NORM_ROWS)
            x = x_ref[sl, :]
            inv = lax.rsqrt(jnp.mean(x * x, axis=-1, keepdims=True) + RMS_EPS)
            h_ref[sl, :] = ((x * inv) * gain + shift).astype(BF16)
            return carry

        lax.fori_loop(0, x_ref.shape[0] // NORM_ROWS, rows, 0, unroll=8)

    o_ref[...] = jnp.dot(h_ref[...], w_ref[...], preferred_element_type=F32).astype(o_ref.dtype)


def _in_proj(x2, g, scale, shift, w_packed, seq):
    M, D = x2.shape
    NP = w_packed.shape[1]
    tm, tn = 1024, 768
    per_b = seq // tm
    return pl.pallas_call(
        _in_proj_kernel,
        out_shape=jax.ShapeDtypeStruct((M, NP), BF16),
        grid=(M // tm, NP // tn),
        in_specs=[pl.BlockSpec((tm, D), lambda i, j: (i, 0)),
                  pl.BlockSpec((1, D), lambda i, j: (0, 0)),
                  pl.BlockSpec((1, 1, D), lambda i, j: (i // per_b, 0, 0)),
                  pl.BlockSpec((1, 1, D), lambda i, j: (i // per_b, 0, 0)),
                  pl.BlockSpec((D, tn), lambda i, j: (0, j))],
        out_specs=pl.BlockSpec((tm, tn), lambda i, j: (i, j)),
        scratch_shapes=[pltpu.VMEM((tm, D), BF16)],
        compiler_params=pltpu.CompilerParams(dimension_semantics=("parallel", "arbitrary"),
                                             vmem_limit_bytes=VMEM_LIMIT),
        name="in_proj",
    )(x2, g, scale, shift, w_packed)


ATTN_GROUP = 4
ATTN_ROWS = 2 * WINDOW


def _dup_halves(slab, lane_lo):
    swapped = pltpu.roll(slab, HEAD_DIM, axis=1)
    return jnp.where(lane_lo, slab, swapped), jnp.where(lane_lo, swapped, slab)


def _attn_kernel(sink_ref, q_ref, ga_lo_ref, ga_hi_ref, k_ref, v_ref, o_ref, kprev_ref, vprev_ref):
    i = pl.program_id(1)
    blk = WINDOW
    lane_lo = lax.broadcasted_iota(jnp.int32, (blk, LANES), 1) < HEAD_DIM

    @pl.when(i == 0)
    def _():
        kprev_ref[...] = jnp.zeros_like(kprev_ref)
        vprev_ref[...] = jnp.zeros_like(vprev_ref)

    ones = jnp.ones((blk, LANES), BF16)
    qi = lax.broadcasted_iota(jnp.int32, (2 * blk, blk), 0) % blk
    si = lax.broadcasted_iota(jnp.int32, (2 * blk, blk), 1)
    cur_ok = si <= qi
    scale = HEAD_DIM ** -0.5
    group = ATTN_Q_HEADS // (KV_WIDTH // HEAD_DIM)
    n_kv = KV_WIDTH // HEAD_DIM
    row_lo = lax.broadcasted_iota(jnp.int32, (2 * blk, 1), 0) < blk

    kprev = [kprev_ref[g] for g in range(n_kv)]
    vprev = [vprev_ref[g] for g in range(n_kv)]
    for w in range(ATTN_ROWS // blk):
        rows = slice(w * blk, (w + 1) * blk)
        prev_ok = jnp.logical_and(si > qi, i > 0) if w == 0 else si > qi

        kcur, vcur = [], []
        for s in range(KV_WIDTH // LANES):
            ks = k_ref[rows, s * LANES:(s + 1) * LANES].astype(F32)
            vs = v_ref[rows, s * LANES:(s + 1) * LANES].astype(F32)
            kcur.extend(t.astype(BF16) for t in _dup_halves(ks, lane_lo))
            vcur.extend(t.astype(BF16) for t in _dup_halves(vs, lane_lo))

        def stacked_q(p):
            q = q_ref[rows, p * LANES:(p + 1) * LANES].astype(F32) * scale
            zero = jnp.zeros_like(q)
            return jnp.concatenate([jnp.where(lane_lo, q, zero), jnp.where(lane_lo, zero, q)], axis=0).astype(BF16)

        for p0 in range(0, PAIRS, ATTN_GROUP):
            ps = range(p0, p0 + ATTN_GROUP)
            kvh = {p: (2 * p) // group for p in ps}
            qs = {p: stacked_q(p) for p in ps}
            s = {p: jnp.where(cur_ok, _dot_nt(qs[p], kcur[kvh[p]]),
                              jnp.where(prev_ok, _dot_nt(qs[p], kprev[kvh[p]]), NEG_BIG)) for p in ps}
            sink = {p: jnp.where(row_lo, sink_ref[2 * p], sink_ref[2 * p + 1]) for p in ps}
            m = {p: jnp.maximum(jnp.max(s[p], axis=-1, keepdims=True), sink[p]) for p in ps}
            e = {p: jnp.exp(s[p] - m[p]) for p in ps}
            e2 = {p: jnp.concatenate([jnp.where(cur_ok, e[p], 0.0), jnp.where(cur_ok, 0.0, e[p])],
                                     axis=1).astype(BF16) for p in ps}
            acc = {p: jnp.dot(e2[p], jnp.concatenate([jnp.concatenate([vcur[kvh[p]], ones], axis=1),
                                                      jnp.concatenate([vprev[kvh[p]], ones], axis=1)], axis=0),
                              preferred_element_type=F32) for p in ps}
            for p in ps:
                num = acc[p][:, :LANES]
                den = acc[p][:, LANES:] + jnp.exp(sink[p] - m[p])
                o = jnp.where(lane_lo, num[:blk], num[blk:]) / jnp.where(lane_lo, den[:blk], den[blk:])
                ga_ref, gp = (ga_lo_ref, p) if p < PAIRS // 2 else (ga_hi_ref, p - PAIRS // 2)
                ga = ga_ref[rows, gp * LANES:(gp + 1) * LANES].astype(F32)
                o_ref[rows, p * LANES:(p + 1) * LANES] = (o * _silu(ga)).astype(o_ref.dtype)
        kprev, vprev = kcur, vcur

    for g in range(n_kv):
        kprev_ref[g] = kprev[g]
        vprev_ref[g] = vprev[g]


def _attention(proj, sinks, batch, seq):
    M = proj.shape[0]
    nb = seq // ATTN_ROWS
    row = lambda b, i: b * nb + i
    n_kv = KV_WIDTH // HEAD_DIM
    return pl.pallas_call(
        _attn_kernel,
        out_shape=jax.ShapeDtypeStruct((M, ATTN_WIDTH), BF16),
        grid=(batch, nb),
        in_specs=[pl.BlockSpec(memory_space=pltpu.SMEM),
                  pl.BlockSpec((ATTN_ROWS, ATTN_WIDTH), lambda b, i: (row(b, i), COL_Q // ATTN_WIDTH)),
                  pl.BlockSpec((ATTN_ROWS, GA_BLOCK), lambda b, i: (row(b, i), COL_GA // GA_BLOCK)),
                  pl.BlockSpec((ATTN_ROWS, GA_BLOCK), lambda b, i: (row(b, i), COL_GA // GA_BLOCK + 1)),
                  pl.BlockSpec((ATTN_ROWS, KV_WIDTH), lambda b, i: (row(b, i), COL_K // KV_WIDTH)),
                  pl.BlockSpec((ATTN_ROWS, KV_WIDTH), lambda b, i: (row(b, i), COL_V // KV_WIDTH))],
        out_specs=pl.BlockSpec((ATTN_ROWS, ATTN_WIDTH), lambda b, i: (row(b, i), 0)),
        scratch_shapes=[pltpu.VMEM((n_kv, WINDOW, LANES), BF16),
                        pltpu.VMEM((n_kv, WINDOW, LANES), BF16)],
        compiler_params=pltpu.CompilerParams(dimension_semantics=("parallel", "arbitrary"),
                                             vmem_limit_bytes=VMEM_LIMIT),
        name="swa_attn",
    )(sinks, proj, proj, proj, proj, proj)


DECAY_SCALE = math.exp(-0.5)
RWKV_TT = 1024
LINK_PERIOD = 13


def _rwkv_kernel(r_ref, k_ref, v_ref, g_ref, wa_ref,
                 mu_r_ref, mu_k_ref, mu_v_ref, mu_wa_ref,
                 w0_ref, wup_ref, a0_ref, aup_ref, kk_ref, ka_ref, rk_ref, lnw_ref, lnb_ref,
                 o_ref, state_ref, last_ref, m_st, n_st, rhat_st, y0_st, post_st, *, nt, total):
    s = pl.program_id(0)
    TT = RWKV_TT
    L = CHUNK
    t_in = jnp.minimum(s, total - 1) % nt
    t_out = jnp.maximum(s - 1, 0) % nt
    cur = s % 2
    prv = 1 - cur

    @pl.when(s == 0)
    def _():
        for ref in (state_ref, last_ref, m_st, n_st, rhat_st, y0_st, post_st):
            ref[...] = jnp.zeros_like(ref)

    chunks = range(TT // L)
    cs = [slice(c * L, (c + 1) * L) for c in chunks]

    lane_lo = lax.broadcasted_iota(jnp.int32, (L, LANES), 1) < HEAD_DIM

    def diag2(x):
        zero = jnp.zeros_like(x)
        return jnp.concatenate([jnp.where(lane_lo, x, zero), jnp.where(lane_lo, zero, x)], axis=0).astype(BF16)

    carry = {"H": jnp.where(t_out == 0, 0.0, state_ref[...]), "ys": [], "next": 0}

    def advance():
        c = carry["next"]
        if c >= len(chunks):
            return
        H = diag2(carry["H"])
        carry["ys"].append(jnp.dot(rhat_st[prv, cs[c], :], H, preferred_element_type=F32) + y0_st[prv, cs[c], :])
        carry["H"] = jnp.dot(m_st[prv, c], H, preferred_element_type=F32) + n_st[prv, c]
        carry["next"] = c + 1

    row0 = lax.broadcasted_iota(jnp.int32, (8, LANES), 0) == 0
    first = t_in == 0

    def shifted(ref, slot, mu_ref):
        x = ref[...].astype(F32)
        carried = jnp.where(first, 0.0, last_ref[slot, 7:8, :])
        rolled = pltpu.roll(x, 1, axis=0)
        prev = jnp.concatenate([jnp.where(row0, carried, rolled[:8]), rolled[8:]], axis=0)
        last_ref[slot] = x[TT - 8:, :]
        return x + (prev - x) * mu_ref[...]

    r = shifted(r_ref, 0, mu_r_ref)
    k = shifted(k_ref, 1, mu_k_ref)
    v = shifted(v_ref, 2, mu_v_ref)
    wa = shifted(wa_ref, 3, mu_wa_ref)
    wd = wa[:, :LORA]
    ad = wa[:, LORA:]

    u = w0_ref[...] + _dot(jnp.tanh(wd), wup_ref[...])
    lw = -DECAY_SCALE / (1.0 + jnp.exp(-u))
    a = _sigmoid(a0_ref[...] + _dot(ad, aup_ref[...]))
    advance()

    lane_r = lax.broadcasted_iota(jnp.int32, (LANES, LANES), 0)
    lane_c = lax.broadcasted_iota(jnp.int32, (LANES, LANES), 1)
    head_blk = (lane_r // HEAD_DIM) == (lane_c // HEAD_DIM)
    ones_blk = jnp.where(head_blk, 1.0, 0.0).astype(BF16)

    kk = k * kk_ref[...]
    kk = kk * lax.rsqrt(jnp.maximum(_dot(kk * kk, ones_blk), 1e-24))
    k = k * ((1.0 - ka_ref[...]) + a * ka_ref[...])
    av = -kk
    bv = kk * a
    post_st[cur, 0] = _dot(r * k * rk_ref[...], ones_blk) * v
    post_st[cur, 1] = _silu(g_ref[...].astype(F32))

    TB = 4 * L
    tr = lax.broadcasted_iota(jnp.int32, (TB, TB), 0)
    tc = lax.broadcasted_iota(jnp.int32, (TB, TB), 1)
    tri = jnp.where(jnp.logical_and(tr // L == tc // L, tc <= tr), 1.0, 0.0).astype(BF16)
    cum = jnp.concatenate([_dot_exact_lhs(tri, lw[i * TB:(i + 1) * TB]) for i in range(TT // TB)], axis=0)
    advance()

    e_pos = jnp.exp(cum)
    e_neg = 1.0 / e_pos
    r_t = r * e_pos
    k_t = k * e_neg
    b_t = bv * e_neg
    a_t = av * jnp.exp(cum - lw)

    row = lax.broadcasted_iota(jnp.int32, (L, LANES), 0)
    col = lax.broadcasted_iota(jnp.int32, (L, LANES), 1) % HEAD_DIM
    strict = col < row
    incl = col <= row
    eye = col == row

    ticks = {"n": 0}

    def each(fn):
        out = []
        for c in chunks:
            out.append(fn(c))
            ticks["n"] += 1
            if ticks["n"] % LINK_PERIOD == 0:
                advance()
        return out

    cum_l = [cum[sl][L - 1:L, :] for sl in cs]
    p_l = [jnp.exp(cum_l[c]) for c in chunks]
    bk_d = [jnp.concatenate([diag2(b_t[sl]), diag2(k_t[sl])], axis=0) for sl in cs]
    v_d = [diag2(v[sl]) for sl in cs]
    gram = each(lambda c: _dot_nt(jnp.concatenate([a_t[cs[c]], r_t[cs[c]]], axis=0), bk_d[c]))
    a_ab = [jnp.where(strict, gram[c][:L, :LANES], 0.0) for c in chunks]
    a_ak = [jnp.where(strict, gram[c][:L, LANES:], 0.0).astype(BF16) for c in chunks]
    a_rb = [jnp.where(incl, gram[c][L:, :LANES], 0.0).astype(BF16) for c in chunks]
    a_rk = [jnp.where(incl, gram[c][L:, LANES:], 0.0).astype(BF16) for c in chunks]
    inv = [jnp.where(eye, 1.0, 0.0) + a_ab[c] for c in chunks]
    pw = each(lambda c: _dot(a_ab[c], diag2(a_ab[c])).astype(BF16))
    w_s = each(lambda c: _dot(a_ak[c], v_d[c]).astype(BF16))
    for _ in range(4):
        both = each(lambda c: _dot(pw[c], jnp.concatenate([diag2(pw[c]), diag2(inv[c])], axis=1)))
        pw = [both[c][:, :LANES].astype(BF16) for c in chunks]
        inv = [inv[c] + both[c][:, LANES:] for c in chunks]
    last = each(lambda c: _dot(pw[c], diag2(inv[c])))
    inv = [inv[c] + last[c] for c in chunks]
    sol = each(lambda c: _dot(inv[c], jnp.concatenate([diag2(a_t[cs[c]]), diag2(w_s[c])], axis=1)))
    ahat_d = [diag2(sol[c][:, :LANES]) for c in chunks]
    u0_d = [diag2(sol[c][:, LANES:]) for c in chunks]

    def heads_down(x):
        return jnp.concatenate([x[:, :HEAD_DIM], x[:, HEAD_DIM:]], axis=0)

    bmn = each(lambda c: _dot_tn(heads_down(b_t[cs[c]] * p_l[c]),
                                 jnp.concatenate([ahat_d[c], u0_d[c]], axis=1)))
    kv = each(lambda c: _dot_tn(heads_down(k_t[cs[c]] * p_l[c]), v_d[c]))
    rb = each(lambda c: _dot(a_rb[c], ahat_d[c]))
    y0 = each(lambda c: _dot(jnp.concatenate([a_rb[c], a_rk[c]], axis=1),
                             jnp.concatenate([u0_d[c], v_d[c]], axis=0)))
    while carry["next"] < len(chunks):
        advance()

    state_ref[...] = carry["H"]
    y = jnp.concatenate(carry["ys"], axis=0)
    mean_blk = jnp.where(head_blk, 1.0 / HEAD_DIM, 0.0).astype(BF16)
    mean = _dot(y, mean_blk)
    yc = y - mean
    var = _dot(yc * yc, mean_blk)
    yn = yc * lax.rsqrt(var + GN_EPS) * lnw_ref[...] + lnb_ref[...]
    o_ref[...] = ((yn + post_st[prv, 0]) * post_st[prv, 1]).astype(o_ref.dtype)

    for c in chunks:
        m_st[cur, c] = (bmn[c][:, :LANES] + jnp.where(eye, p_l[c], 0.0)).astype(BF16)
        n_st[cur, c] = bmn[c][:, LANES:] + kv[c]
        rhat_st[cur, cs[c], :] = (r_t[cs[c]] + rb[c]).astype(BF16)
        y0_st[cur, cs[c], :] = y0[c]


def _rwkv(proj, mu, w0, w_up, a0, a_up, k_k, k_a, r_k, ln_w, ln_b, batch, seq):
    M = proj.shape[0]
    TT = RWKV_TT
    nt = seq // TT
    n_chunks = TT // CHUNK
    total = batch * PAIRS * nt

    def where(s):
        b, p, t = s // (PAIRS * nt), (s // nt) % PAIRS, s % nt
        return b * nt + t, p

    src = lambda s: where(jnp.minimum(s, total - 1))
    dst = lambda s: where(jnp.maximum(s - 1, 0))

    slab = lambda col: pl.BlockSpec((TT, LANES), lambda s: (src(s)[0], col // LANES + src(s)[1]))
    vec = pl.BlockSpec((1, LANES), lambda s: (0, src(s)[1]))
    vec0 = pl.BlockSpec((1, LANES), lambda s: (0, 0))
    vec_dst = pl.BlockSpec((1, LANES), lambda s: (0, dst(s)[1]))
    lora = pl.BlockSpec((LORA, LANES), lambda s: (0, src(s)[1]))
    mu_r, mu_k, mu_v, mu_wa = (mu[:, :1024], mu[:, 1024:2048], mu[:, 2048:3072], mu[:, 3072:])
    return pl.pallas_call(
        functools.partial(_rwkv_kernel, nt=nt, total=total),
        out_shape=jax.ShapeDtypeStruct((M, RWKV_WIDTH), BF16),
        grid=(total + 1,),
        in_specs=[slab(COL_R), slab(COL_RK), slab(COL_RV), slab(COL_RG),
                  pl.BlockSpec((TT, LANES), lambda s: (src(s)[0], COL_WA // LANES)),
                  vec, vec, vec, vec0,
                  vec, lora, vec, lora, vec, vec, vec, vec_dst, vec_dst],
        out_specs=pl.BlockSpec((TT, LANES), lambda s: dst(s)),
        scratch_shapes=[pltpu.VMEM((CHUNK, LANES), F32),
                        pltpu.VMEM((4, 8, LANES), F32),
                        pltpu.VMEM((2, n_chunks, CHUNK, LANES), BF16),
                        pltpu.VMEM((2, n_chunks, CHUNK, LANES), F32),
                        pltpu.VMEM((2, TT, LANES), BF16),
                        pltpu.VMEM((2, TT, LANES), F32),
                        pltpu.VMEM((2, 2, TT, LANES), F32)],
        compiler_params=pltpu.CompilerParams(dimension_semantics=("arbitrary",),
                                             vmem_limit_bytes=VMEM_LIMIT),
        name="rwkv7_mix",
    )(proj, proj, proj, proj, proj, mu_r, mu_k, mu_v, mu_wa,
      w0, w_up, a0, a_up, k_k, k_a, r_k, ln_w, ln_b)


def _out_proj_kernel(x_ref, ya_ref, yr_ref, wa_ref, wr_ref, g_ref, gate_ref, o_ref):
    mix = (jnp.dot(ya_ref[...], wa_ref[...], preferred_element_type=F32)
           + jnp.dot(yr_ref[...], wr_ref[...], preferred_element_type=F32))
    inv = lax.rsqrt(jnp.mean(mix * mix, axis=-1, keepdims=True) + RMS_EPS)
    o_ref[...] = x_ref[...] + gate_ref[0] * ((mix * inv) * g_ref[...])


def _out_proj(x2, ya, yr, w, g, gate, seq):
    M, D = x2.shape
    tm = 512
    per_b = seq // tm
    half = ya.shape[1]
    return pl.pallas_call(
        _out_proj_kernel,
        out_shape=jax.ShapeDtypeStruct((M, D), F32),
        grid=(M // tm,),
        in_specs=[pl.BlockSpec((tm, D), lambda i: (i, 0)),
                  pl.BlockSpec((tm, half), lambda i: (i, 0)),
                  pl.BlockSpec((tm, half), lambda i: (i, 0)),
                  pl.BlockSpec((half, D), lambda i: (0, 0)),
                  pl.BlockSpec((half, D), lambda i: (1, 0)),
                  pl.BlockSpec((1, D), lambda i: (0, 0)),
                  pl.BlockSpec((1, 1, D), lambda i: (i // per_b, 0, 0))],
        out_specs=pl.BlockSpec((tm, D), lambda i: (i, 0)),
        compiler_params=pltpu.CompilerParams(dimension_semantics=("parallel",),
                                             vmem_limit_bytes=VMEM_LIMIT),
        name="out_proj",
    )(x2, ya, yr, w, w, g, gate)


def _pack_w_in(w):
    return jnp.pad(w, ((0, 0), (0, IN_COLS_PAD - IN_COLS))).astype(BF16)


def kernel(x, c, w_ada, b_ada, pre_norm_g, post_norm_g, w_in, w_out, attn_sinks, rwkv_mu, rwkv_w0,
           rwkv_w_up, rwkv_a0, rwkv_a_up, rwkv_k_k, rwkv_k_a, rwkv_r_k, rwkv_ln_w, rwkv_ln_b):
    B, T, D = x.shape
    depth = w_ada.shape[0]
    x2 = x.reshape(B * T, D)
    for l in range(depth):
        mod = _adaln(c, w_ada[l], b_ada[l][None, :])
        shift, scale, gate = (mod[:, i * D:(i + 1) * D].reshape(B, 1, D) for i in range(3))
        proj = _in_proj(x2, pre_norm_g[l][None, :], scale, shift, _pack_w_in(w_in[l]), T)
        y_attn = _attention(proj, attn_sinks[l], B, T)
        y_rwkv = _rwkv(proj, rwkv_mu[l][None, :], rwkv_w0[l][None, :], rwkv_w_up[l].astype(BF16),
                       rwkv_a0[l][None, :], rwkv_a_up[l].astype(BF16), rwkv_k_k[l][None, :],
                       rwkv_k_a[l][None, :], rwkv_r_k[l].reshape(1, RWKV_WIDTH),
                       rwkv_ln_w[l][None, :], rwkv_ln_b[l][None, :], B, T)
        x2 = _out_proj(x2, y_attn, y_rwkv, w_out[l].astype(BF16), post_norm_g[l][None, :], gate, T)
    return x2.reshape(B, T, D)
```

```python
import functools
import math

import jax
import jax.numpy as jnp
from jax import lax
from jax.experimental import pallas as pl
from jax.experimental.pallas import tpu as pltpu

D_MODEL = 2048
HEAD_DIM = 64
ATTN_WIDTH = 1024
ATTN_Q_HEADS = 16
KV_WIDTH = 256
WINDOW = 128
RWKV_WIDTH = 1024
LORA = 64
IN_COLS = 6784
RMS_EPS = 1e-6
GN_EPS = 64e-5
NEG_BIG = -1e30

LANES = 128
PAIRS = RWKV_WIDTH // LANES
CHUNK = 64

COL_Q = 0
COL_K = 1024
COL_V = 1280
COL_GA = 1536
COL_R = 2560
COL_RK = 3584
COL_RV = 4608
COL_WA = 5632
COL_RG = 5760
GA_BLOCK = 512
IN_COLS_PAD = 6912

VMEM_LIMIT = 56 * 1024 * 1024

F32 = jnp.float32
BF16 = jnp.bfloat16


def _dot(a, b):
    return jnp.dot(a.astype(BF16), b.astype(BF16), preferred_element_type=F32)


def _dot_nt(a, b):
    return lax.dot_general(a.astype(BF16), b.astype(BF16), (((1,), (1,)), ((), ())),
                           preferred_element_type=F32)


def _dot_tn(a, b):
    return lax.dot_general(a.astype(BF16), b.astype(BF16), (((0,), (0,)), ((), ())),
                           preferred_element_type=F32)


def _split2(x):
    hi = x.astype(BF16)
    return hi, (x - hi.astype(F32)).astype(BF16)


def _dot_exact_lhs(a_bf16, b):
    hi, lo = _split2(b)
    return jnp.dot(a_bf16, hi, preferred_element_type=F32) + jnp.dot(a_bf16, lo, preferred_element_type=F32)


def _sigmoid(x):
    return 1.0 / (1.0 + jnp.exp(-x))


def _silu(x):
    return x * _sigmoid(x)


def _adaln_kernel(c_ref, w_ref, b_ref, o_ref):
    s = _silu(c_ref[...])
    s_hi, s_lo = _split2(s)
    w_hi, w_lo = _split2(w_ref[...])
    d = lambda p, q: jnp.dot(p, q, preferred_element_type=F32)
    o_ref[...] = d(s_hi, w_hi) + (d(s_hi, w_lo) + d(s_lo, w_hi)) + b_ref[...]


def _adaln(c, w, b):
    B, D = c.shape
    N = w.shape[1]
    tn = 512
    return pl.pallas_call(
        _adaln_kernel,
        out_shape=jax.ShapeDtypeStruct((B, N), F32),
        grid=(N // tn,),
        in_specs=[pl.BlockSpec((B, D), lambda j: (0, 0)),
                  pl.BlockSpec((D, tn), lambda j: (0, j)),
                  pl.BlockSpec((1, tn), lambda j: (0, j))],
        out_specs=pl.BlockSpec((B, tn), lambda j: (0, j)),
        compiler_params=pltpu.CompilerParams(dimension_semantics=("arbitrary",),
                                             vmem_limit_bytes=VMEM_LIMIT),
        name="adaln_mod",
    )(c, w, b)


NORM_ROWS = 16


def _in_proj_kernel(x_ref, g_ref, scale_ref, shift_ref, w_ref, o_ref, h_ref):
    @pl.when(pl.program_id(1) == 0)
    def _():
        gain = g_ref[...] * (1.0 + scale_ref[0])
        shift = shift_ref[0]

        def rows(c, carry):
            sl = pl.ds(pl.multiple_of(c * NORM_ROWS, NORM_ROWS), NORM_ROWS)
            x = x_ref[sl, :]
            inv = lax.rsqrt(jnp.mean(x * x, axis=-1, keepdims=True) + RMS_EPS)
            h_ref[sl, :] = ((x * inv) * gain + shift).astype(BF16)
            return carry

        lax.fori_loop(0, x_ref.shape[0] // NORM_ROWS, rows, 0, unroll=8)

    o_ref[...] = jnp.dot(h_ref[...], w_ref[...], preferred_element_type=F32).astype(o_ref.dtype)


def _in_proj(x2, g, scale, shift, w_packed, seq):
    M, D = x2.shape
    NP = w_packed.shape[1]
    tm, tn = 512, 2304
    per_b = seq // tm
    return pl.pallas_call(
        _in_proj_kernel,
        out_shape=jax.ShapeDtypeStruct((M, NP), BF16),
        grid=(M // tm, NP // tn),
        in_specs=[pl.BlockSpec((tm, D), lambda i, j: (i, 0)),
                  pl.BlockSpec((1, D), lambda i, j: (0, 0)),
                  pl.BlockSpec((1, 1, D), lambda i, j: (i // per_b, 0, 0)),
                  pl.BlockSpec((1, 1, D), lambda i, j: (i // per_b, 0, 0)),
                  pl.BlockSpec((D, tn), lambda i, j: (0, j))],
        out_specs=pl.BlockSpec((tm, tn), lambda i, j: (i, j)),
        scratch_shapes=[pltpu.VMEM((tm, D), BF16)],
        compiler_params=pltpu.CompilerParams(dimension_semantics=("parallel", "arbitrary"),
                                             vmem_limit_bytes=VMEM_LIMIT),
        name="in_proj",
    )(x2, g, scale, shift, w_packed)


ATTN_GROUP = 8
ATTN_ROWS = 4 * WINDOW


def _dup_halves(slab, lane_lo):
    swapped = pltpu.roll(slab, HEAD_DIM, axis=1)
    return jnp.where(lane_lo, slab, swapped), jnp.where(lane_lo, swapped, slab)


def _attn_kernel(sink_ref, q_ref, ga_lo_ref, ga_hi_ref, k_ref, v_ref, o_ref, kprev_ref, vprev_ref):
    i = pl.program_id(1)
    blk = WINDOW
    lane_lo = lax.broadcasted_iota(jnp.int32, (blk, LANES), 1) < HEAD_DIM

    @pl.when(i == 0)
    def _():
        kprev_ref[...] = jnp.zeros_like(kprev_ref)
        vprev_ref[...] = jnp.zeros_like(vprev_ref)

    ones = jnp.ones((blk, LANES), BF16)
    qi = lax.broadcasted_iota(jnp.int32, (2 * blk, blk), 0) % blk
    si = lax.broadcasted_iota(jnp.int32, (2 * blk, blk), 1)
    cur_ok = si <= qi
    scale = HEAD_DIM ** -0.5
    group = ATTN_Q_HEADS // (KV_WIDTH // HEAD_DIM)
    n_kv = KV_WIDTH // HEAD_DIM
    row_lo = lax.broadcasted_iota(jnp.int32, (2 * blk, 1), 0) < blk

    kprev = [kprev_ref[g] for g in range(n_kv)]
    vprev = [vprev_ref[g] for g in range(n_kv)]
    for w in range(ATTN_ROWS // blk):
        rows = slice(w * blk, (w + 1) * blk)
        prev_ok = jnp.logical_and(si > qi, i > 0) if w == 0 else si > qi

        kcur, vcur = [], []
        for s in range(KV_WIDTH // LANES):
            ks = k_ref[rows, s * LANES:(s + 1) * LANES].astype(F32)
            vs = v_ref[rows, s * LANES:(s + 1) * LANES].astype(F32)
            kcur.extend(t.astype(BF16) for t in _dup_halves(ks, lane_lo))
            vcur.extend(t.astype(BF16) for t in _dup_halves(vs, lane_lo))

        def stacked_q(p):
            q = q_ref[rows, p * LANES:(p + 1) * LANES].astype(F32) * scale
            zero = jnp.zeros_like(q)
            return jnp.concatenate([jnp.where(lane_lo, q, zero), jnp.where(lane_lo, zero, q)], axis=0).astype(BF16)

        for p0 in range(0, PAIRS, ATTN_GROUP):
            ps = range(p0, p0 + ATTN_GROUP)
            kvh = {p: (2 * p) // group for p in ps}
            qs = {p: stacked_q(p) for p in ps}
            s = {p: jnp.where(cur_ok, _dot_nt(qs[p], kcur[kvh[p]]),
                              jnp.where(prev_ok, _dot_nt(qs[p], kprev[kvh[p]]), NEG_BIG)) for p in ps}
            sink = {p: jnp.where(row_lo, sink_ref[2 * p], sink_ref[2 * p + 1]) for p in ps}
            m = {p: jnp.max(s[p], axis=-1, keepdims=True) for p in ps}
            e = {p: jnp.exp(s[p] - m[p]) for p in ps}
            e2 = {p: jnp.concatenate([jnp.where(cur_ok, e[p], 0.0), jnp.where(cur_ok, 0.0, e[p])],
                                     axis=1).astype(BF16) for p in ps}
            acc = {p: jnp.dot(e2[p], jnp.concatenate([jnp.concatenate([vcur[kvh[p]], ones], axis=1),
                                                      jnp.concatenate([vprev[kvh[p]], ones], axis=1)], axis=0),
                              preferred_element_type=F32) for p in ps}
            for p in ps:
                num = acc[p][:, :LANES]
                den = acc[p][:, LANES:] + jnp.exp(sink[p] - m[p])
                o = jnp.where(lane_lo, num[:blk], num[blk:]) / jnp.where(lane_lo, den[:blk], den[blk:])
                ga_ref, gp = (ga_lo_ref, p) if p < PAIRS // 2 else (ga_hi_ref, p - PAIRS // 2)
                ga = ga_ref[rows, gp * LANES:(gp + 1) * LANES].astype(F32)
                o_ref[rows, p * LANES:(p + 1) * LANES] = (o * _silu(ga)).astype(o_ref.dtype)
        kprev, vprev = kcur, vcur

    for g in range(n_kv):
        kprev_ref[g] = kprev[g]
        vprev_ref[g] = vprev[g]


def _attention(proj, sinks, batch, seq):
    M = proj.shape[0]
    nb = seq // ATTN_ROWS
    row = lambda b, i: b * nb + i
    n_kv = KV_WIDTH // HEAD_DIM
    return pl.pallas_call(
        _attn_kernel,
        out_shape=jax.ShapeDtypeStruct((M, ATTN_WIDTH), BF16),
        grid=(batch, nb),
        in_specs=[pl.BlockSpec(memory_space=pltpu.SMEM),
                  pl.BlockSpec((ATTN_ROWS, ATTN_WIDTH), lambda b, i: (row(b, i), COL_Q // ATTN_WIDTH)),
                  pl.BlockSpec((ATTN_ROWS, GA_BLOCK), lambda b, i: (row(b, i), COL_GA // GA_BLOCK)),
                  pl.BlockSpec((ATTN_ROWS, GA_BLOCK), lambda b, i: (row(b, i), COL_GA // GA_BLOCK + 1)),
                  pl.BlockSpec((ATTN_ROWS, KV_WIDTH), lambda b, i: (row(b, i), COL_K // KV_WIDTH)),
                  pl.BlockSpec((ATTN_ROWS, KV_WIDTH), lambda b, i: (row(b, i), COL_V // KV_WIDTH))],
        out_specs=pl.BlockSpec((ATTN_ROWS, ATTN_WIDTH), lambda b, i: (row(b, i), 0)),
        scratch_shapes=[pltpu.VMEM((n_kv, WINDOW, LANES), BF16),
                        pltpu.VMEM((n_kv, WINDOW, LANES), BF16)],
        compiler_params=pltpu.CompilerParams(dimension_semantics=("parallel", "arbitrary"),
                                             vmem_limit_bytes=VMEM_LIMIT),
        name="swa_attn",
    )(sinks, proj, proj, proj, proj, proj)


DECAY_SCALE = math.exp(-0.5)
RWKV_TT = 1024
LINK_PERIOD = 13


def _rwkv_kernel(r_ref, k_ref, v_ref, g_ref, wa_ref,
                 mu_r_ref, mu_k_ref, mu_v_ref, mu_wa_ref,
                 w0_ref, wup_ref, a0_ref, aup_ref, kk_ref, ka_ref, rk_ref, lnw_ref, lnb_ref,
                 o_ref, state_ref, last_ref, m_st, n_st, rhat_st, y0_st, post_st, *, nt, total):
    s = pl.program_id(0)
    TT = RWKV_TT
    L = CHUNK
    t_in = jnp.minimum(s, total - 1) % nt
    t_out = jnp.maximum(s - 1, 0) % nt
    cur = s % 2
    prv = 1 - cur

    @pl.when(s == 0)
    def _():
        for ref in (state_ref, last_ref, m_st, n_st, rhat_st, y0_st, post_st):
            ref[...] = jnp.zeros_like(ref)

    chunks = range(TT // L)
    cs = [slice(c * L, (c + 1) * L) for c in chunks]

    lane_lo = lax.broadcasted_iota(jnp.int32, (L, LANES), 1) < HEAD_DIM

    def diag2(x):
        zero = jnp.zeros_like(x)
        return jnp.concatenate([jnp.where(lane_lo, x, zero), jnp.where(lane_lo, zero, x)], axis=0).astype(BF16)

    carry = {"H": jnp.where(t_out == 0, 0.0, state_ref[...]), "ys": [], "next": 0}

    def advance():
        c = carry["next"]
        if c >= len(chunks):
            return
        H = diag2(carry["H"])
        carry["ys"].append(jnp.dot(rhat_st[prv, cs[c], :], H, preferred_element_type=F32) + y0_st[prv, cs[c], :])
        carry["H"] = jnp.dot(m_st[prv, c], H, preferred_element_type=F32) + n_st[prv, c]
        carry["next"] = c + 1

    row0 = lax.broadcasted_iota(jnp.int32, (8, LANES), 0) == 0
    first = t_in == 0

    def shifted(ref, slot, mu_ref):
        x = ref[...].astype(F32)
        carried = jnp.where(first, 0.0, last_ref[slot, 7:8, :])
        rolled = pltpu.roll(x, 1, axis=0)
        prev = jnp.concatenate([jnp.where(row0, carried, rolled[:8]), rolled[8:]], axis=0)
        last_ref[slot] = x[TT - 8:, :]
        return x + (prev - x) * mu_ref[...]

    r = shifted(r_ref, 0, mu_r_ref)
    k = shifted(k_ref, 1, mu_k_ref)
    v = shifted(v_ref, 2, mu_v_ref)
    wa = shifted(wa_ref, 3, mu_wa_ref)
    wd = wa[:, :LORA]
    ad = wa[:, LORA:]

    u = w0_ref[...] + _dot(jnp.tanh(wd), wup_ref[...])
    lw = -DECAY_SCALE / (1.0 + jnp.exp(-u))
    a = _sigmoid(a0_ref[...] + _dot(ad, aup_ref[...]))
    advance()

    lane_r = lax.broadcasted_iota(jnp.int32, (LANES, LANES), 0)
    lane_c = lax.broadcasted_iota(jnp.int32, (LANES, LANES), 1)
    head_blk = (lane_r // HEAD_DIM) == (lane_c // HEAD_DIM)
    ones_blk = jnp.where(head_blk, 1.0, 0.0).astype(BF16)

    kk = k * kk_ref[...]
    kk = kk * lax.rsqrt(jnp.maximum(_dot(kk * kk, ones_blk), 1e-24))
    k = k * ((1.0 - ka_ref[...]) + a * ka_ref[...])
    av = -kk
    bv = kk * a
    post_st[cur, 0] = _dot(r * k * rk_ref[...], ones_blk) * v
    post_st[cur, 1] = _silu(g_ref[...].astype(F32))

    TB = 4 * L
    tr = lax.broadcasted_iota(jnp.int32, (TB, TB), 0)
    tc = lax.broadcasted_iota(jnp.int32, (TB, TB), 1)
    tri = jnp.where(jnp.logical_and(tr // L == tc // L, tc <= tr), 1.0, 0.0).astype(BF16)
    cum = jnp.concatenate([_dot_exact_lhs(tri, lw[i * TB:(i + 1) * TB]) for i in range(TT // TB)], axis=0)
    advance()

    e_pos = jnp.exp(cum)
    e_neg = 1.0 / e_pos
    r_t = r * e_pos
    k_t = k * e_neg
    b_t = bv * e_neg
    a_t = av * jnp.exp(cum - lw)

    row = lax.broadcasted_iota(jnp.int32, (L, LANES), 0)
    col = lax.broadcasted_iota(jnp.int32, (L, LANES), 1) % HEAD_DIM
    strict = col < row
    incl = col <= row
    eye = col == row

    ticks = {"n": 0}

    def each(fn):
        out = []
        for c in chunks:
            out.append(fn(c))
            ticks["n"] += 1
            if ticks["n"] % LINK_PERIOD == 0:
                advance()
        return out

    cum_l = [cum[sl][L - 1:L, :] for sl in cs]
    p_l = [jnp.exp(cum_l[c]) for c in chunks]
    bk_d = [jnp.concatenate([diag2(b_t[sl]), diag2(k_t[sl])], axis=0) for sl in cs]
    v_d = [diag2(v[sl]) for sl in cs]
    gram = each(lambda c: _dot_nt(jnp.concatenate([a_t[cs[c]], r_t[cs[c]]], axis=0), bk_d[c]))
    a_ab = [jnp.where(strict, gram[c][:L, :LANES], 0.0) for c in chunks]
    a_ak = [jnp.where(strict, gram[c][:L, LANES:], 0.0).astype(BF16) for c in chunks]
    a_rb = [jnp.where(incl, gram[c][L:, :LANES], 0.0).astype(BF16) for c in chunks]
    a_rk = [jnp.where(incl, gram[c][L:, LANES:], 0.0).astype(BF16) for c in chunks]
    inv = [jnp.where(eye, 1.0, 0.0) + a_ab[c] for c in chunks]
    pw = each(lambda c: _dot(a_ab[c], diag2(a_ab[c])).astype(BF16))
    w_s = each(lambda c: _dot(a_ak[c], v_d[c]).astype(BF16))
    for _ in range(4):
        both = each(lambda c: _dot(pw[c], jnp.concatenate([diag2(pw[c]), diag2(inv[c])], axis=1)))
        pw = [both[c][:, :LANES].astype(BF16) for c in chunks]
        inv = [inv[c] + both[c][:, LANES:] for c in chunks]
    last = each(lambda c: _dot(pw[c], diag2(inv[c])))
    inv = [inv[c] + last[c] for c in chunks]
    sol = each(lambda c: _dot(inv[c], jnp.concatenate([diag2(a_t[cs[c]]), diag2(w_s[c])], axis=1)))
    ahat_d = [diag2(sol[c][:, :LANES]) for c in chunks]
    u0_d = [diag2(sol[c][:, LANES:]) for c in chunks]

    def heads_down(x):
        return jnp.concatenate([x[:, :HEAD_DIM], x[:, HEAD_DIM:]], axis=0)

    bmn = each(lambda c: _dot_tn(heads_down(b_t[cs[c]] * p_l[c]),
                                 jnp.concatenate([ahat_d[c], u0_d[c]], axis=1)))
    kv = each(lambda c: _dot_tn(heads_down(k_t[cs[c]] * p_l[c]), v_d[c]))
    rb = each(lambda c: _dot(a_rb[c], ahat_d[c]))
    y0 = each(lambda c: _dot(jnp.concatenate([a_rb[c], a_rk[c]], axis=1),
                             jnp.concatenate([u0_d[c], v_d[c]], axis=0)))
    while carry["next"] < len(chunks):
        advance()

    state_ref[...] = carry["H"]
    y = jnp.concatenate(carry["ys"], axis=0)
    mean_blk = jnp.where(head_blk, 1.0 / HEAD_DIM, 0.0).astype(BF16)
    mean = _dot(y, mean_blk)
    yc = y - mean
    var = _dot(yc * yc, mean_blk)
    yn = yc * lax.rsqrt(var + GN_EPS) * lnw_ref[...] + lnb_ref[...]
    o_ref[...] = ((yn + post_st[prv, 0]) * post_st[prv, 1]).astype(o_ref.dtype)

    for c in chunks:
        m_st[cur, c] = (bmn[c][:, :LANES] + jnp.where(eye, p_l[c], 0.0)).astype(BF16)
        n_st[cur, c] = bmn[c][:, LANES:] + kv[c]
        rhat_st[cur, cs[c], :] = (r_t[cs[c]] + rb[c]).astype(BF16)
        y0_st[cur, cs[c], :] = y0[c]


def _rwkv(proj, mu, w0, w_up, a0, a_up, k_k, k_a, r_k, ln_w, ln_b, batch, seq):
    M = proj.shape[0]
    TT = RWKV_TT
    nt = seq // TT
    n_chunks = TT // CHUNK
    total = batch * PAIRS * nt

    def where(s):
        b, p, t = s // (PAIRS * nt), (s // nt) % PAIRS, s % nt
        return b * nt + t, p

    src = lambda s: where(jnp.minimum(s, total - 1))
    dst = lambda s: where(jnp.maximum(s - 1, 0))

    slab = lambda col: pl.BlockSpec((TT, LANES), lambda s: (src(s)[0], col // LANES + src(s)[1]))
    vec = pl.BlockSpec((1, LANES), lambda s: (0, src(s)[1]))
    vec0 = pl.BlockSpec((1, LANES), lambda s: (0, 0))
    vec_dst = pl.BlockSpec((1, LANES), lambda s: (0, dst(s)[1]))
    lora = pl.BlockSpec((LORA, LANES), lambda s: (0, src(s)[1]))
    mu_r, mu_k, mu_v, mu_wa = (mu[:, :1024], mu[:, 1024:2048], mu[:, 2048:3072], mu[:, 3072:])
    return pl.pallas_call(
        functools.partial(_rwkv_kernel, nt=nt, total=total),
        out_shape=jax.ShapeDtypeStruct((M, RWKV_WIDTH), BF16),
        grid=(total + 1,),
        in_specs=[slab(COL_R), slab(COL_RK), slab(COL_RV), slab(COL_RG),
                  pl.BlockSpec((TT, LANES), lambda s: (src(s)[0], COL_WA // LANES)),
                  vec, vec, vec, vec0,
                  vec, lora, vec, lora, vec, vec, vec, vec_dst, vec_dst],
        out_specs=pl.BlockSpec((TT, LANES), lambda s: dst(s)),
        scratch_shapes=[pltpu.VMEM((CHUNK, LANES), F32),
                        pltpu.VMEM((4, 8, LANES), F32),
                        pltpu.VMEM((2, n_chunks, CHUNK, LANES), BF16),
                        pltpu.VMEM((2, n_chunks, CHUNK, LANES), F32),
                        pltpu.VMEM((2, TT, LANES), BF16),
                        pltpu.VMEM((2, TT, LANES), F32),
                        pltpu.VMEM((2, 2, TT, LANES), F32)],
        compiler_params=pltpu.CompilerParams(dimension_semantics=("arbitrary",),
                                             vmem_limit_bytes=VMEM_LIMIT),
        name="rwkv7_mix",
    )(proj, proj, proj, proj, proj, mu_r, mu_k, mu_v, mu_wa,
      w0, w_up, a0, a_up, k_k, k_a, r_k, ln_w, ln_b)


def _out_proj_kernel(x_ref, ya_ref, yr_ref, wa_ref, wr_ref, g_ref, gate_ref, o_ref):
    mix = (jnp.dot(ya_ref[...], wa_ref[...], preferred_element_type=F32)
           + jnp.dot(yr_ref[...], wr_ref[...], preferred_element_type=F32))
    inv = lax.rsqrt(jnp.mean(mix * mix, axis=-1, keepdims=True) + RMS_EPS)
    o_ref[...] = x_ref[...] + gate_ref[0] * ((mix * inv) * g_ref[...])


def _out_proj(x2, ya, yr, w, g, gate, seq):
    M, D = x2.shape
    tm = 512
    per_b = seq // tm
    half = ya.shape[1]
    return pl.pallas_call(
        _out_proj_kernel,
        out_shape=jax.ShapeDtypeStruct((M, D), F32),
        grid=(M // tm,),
        in_specs=[pl.BlockSpec((tm, D), lambda i: (i, 0)),
                  pl.BlockSpec((tm, half), lambda i: (i, 0)),
                  pl.BlockSpec((tm, half), lambda i: (i, 0)),
                  pl.BlockSpec((half, D), lambda i: (0, 0)),
                  pl.BlockSpec((half, D), lambda i: (1, 0)),
                  pl.BlockSpec((1, D), lambda i: (0, 0)),
                  pl.BlockSpec((1, 1, D), lambda i: (i // per_b, 0, 0))],
        out_specs=pl.BlockSpec((tm, D), lambda i: (i, 0)),
        compiler_params=pltpu.CompilerParams(dimension_semantics=("parallel",),
                                             vmem_limit_bytes=VMEM_LIMIT),
        name="out_proj",
    )(x2, ya, yr, w, w, g, gate)


def _pack_w_in(w):
    return jnp.zeros((w.shape[0], IN_COLS_PAD), BF16).at[:, :IN_COLS].set(w.astype(BF16))


def kernel(x, c, w_ada, b_ada, pre_norm_g, post_norm_g, w_in, w_out, attn_sinks, rwkv_mu, rwkv_w0,
           rwkv_w_up, rwkv_a0, rwkv_a_up, rwkv_k_k, rwkv_k_a, rwkv_r_k, rwkv_ln_w, rwkv_ln_b):
    B, T, D = x.shape
    depth = w_ada.shape[0]
    x2 = x.reshape(B * T, D)
    for l in range(depth):
        mod = _adaln(c, w_ada[l], b_ada[l][None, :])
        shift, scale, gate = (mod[:, i * D:(i + 1) * D].reshape(B, 1, D) for i in range(3))
        proj = _in_proj(x2, pre_norm_g[l][None, :], scale, shift, _pack_w_in(w_in[l]), T)
        y_attn = _attention(proj, attn_sinks[l], B, T)
        y_rwkv = _rwkv(proj, rwkv_mu[l][None, :], rwkv_w0[l][None, :], rwkv_w_up[l].astype(BF16),
                       rwkv_a0[l][None, :], rwkv_a_up[l].astype(BF16), rwkv_k_k[l][None, :],
                       rwkv_k_a[l][None, :], rwkv_r_k[l].reshape(1, RWKV_WIDTH),
                       rwkv_ln_w[l][None, :], rwkv_ln_b[l][None, :], B, T)
        x2 = _out_proj(x2, y_attn, y_rwkv, w_out[l].astype(BF16), post_norm_g[l][None, :], gate, T)
    return x2.reshape(B, T, D)
```

```python
import functools
import math

import jax
import jax.numpy as jnp
from jax import lax
from jax.experimental import pallas as pl
from jax.experimental.pallas import tpu as pltpu

D_MODEL = 2048
HEAD_DIM = 64
ATTN_WIDTH = 1024
ATTN_Q_HEADS = 16
KV_WIDTH = 256
WINDOW = 128
RWKV_WIDTH = 1024
LORA = 64
IN_COLS = 6784
RMS_EPS = 1e-6
GN_EPS = 64e-5
NEG_BIG = -1e30

LANES = 128
PAIRS = RWKV_WIDTH // LANES
CHUNK = 64

COL_Q = 0
COL_K = 1024
COL_V = 1280
COL_GA = 1536
COL_R = 2560
COL_RK = 3584
COL_RV = 4608
COL_WA = 5632
COL_RG = 5760
GA_BLOCK = 512

VMEM_LIMIT = 56 * 1024 * 1024

F32 = jnp.float32
BF16 = jnp.bfloat16


def _dot(a, b):
    return jnp.dot(a.astype(BF16), b.astype(BF16), preferred_element_type=F32)


def _dot_nt(a, b):
    return lax.dot_general(a.astype(BF16), b.astype(BF16), (((1,), (1,)), ((), ())),
                           preferred_element_type=F32)


def _dot_tn(a, b):
    return lax.dot_general(a.astype(BF16), b.astype(BF16), (((0,), (0,)), ((), ())),
                           preferred_element_type=F32)


def _split2(x):
    hi = x.astype(BF16)
    return hi, (x - hi.astype(F32)).astype(BF16)


def _dot_exact_lhs(a_bf16, b):
    hi, lo = _split2(b)
    return jnp.dot(a_bf16, hi, preferred_element_type=F32) + jnp.dot(a_bf16, lo, preferred_element_type=F32)


def _sigmoid(x):
    return 1.0 / (1.0 + jnp.exp(-x))


def _silu(x):
    return x * _sigmoid(x)


def _adaln_kernel(c_ref, w_ref, b_ref, o_ref):
    s = _silu(c_ref[...])
    s_hi, s_lo = _split2(s)
    w_hi, w_lo = _split2(w_ref[...])
    d = lambda p, q: jnp.dot(p, q, preferred_element_type=F32)
    o_ref[...] = d(s_hi, w_hi) + (d(s_hi, w_lo) + d(s_lo, w_hi)) + b_ref[...]


def _adaln(c, w, b):
    B, D = c.shape
    N = w.shape[1]
    tn = 1024
    return pl.pallas_call(
        _adaln_kernel,
        out_shape=jax.ShapeDtypeStruct((B, N), F32),
        grid=(N // tn,),
        in_specs=[pl.BlockSpec((B, D), lambda j: (0, 0)),
                  pl.BlockSpec((D, tn), lambda j: (0, j)),
                  pl.BlockSpec((1, tn), lambda j: (0, j))],
        out_specs=pl.BlockSpec((B, tn), lambda j: (0, j)),
        compiler_params=pltpu.CompilerParams(dimension_semantics=("arbitrary",),
                                             vmem_limit_bytes=VMEM_LIMIT),
        name="adaln_mod",
    )(c, w, b)


NORM_ROWS = 16


def _in_proj_kernel(x_ref, g_ref, scale_ref, shift_ref, w_ref, o_ref, h_ref):
    @pl.when(pl.program_id(1) == 0)
    def _():
        gain = g_ref[...] * (1.0 + scale_ref[0])
        shift = shift_ref[0]

        def rows(c, carry):
            sl = pl.ds(pl.multiple_of(c * NORM_ROWS, NORM_ROWS), NORM_ROWS)
            x = x_ref[sl, :]
            inv = lax.rsqrt(jnp.mean(x * x, axis=-1, keepdims=True) + RMS_EPS)
            h_ref[sl, :] = ((x * inv) * gain + shift).astype(BF16)
            return carry

        lax.fori_loop(0, x_ref.shape[0] // NORM_ROWS, rows, 0, unroll=8)

    o_ref[...] = jnp.dot(h_ref[...], w_ref[...], preferred_element_type=F32).astype(o_ref.dtype)


def _in_proj(x2, g, scale, shift, w_packed, seq):
    M, D = x2.shape
    NP = w_packed.shape[1]
    tm, tn = 512, 2304
    per_b = seq // tm
    return pl.pallas_call(
        _in_proj_kernel,
        out_shape=jax.ShapeDtypeStruct((M, NP), BF16),
        grid=(M // tm, pl.cdiv(NP, tn)),
        in_specs=[pl.BlockSpec((tm, D), lambda i, j: (i, 0)),
                  pl.BlockSpec((1, D), lambda i, j: (0, 0)),
                  pl.BlockSpec((1, 1, D), lambda i, j: (i // per_b, 0, 0)),
                  pl.BlockSpec((1, 1, D), lambda i, j: (i // per_b, 0, 0)),
                  pl.BlockSpec((D, tn), lambda i, j: (0, j))],
        out_specs=pl.BlockSpec((tm, tn), lambda i, j: (i, j)),
        scratch_shapes=[pltpu.VMEM((tm, D), BF16)],
        compiler_params=pltpu.CompilerParams(dimension_semantics=("parallel", "arbitrary"),
                                             vmem_limit_bytes=VMEM_LIMIT),
        name="in_proj",
    )(x2, g, scale, shift, w_packed)


ATTN_GROUP = 8
ATTN_ROWS = 4 * WINDOW


def _dup_halves(slab, lane_lo):
    swapped = pltpu.roll(slab, HEAD_DIM, axis=1)
    return jnp.where(lane_lo, slab, swapped), jnp.where(lane_lo, swapped, slab)


def _attn_kernel(sink_ref, q_ref, ga_lo_ref, ga_hi_ref, k_ref, v_ref, o_ref, kprev_ref, vprev_ref):
    i = pl.program_id(1)
    blk = WINDOW
    lane_lo = lax.broadcasted_iota(jnp.int32, (blk, LANES), 1) < HEAD_DIM

    @pl.when(i == 0)
    def _():
        kprev_ref[...] = jnp.zeros_like(kprev_ref)
        vprev_ref[...] = jnp.zeros_like(vprev_ref)

    ones = jnp.ones((blk, LANES), BF16)
    qi = lax.broadcasted_iota(jnp.int32, (2 * blk, blk), 0) % blk
    si = lax.broadcasted_iota(jnp.int32, (2 * blk, blk), 1)
    cur_ok = si <= qi
    scale = HEAD_DIM ** -0.5
    group = ATTN_Q_HEADS // (KV_WIDTH // HEAD_DIM)
    n_kv = KV_WIDTH // HEAD_DIM
    row_lo = lax.broadcasted_iota(jnp.int32, (2 * blk, 1), 0) < blk

    kprev = [kprev_ref[g] for g in range(n_kv)]
    vprev = [vprev_ref[g] for g in range(n_kv)]
    for w in range(ATTN_ROWS // blk):
        rows = slice(w * blk, (w + 1) * blk)
        prev_ok = jnp.logical_and(si > qi, i > 0) if w == 0 else si > qi

        kcur, vcur = [], []
        for s in range(KV_WIDTH // LANES):
            ks = k_ref[rows, s * LANES:(s + 1) * LANES].astype(F32)
            vs = v_ref[rows, s * LANES:(s + 1) * LANES].astype(F32)
            kcur.extend(t.astype(BF16) for t in _dup_halves(ks, lane_lo))
            vcur.extend(t.astype(BF16) for t in _dup_halves(vs, lane_lo))

        def stacked_q(p):
            q = q_ref[rows, p * LANES:(p + 1) * LANES].astype(F32) * scale
            zero = jnp.zeros_like(q)
            return jnp.concatenate([jnp.where(lane_lo, q, zero), jnp.where(lane_lo, zero, q)], axis=0).astype(BF16)

        for p0 in range(0, PAIRS, ATTN_GROUP):
            ps = range(p0, p0 + ATTN_GROUP)
            kvh = {p: (2 * p) // group for p in ps}
            qs = {p: stacked_q(p) for p in ps}
            s = {p: jnp.where(cur_ok, _dot_nt(qs[p], kcur[kvh[p]]),
                              jnp.where(prev_ok, _dot_nt(qs[p], kprev[kvh[p]]), NEG_BIG)) for p in ps}
            sink = {p: jnp.where(row_lo, sink_ref[2 * p], sink_ref[2 * p + 1]) for p in ps}
            m = {p: jnp.max(s[p], axis=-1, keepdims=True) for p in ps}
            e = {p: jnp.exp(s[p] - m[p]) for p in ps}
            e2 = {p: jnp.concatenate([jnp.where(cur_ok, e[p], 0.0), jnp.where(cur_ok, 0.0, e[p])],
                                     axis=1).astype(BF16) for p in ps}
            acc = {p: jnp.dot(e2[p], jnp.concatenate([jnp.concatenate([vcur[kvh[p]], ones], axis=1),
                                                      jnp.concatenate([vprev[kvh[p]], ones], axis=1)], axis=0),
                              preferred_element_type=F32) for p in ps}
            for p in ps:
                num = acc[p][:, :LANES]
                den = acc[p][:, LANES:] + jnp.exp(sink[p] - m[p])
                o = jnp.where(lane_lo, num[:blk], num[blk:]) / jnp.where(lane_lo, den[:blk], den[blk:])
                ga_ref, gp = (ga_lo_ref, p) if p < PAIRS // 2 else (ga_hi_ref, p - PAIRS // 2)
                ga = ga_ref[rows, gp * LANES:(gp + 1) * LANES].astype(F32)
                o_ref[rows, p * LANES:(p + 1) * LANES] = (o * _silu(ga)).astype(o_ref.dtype)
        kprev, vprev = kcur, vcur

    for g in range(n_kv):
        kprev_ref[g] = kprev[g]
        vprev_ref[g] = vprev[g]


def _attention(proj, sinks, batch, seq):
    M = proj.shape[0]
    nb = seq // ATTN_ROWS
    row = lambda b, i: b * nb + i
    n_kv = KV_WIDTH // HEAD_DIM
    return pl.pallas_call(
        _attn_kernel,
        out_shape=jax.ShapeDtypeStruct((M, ATTN_WIDTH), BF16),
        grid=(batch, nb),
        in_specs=[pl.BlockSpec(memory_space=pltpu.SMEM),
                  pl.BlockSpec((ATTN_ROWS, ATTN_WIDTH), lambda b, i: (row(b, i), COL_Q // ATTN_WIDTH)),
                  pl.BlockSpec((ATTN_ROWS, GA_BLOCK), lambda b, i: (row(b, i), COL_GA // GA_BLOCK)),
                  pl.BlockSpec((ATTN_ROWS, GA_BLOCK), lambda b, i: (row(b, i), COL_GA // GA_BLOCK + 1)),
                  pl.BlockSpec((ATTN_ROWS, KV_WIDTH), lambda b, i: (row(b, i), COL_K // KV_WIDTH)),
                  pl.BlockSpec((ATTN_ROWS, KV_WIDTH), lambda b, i: (row(b, i), COL_V // KV_WIDTH))],
        out_specs=pl.BlockSpec((ATTN_ROWS, ATTN_WIDTH), lambda b, i: (row(b, i), 0)),
        scratch_shapes=[pltpu.VMEM((n_kv, WINDOW, LANES), BF16),
                        pltpu.VMEM((n_kv, WINDOW, LANES), BF16)],
        compiler_params=pltpu.CompilerParams(dimension_semantics=("parallel", "arbitrary"),
                                             vmem_limit_bytes=VMEM_LIMIT),
        name="swa_attn",
    )(sinks, proj, proj, proj, proj, proj)


DECAY_SCALE = math.exp(-0.5)
RWKV_TT = 2048
LINK_PERIOD = 13


def _rwkv_kernel(r_ref, k_ref, v_ref, g_ref, wa_ref,
                 mu_r_ref, mu_k_ref, mu_v_ref, mu_wa_ref,
                 w0_ref, wup_ref, a0_ref, aup_ref, kk_ref, ka_ref, rk_ref, lnw_ref, lnb_ref,
                 o_ref, state_ref, last_ref, m_st, n_st, rhat_st, y0_st, post_st, *, nt, total):
    s = pl.program_id(0)
    TT = RWKV_TT
    L = CHUNK
    t_in = jnp.minimum(s, total - 1) % nt
    t_out = jnp.maximum(s - 1, 0) % nt
    cur = s % 2
    prv = 1 - cur

    @pl.when(s == 0)
    def _():
        for ref in (state_ref, last_ref, m_st, n_st, rhat_st, y0_st, post_st):
            ref[...] = jnp.zeros_like(ref)

    chunks = range(TT // L)
    cs = [slice(c * L, (c + 1) * L) for c in chunks]

    lane_lo = lax.broadcasted_iota(jnp.int32, (L, LANES), 1) < HEAD_DIM

    def diag2(x):
        zero = jnp.zeros_like(x)
        return jnp.concatenate([jnp.where(lane_lo, x, zero), jnp.where(lane_lo, zero, x)], axis=0).astype(BF16)

    carry = {"H": jnp.where(t_out == 0, 0.0, state_ref[...]), "ys": [], "next": 0}

    def advance():
        c = carry["next"]
        if c >= len(chunks):
            return
        H = diag2(carry["H"])
        carry["ys"].append(jnp.dot(rhat_st[prv, cs[c], :], H, preferred_element_type=F32) + y0_st[prv, cs[c], :])
        carry["H"] = jnp.dot(m_st[prv, c], H, preferred_element_type=F32) + n_st[prv, c]
        carry["next"] = c + 1

    row0 = lax.broadcasted_iota(jnp.int32, (8, LANES), 0) == 0
    first = t_in == 0

    def shifted(ref, slot, mu_ref):
        x = ref[...].astype(F32)
        carried = jnp.where(first, 0.0, last_ref[slot, 7:8, :])
        rolled = pltpu.roll(x, 1, axis=0)
        prev = jnp.concatenate([jnp.where(row0, carried, rolled[:8]), rolled[8:]], axis=0)
        last_ref[slot] = x[TT - 8:, :]
        return x + (prev - x) * mu_ref[...]

    r = shifted(r_ref, 0, mu_r_ref)
    k = shifted(k_ref, 1, mu_k_ref)
    v = shifted(v_ref, 2, mu_v_ref)
    wa = shifted(wa_ref, 3, mu_wa_ref)
    wd = wa[:, :LORA]
    ad = wa[:, LORA:]

    u = w0_ref[...] + _dot(jnp.tanh(wd), wup_ref[...])
    lw = -DECAY_SCALE / (1.0 + jnp.exp(-u))
    a = _sigmoid(a0_ref[...] + _dot(ad, aup_ref[...]))
    advance()

    lane_r = lax.broadcasted_iota(jnp.int32, (LANES, LANES), 0)
    lane_c = lax.broadcasted_iota(jnp.int32, (LANES, LANES), 1)
    head_blk = (lane_r // HEAD_DIM) == (lane_c // HEAD_DIM)
    ones_blk = jnp.where(head_blk, 1.0, 0.0).astype(BF16)

    kk = k * kk_ref[...]
    kk = kk * lax.rsqrt(jnp.maximum(_dot(kk * kk, ones_blk), 1e-24))
    k = k * ((1.0 - ka_ref[...]) + a * ka_ref[...])
    av = -kk
    bv = kk * a
    post_st[cur, 0] = _dot(r * k * rk_ref[...], ones_blk) * v
    post_st[cur, 1] = _silu(g_ref[...].astype(F32))

    TB = 4 * L
    tr = lax.broadcasted_iota(jnp.int32, (TB, TB), 0)
    tc = lax.broadcasted_iota(jnp.int32, (TB, TB), 1)
    tri = jnp.where(jnp.logical_and(tr // L == tc // L, tc <= tr), 1.0, 0.0).astype(BF16)
    cum = jnp.concatenate([_dot_exact_lhs(tri, lw[i * TB:(i + 1) * TB]) for i in range(TT // TB)], axis=0)
    advance()

    e_pos = jnp.exp(cum)
    e_neg = 1.0 / e_pos
    r_t = r * e_pos
    k_t = k * e_neg
    b_t = bv * e_neg
    a_t = av * jnp.exp(cum - lw)

    row = lax.broadcasted_iota(jnp.int32, (L, LANES), 0)
    col = lax.broadcasted_iota(jnp.int32, (L, LANES), 1) % HEAD_DIM
    strict = col < row
    incl = col <= row
    eye = col == row

    ticks = {"n": 0}

    def each(fn):
        out = []
        for c in chunks:
            out.append(fn(c))
            ticks["n"] += 1
            if ticks["n"] % LINK_PERIOD == 0:
                advance()
        return out

    cum_l = [cum[sl][L - 1:L, :] for sl in cs]
    p_l = [jnp.exp(cum_l[c]) for c in chunks]
    bk_d = [jnp.concatenate([diag2(b_t[sl]), diag2(k_t[sl])], axis=0) for sl in cs]
    v_d = [diag2(v[sl]) for sl in cs]
    gram = each(lambda c: _dot_nt(jnp.concatenate([a_t[cs[c]], r_t[cs[c]]], axis=0), bk_d[c]))
    a_ab = [jnp.where(strict, gram[c][:L, :LANES], 0.0) for c in chunks]
    a_ak = [jnp.where(strict, gram[c][:L, LANES:], 0.0).astype(BF16) for c in chunks]
    a_rb = [jnp.where(incl, gram[c][L:, :LANES], 0.0).astype(BF16) for c in chunks]
    a_rk = [jnp.where(incl, gram[c][L:, LANES:], 0.0).astype(BF16) for c in chunks]
    inv = [jnp.where(eye, 1.0, 0.0) + a_ab[c] for c in chunks]
    pw = each(lambda c: _dot(a_ab[c], diag2(a_ab[c])).astype(BF16))
    w_s = each(lambda c: _dot(a_ak[c], v_d[c]).astype(BF16))
    for _ in range(4):
        both = each(lambda c: _dot(pw[c], jnp.concatenate([diag2(pw[c]), diag2(inv[c])], axis=1)))
        pw = [both[c][:, :LANES].astype(BF16) for c in chunks]
        inv = [inv[c] + both[c][:, LANES:] for c in chunks]
    last = each(lambda c: _dot(pw[c], diag2(inv[c])))
    inv = [inv[c] + last[c] for c in chunks]
    sol = each(lambda c: _dot(inv[c], jnp.concatenate([diag2(a_t[cs[c]]), diag2(w_s[c])], axis=1)))
    ahat_d = [diag2(sol[c][:, :LANES]) for c in chunks]
    u0_d = [diag2(sol[c][:, LANES:]) for c in chunks]

    def heads_down(x):
        return jnp.concatenate([x[:, :HEAD_DIM], x[:, HEAD_DIM:]], axis=0)

    bmn = each(lambda c: _dot_tn(heads_down(b_t[cs[c]] * p_l[c]),
                                 jnp.concatenate([ahat_d[c], u0_d[c]], axis=1)))
    kv = each(lambda c: _dot_tn(heads_down(k_t[cs[c]] * p_l[c]), v_d[c]))
    rb = each(lambda c: _dot(a_rb[c], ahat_d[c]))
    y0 = each(lambda c: _dot(jnp.concatenate([a_rb[c], a_rk[c]], axis=1),
                             jnp.concatenate([u0_d[c], v_d[c]], axis=0)))
    while carry["next"] < len(chunks):
        advance()

    state_ref[...] = carry["H"]
    y = jnp.concatenate(carry["ys"], axis=0)
    mean_blk = jnp.where(head_blk, 1.0 / HEAD_DIM, 0.0).astype(BF16)
    mean = _dot(y, mean_blk)
    yc = y - mean
    var = _dot(yc * yc, mean_blk)
    yn = yc * lax.rsqrt(var + GN_EPS) * lnw_ref[...] + lnb_ref[...]
    o_ref[...] = ((yn + post_st[prv, 0]) * post_st[prv, 1]).astype(o_ref.dtype)

    for c in chunks:
        m_st[cur, c] = (bmn[c][:, :LANES] + jnp.where(eye, p_l[c], 0.0)).astype(BF16)
        n_st[cur, c] = bmn[c][:, LANES:] + kv[c]
        rhat_st[cur, cs[c], :] = (r_t[cs[c]] + rb[c]).astype(BF16)
        y0_st[cur, cs[c], :] = y0[c]


def _rwkv(proj, mu, w0, w_up, a0, a_up, k_k, k_a, r_k, ln_w, ln_b, batch, seq):
    M = proj.shape[0]
    TT = RWKV_TT
    nt = seq // TT
    n_chunks = TT // CHUNK
    total = batch * PAIRS * nt

    def where(s):
        b, p, t = s // (PAIRS * nt), (s // nt) % PAIRS, s % nt
        return b * nt + t, p

    src = lambda s: where(jnp.minimum(s, total - 1))
    dst = lambda s: where(jnp.maximum(s - 1, 0))

    slab = lambda col: pl.BlockSpec((TT, LANES), lambda s: (src(s)[0], col // LANES + src(s)[1]))
    vec = pl.BlockSpec((1, LANES), lambda s: (0, src(s)[1]))
    vec0 = pl.BlockSpec((1, LANES), lambda s: (0, 0))
    vec_dst = pl.BlockSpec((1, LANES), lambda s: (0, dst(s)[1]))
    lora = pl.BlockSpec((LORA, LANES), lambda s: (0, src(s)[1]))
    mu_r, mu_k, mu_v, mu_wa = (mu[:, :1024], mu[:, 1024:2048], mu[:, 2048:3072], mu[:, 3072:])
    return pl.pallas_call(
        functools.partial(_rwkv_kernel, nt=nt, total=total),
        out_shape=jax.ShapeDtypeStruct((M, RWKV_WIDTH), BF16),
        grid=(total + 1,),
        in_specs=[slab(COL_R), slab(COL_RK), slab(COL_RV), slab(COL_RG),
                  pl.BlockSpec((TT, LANES), lambda s: (src(s)[0], COL_WA // LANES)),
                  vec, vec, vec, vec0,
                  vec, lora, vec, lora, vec, vec, vec, vec_dst, vec_dst],
        out_specs=pl.BlockSpec((TT, LANES), lambda s: dst(s)),
        scratch_shapes=[pltpu.VMEM((CHUNK, LANES), F32),
                        pltpu.VMEM((4, 8, LANES), F32),
                        pltpu.VMEM((2, n_chunks, CHUNK, LANES), BF16),
                        pltpu.VMEM((2, n_chunks, CHUNK, LANES), F32),
                        pltpu.VMEM((2, TT, LANES), BF16),
                        pltpu.VMEM((2, TT, LANES), F32),
                        pltpu.VMEM((2, 2, TT, LANES), F32)],
        compiler_params=pltpu.CompilerParams(dimension_semantics=("arbitrary",),
                                             vmem_limit_bytes=VMEM_LIMIT),
        name="rwkv7_mix",
    )(proj, proj, proj, proj, proj, mu_r, mu_k, mu_v, mu_wa,
      w0, w_up, a0, a_up, k_k, k_a, r_k, ln_w, ln_b)


def _out_proj_kernel(x_ref, ya_ref, yr_ref, wa_ref, wr_ref, g_ref, gate_ref, o_ref):
    mix = (jnp.dot(ya_ref[...], wa_ref[...], preferred_element_type=F32)
           + jnp.dot(yr_ref[...], wr_ref[...], preferred_element_type=F32))
    inv = lax.rsqrt(jnp.mean(mix * mix, axis=-1, keepdims=True) + RMS_EPS)
    o_ref[...] = x_ref[...] + gate_ref[0] * ((mix * inv) * g_ref[...])


def _out_proj(x2, ya, yr, w, g, gate, seq):
    M, D = x2.shape
    tm = 512
    per_b = seq // tm
    half = ya.shape[1]
    return pl.pallas_call(
        _out_proj_kernel,
        out_shape=jax.ShapeDtypeStruct((M, D), F32),
        grid=(M // tm,),
        in_specs=[pl.BlockSpec((tm, D), lambda i: (i, 0)),
                  pl.BlockSpec((tm, half), lambda i: (i, 0)),
                  pl.BlockSpec((tm, half), lambda i: (i, 0)),
                  pl.BlockSpec((half, D), lambda i: (0, 0)),
                  pl.BlockSpec((half, D), lambda i: (1, 0)),
                  pl.BlockSpec((1, D), lambda i: (0, 0)),
                  pl.BlockSpec((1, 1, D), lambda i: (i // per_b, 0, 0))],
        out_specs=pl.BlockSpec((tm, D), lambda i: (i, 0)),
        compiler_params=pltpu.CompilerParams(dimension_semantics=("parallel",),
                                             vmem_limit_bytes=VMEM_LIMIT),
        name="out_proj",
    )(x2, ya, yr, w, w, g, gate)


def kernel(x, c, w_ada, b_ada, pre_norm_g, post_norm_g, w_in, w_out, attn_sinks, rwkv_mu, rwkv_w0,
           rwkv_w_up, rwkv_a0, rwkv_a_up, rwkv_k_k, rwkv_k_a, rwkv_r_k, rwkv_ln_w, rwkv_ln_b):
    B, T, D = x.shape
    depth = w_ada.shape[0]
    x2 = x.reshape(B * T, D)
    for l in range(depth):
        mod = _adaln(c, w_ada[l], b_ada[l][None, :])
        shift, scale, gate = (mod[:, i * D:(i + 1) * D].reshape(B, 1, D) for i in range(3))
        proj = _in_proj(x2, pre_norm_g[l][None, :], scale, shift, w_in[l].astype(BF16), T)
        y_attn = _attention(proj, attn_sinks[l], B, T)
        y_rwkv = _rwkv(proj, rwkv_mu[l][None, :], rwkv_w0[l][None, :], rwkv_w_up[l].astype(BF16),
                       rwkv_a0[l][None, :], rwkv_a_up[l].astype(BF16), rwkv_k_k[l][None, :],
                       rwkv_k_a[l][None, :], rwkv_r_k[l].reshape(1, RWKV_WIDTH),
                       rwkv_ln_w[l][None, :], rwkv_ln_b[l][None, :], B, T)
        x2 = _out_proj(x2, y_attn, y_rwkv, w_out[l].astype(BF16), post_norm_g[l][None, :], gate, T)
    return x2.reshape(B, T, D)
```

```python
import functools
import math

import jax
import jax.numpy as jnp
from jax import lax
from jax.experimental import pallas as pl
from jax.experimental.pallas import tpu as pltpu

D_MODEL = 2048
HEAD_DIM = 64
ATTN_WIDTH = 1024
ATTN_Q_HEADS = 16
KV_WIDTH = 256
WINDOW = 128
RWKV_WIDTH = 1024
LORA = 64
IN_COLS = 6784
RMS_EPS = 1e-6
GN_EPS = 64e-5
NEG_BIG = -1e30

LANES = 128
PAIRS = RWKV_WIDTH // LANES
CHUNK = 64

COL_Q = 0
COL_K = 1024
COL_V = 1280
COL_GA = 1536
COL_R = 2560
COL_RK = 3584
COL_RV = 4608
COL_WA = 5632
COL_RG = 5760
GA_BLOCK = 512

VMEM_LIMIT = 56 * 1024 * 1024

F32 = jnp.float32
BF16 = jnp.bfloat16


def _dot(a, b):
    return jnp.dot(a.astype(BF16), b.astype(BF16), preferred_element_type=F32)


def _dot_nt(a, b):
    return lax.dot_general(a.astype(BF16), b.astype(BF16), (((1,), (1,)), ((), ())),
                           preferred_element_type=F32)


def _dot_tn(a, b):
    return lax.dot_general(a.astype(BF16), b.astype(BF16), (((0,), (0,)), ((), ())),
                           preferred_element_type=F32)


def _split2(x):
    hi = x.astype(BF16)
    return hi, (x - hi.astype(F32)).astype(BF16)


def _dot_exact_lhs(a_bf16, b):
    hi, lo = _split2(b)
    return jnp.dot(a_bf16, hi, preferred_element_type=F32) + jnp.dot(a_bf16, lo, preferred_element_type=F32)


def _sigmoid(x):
    return 1.0 / (1.0 + jnp.exp(-x))


def _silu(x):
    return x * _sigmoid(x)


def _adaln_kernel(c_ref, w_ref, b_ref, o_ref):
    s = _silu(c_ref[...])
    s_hi, s_lo = _split2(s)
    w_hi, w_lo = _split2(w_ref[...])
    d = lambda p, q: jnp.dot(p, q, preferred_element_type=F32)
    o_ref[...] = d(s_hi, w_hi) + (d(s_hi, w_lo) + d(s_lo, w_hi)) + b_ref[...]


def _adaln(c, w, b):
    B, D = c.shape
    N = w.shape[1]
    tn = 1024
    return pl.pallas_call(
        _adaln_kernel,
        out_shape=jax.ShapeDtypeStruct((B, N), F32),
        grid=(N // tn,),
        in_specs=[pl.BlockSpec((B, D), lambda j: (0, 0)),
                  pl.BlockSpec((D, tn), lambda j: (0, j)),
                  pl.BlockSpec((1, tn), lambda j: (0, j))],
        out_specs=pl.BlockSpec((B, tn), lambda j: (0, j)),
        compiler_params=pltpu.CompilerParams(dimension_semantics=("arbitrary",),
                                             vmem_limit_bytes=VMEM_LIMIT),
        name="adaln_mod",
    )(c, w, b)


NORM_ROWS = 16


def _in_proj_kernel(x_ref, g_ref, scale_ref, shift_ref, w_ref, o_ref, h_ref):
    @pl.when(pl.program_id(1) == 0)
    def _():
        gain = g_ref[...] * (1.0 + scale_ref[0])
        shift = shift_ref[0]

        def rows(c, carry):
            sl = pl.ds(pl.multiple_of(c * NORM_ROWS, NORM_ROWS), NORM_ROWS)
            x = x_ref[sl, :]
            inv = lax.rsqrt(jnp.mean(x * x, axis=-1, keepdims=True) + RMS_EPS)
            h_ref[sl, :] = ((x * inv) * gain + shift).astype(BF16)
            return carry

        lax.fori_loop(0, x_ref.shape[0] // NORM_ROWS, rows, 0, unroll=8)

    o_ref[...] = jnp.dot(h_ref[...], w_ref[...], preferred_element_type=F32).astype(o_ref.dtype)


def _in_proj(x2, g, scale, shift, w_packed, seq):
    M, D = x2.shape
    NP = w_packed.shape[1]
    tm, tn = 512, 2304
    per_b = seq // tm
    return pl.pallas_call(
        _in_proj_kernel,
        out_shape=jax.ShapeDtypeStruct((M, NP), BF16),
        grid=(M // tm, pl.cdiv(NP, tn)),
        in_specs=[pl.BlockSpec((tm, D), lambda i, j: (i, 0)),
                  pl.BlockSpec((1, D), lambda i, j: (0, 0)),
                  pl.BlockSpec((1, 1, D), lambda i, j: (i // per_b, 0, 0)),
                  pl.BlockSpec((1, 1, D), lambda i, j: (i // per_b, 0, 0)),
                  pl.BlockSpec((D, tn), lambda i, j: (0, j))],
        out_specs=pl.BlockSpec((tm, tn), lambda i, j: (i, j)),
        scratch_shapes=[pltpu.VMEM((tm, D), BF16)],
        compiler_params=pltpu.CompilerParams(dimension_semantics=("parallel", "arbitrary"),
                                             vmem_limit_bytes=VMEM_LIMIT),
        name="in_proj",
    )(x2, g, scale, shift, w_packed)


LOG2_E = math.log2(math.e)
ATTN_GROUP = 8
ATTN_ROWS = 4 * WINDOW


def _dup_halves(slab, lane_lo):
    swapped = pltpu.roll(slab, HEAD_DIM, axis=1)
    return jnp.where(lane_lo, slab, swapped), jnp.where(lane_lo, swapped, slab)


def _attn_kernel(sink_ref, q_ref, ga_lo_ref, ga_hi_ref, k_ref, v_ref, o_ref, kprev_ref, vprev_ref):
    i = pl.program_id(1)
    blk = WINDOW
    lane_lo = lax.broadcasted_iota(jnp.int32, (blk, LANES), 1) < HEAD_DIM

    @pl.when(i == 0)
    def _():
        kprev_ref[...] = jnp.zeros_like(kprev_ref)
        vprev_ref[...] = jnp.zeros_like(vprev_ref)

    ones = jnp.ones((blk, LANES), BF16)
    qi = lax.broadcasted_iota(jnp.int32, (2 * blk, blk), 0) % blk
    si = lax.broadcasted_iota(jnp.int32, (2 * blk, blk), 1)
    cur_ok = si <= qi
    scale = HEAD_DIM ** -0.5 * LOG2_E
    group = ATTN_Q_HEADS // (KV_WIDTH // HEAD_DIM)
    n_kv = KV_WIDTH // HEAD_DIM
    row_lo = lax.broadcasted_iota(jnp.int32, (2 * blk, 1), 0) < blk

    kprev = [kprev_ref[g] for g in range(n_kv)]
    vprev = [vprev_ref[g] for g in range(n_kv)]
    for w in range(ATTN_ROWS // blk):
        rows = slice(w * blk, (w + 1) * blk)
        prev_ok = jnp.logical_and(si > qi, i > 0) if w == 0 else si > qi

        kcur, vcur = [], []
        for s in range(KV_WIDTH // LANES):
            ks = k_ref[rows, s * LANES:(s + 1) * LANES].astype(F32)
            vs = v_ref[rows, s * LANES:(s + 1) * LANES].astype(F32)
            kcur.extend(t.astype(BF16) for t in _dup_halves(ks, lane_lo))
            vcur.extend(t.astype(BF16) for t in _dup_halves(vs, lane_lo))

        def stacked_q(p):
            q = q_ref[rows, p * LANES:(p + 1) * LANES].astype(F32) * scale
            zero = jnp.zeros_like(q)
            return jnp.concatenate([jnp.where(lane_lo, q, zero), jnp.where(lane_lo, zero, q)], axis=0).astype(BF16)

        for p0 in range(0, PAIRS, ATTN_GROUP):
            ps = range(p0, p0 + ATTN_GROUP)
            kvh = {p: (2 * p) // group for p in ps}
            qs = {p: stacked_q(p) for p in ps}
            s = {p: jnp.where(cur_ok, _dot_nt(qs[p], kcur[kvh[p]]),
                              jnp.where(prev_ok, _dot_nt(qs[p], kprev[kvh[p]]), NEG_BIG)) for p in ps}
            sink = {p: jnp.where(row_lo, sink_ref[2 * p] * LOG2_E, sink_ref[2 * p + 1] * LOG2_E) for p in ps}
            m = {p: jnp.max(s[p], axis=-1, keepdims=True) for p in ps}
            e = {p: jnp.exp2(s[p] - m[p]) for p in ps}
            e2 = {p: jnp.concatenate([jnp.where(cur_ok, e[p], 0.0), jnp.where(cur_ok, 0.0, e[p])],
                                     axis=1).astype(BF16) for p in ps}
            acc = {p: jnp.dot(e2[p], jnp.concatenate([jnp.concatenate([vcur[kvh[p]], ones], axis=1),
                                                      jnp.concatenate([vprev[kvh[p]], ones], axis=1)], axis=0),
                              preferred_element_type=F32) for p in ps}
            for p in ps:
                num = acc[p][:, :LANES]
                den = acc[p][:, LANES:] + jnp.exp2(sink[p] - m[p])
                o = jnp.where(lane_lo, num[:blk], num[blk:]) / jnp.where(lane_lo, den[:blk], den[blk:])
                ga_ref, gp = (ga_lo_ref, p) if p < PAIRS // 2 else (ga_hi_ref, p - PAIRS // 2)
                ga = ga_ref[rows, gp * LANES:(gp + 1) * LANES].astype(F32)
                o_ref[rows, p * LANES:(p + 1) * LANES] = (o * _silu(ga)).astype(o_ref.dtype)
        kprev, vprev = kcur, vcur

    for g in range(n_kv):
        kprev_ref[g] = kprev[g]
        vprev_ref[g] = vprev[g]


def _attention(proj, sinks, batch, seq):
    M = proj.shape[0]
    nb = seq // ATTN_ROWS
    row = lambda b, i: b * nb + i
    n_kv = KV_WIDTH // HEAD_DIM
    return pl.pallas_call(
        _attn_kernel,
        out_shape=jax.ShapeDtypeStruct((M, ATTN_WIDTH), BF16),
        grid=(batch, nb),
        in_specs=[pl.BlockSpec(memory_space=pltpu.SMEM),
                  pl.BlockSpec((ATTN_ROWS, ATTN_WIDTH), lambda b, i: (row(b, i), COL_Q // ATTN_WIDTH)),
                  pl.BlockSpec((ATTN_ROWS, GA_BLOCK), lambda b, i: (row(b, i), COL_GA // GA_BLOCK)),
                  pl.BlockSpec((ATTN_ROWS, GA_BLOCK), lambda b, i: (row(b, i), COL_GA // GA_BLOCK + 1)),
                  pl.BlockSpec((ATTN_ROWS, KV_WIDTH), lambda b, i: (row(b, i), COL_K // KV_WIDTH)),
                  pl.BlockSpec((ATTN_ROWS, KV_WIDTH), lambda b, i: (row(b, i), COL_V // KV_WIDTH))],
        out_specs=pl.BlockSpec((ATTN_ROWS, ATTN_WIDTH), lambda b, i: (row(b, i), 0)),
        scratch_shapes=[pltpu.VMEM((n_kv, WINDOW, LANES), BF16),
                        pltpu.VMEM((n_kv, WINDOW, LANES), BF16)],
        compiler_params=pltpu.CompilerParams(dimension_semantics=("parallel", "arbitrary"),
                                             vmem_limit_bytes=VMEM_LIMIT),
        name="swa_attn",
    )(sinks, proj, proj, proj, proj, proj)


DECAY_SCALE = math.exp(-0.5)
RWKV_TT = 1024
LINK_PERIOD = 13


def _rwkv_kernel(r_ref, k_ref, v_ref, g_ref, wa_ref,
                 mu_r_ref, mu_k_ref, mu_v_ref, mu_wa_ref,
                 w0_ref, wup_ref, a0_ref, aup_ref, kk_ref, ka_ref, rk_ref, lnw_ref, lnb_ref,
                 o_ref, state_ref, last_ref, m_st, n_st, rhat_st, y0_st, post_st, *, nt, total):
    s = pl.program_id(0)
    TT = RWKV_TT
    L = CHUNK
    t_in = jnp.minimum(s, total - 1) % nt
    t_out = jnp.maximum(s - 1, 0) % nt
    cur = s % 2
    prv = 1 - cur

    @pl.when(s == 0)
    def _():
        for ref in (state_ref, last_ref, m_st, n_st, rhat_st, y0_st, post_st):
            ref[...] = jnp.zeros_like(ref)

    chunks = range(TT // L)
    cs = [slice(c * L, (c + 1) * L) for c in chunks]

    lane_lo = lax.broadcasted_iota(jnp.int32, (L, LANES), 1) < HEAD_DIM

    def diag2(x):
        zero = jnp.zeros_like(x)
        return jnp.concatenate([jnp.where(lane_lo, x, zero), jnp.where(lane_lo, zero, x)], axis=0).astype(BF16)

    carry = {"H": jnp.where(t_out == 0, 0.0, state_ref[...]), "ys": [], "next": 0}

    def advance():
        c = carry["next"]
        if c >= len(chunks):
            return
        H = diag2(carry["H"])
        carry["ys"].append(jnp.dot(rhat_st[prv, cs[c], :], H, preferred_element_type=F32) + y0_st[prv, cs[c], :])
        carry["H"] = jnp.dot(m_st[prv, c], H, preferred_element_type=F32) + n_st[prv, c]
        carry["next"] = c + 1

    row0 = lax.broadcasted_iota(jnp.int32, (8, LANES), 0) == 0
    first = t_in == 0

    def shifted(ref, slot, mu_ref):
        x = ref[...].astype(F32)
        carried = jnp.where(first, 0.0, last_ref[slot, 7:8, :])
        rolled = pltpu.roll(x, 1, axis=0)
        prev = jnp.concatenate([jnp.where(row0, carried, rolled[:8]), rolled[8:]], axis=0)
        last_ref[slot] = x[TT - 8:, :]
        return x + (prev - x) * mu_ref[...]

    r = shifted(r_ref, 0, mu_r_ref)
    k = shifted(k_ref, 1, mu_k_ref)
    v = shifted(v_ref, 2, mu_v_ref)
    wa = shifted(wa_ref, 3, mu_wa_ref)
    wd = wa[:, :LORA]
    ad = wa[:, LORA:]

    u = w0_ref[...] + _dot(jnp.tanh(wd), wup_ref[...])
    lw = -(DECAY_SCALE * LOG2_E) / (1.0 + jnp.exp(-u))
    a = _sigmoid(a0_ref[...] + _dot(ad, aup_ref[...]))
    advance()

    lane_r = lax.broadcasted_iota(jnp.int32, (LANES, LANES), 0)
    lane_c = lax.broadcasted_iota(jnp.int32, (LANES, LANES), 1)
    head_blk = (lane_r // HEAD_DIM) == (lane_c // HEAD_DIM)
    ones_blk = jnp.where(head_blk, 1.0, 0.0).astype(BF16)

    kk = k * kk_ref[...]
    kk = kk * lax.rsqrt(jnp.maximum(_dot(kk * kk, ones_blk), 1e-24))
    k = k * ((1.0 - ka_ref[...]) + a * ka_ref[...])
    av = -kk
    bv = kk * a
    post_st[cur, 0] = _dot(r * k * rk_ref[...], ones_blk) * v
    post_st[cur, 1] = _silu(g_ref[...].astype(F32))

    TB = 4 * L
    tr = lax.broadcasted_iota(jnp.int32, (TB, TB), 0)
    tc = lax.broadcasted_iota(jnp.int32, (TB, TB), 1)
    tri = jnp.where(jnp.logical_and(tr // L == tc // L, tc <= tr), 1.0, 0.0).astype(BF16)
    cum = jnp.concatenate([_dot_exact_lhs(tri, lw[i * TB:(i + 1) * TB]) for i in range(TT // TB)], axis=0)
    advance()

    e_pos = jnp.exp2(cum)
    e_neg = 1.0 / e_pos
    r_t = r * e_pos
    k_t = k * e_neg
    b_t = bv * e_neg
    a_t = av * jnp.exp2(cum - lw)

    row = lax.broadcasted_iota(jnp.int32, (L, LANES), 0)
    col = lax.broadcasted_iota(jnp.int32, (L, LANES), 1) % HEAD_DIM
    strict = col < row
    incl = col <= row
    eye = col == row

    ticks = {"n": 0}

    def each(fn):
        out = []
        for c in chunks:
            out.append(fn(c))
            ticks["n"] += 1
            if ticks["n"] % LINK_PERIOD == 0:
                advance()
        return out

    cum_l = [cum[sl][L - 1:L, :] for sl in cs]
    p_l = [jnp.exp2(cum_l[c]) for c in chunks]
    bk_d = [jnp.concatenate([diag2(b_t[sl]), diag2(k_t[sl])], axis=0) for sl in cs]
    v_d = [diag2(v[sl]) for sl in cs]
    gram = each(lambda c: _dot_nt(jnp.concatenate([a_t[cs[c]], r_t[cs[c]]], axis=0), bk_d[c]))
    a_ab = [jnp.where(strict, gram[c][:L, :LANES], 0.0) for c in chunks]
    a_ak = [jnp.where(strict, gram[c][:L, LANES:], 0.0).astype(BF16) for c in chunks]
    a_rb = [jnp.where(incl, gram[c][L:, :LANES], 0.0).astype(BF16) for c in chunks]
    a_rk = [jnp.where(incl, gram[c][L:, LANES:], 0.0).astype(BF16) for c in chunks]
    inv = [jnp.where(eye, 1.0, 0.0) + a_ab[c] for c in chunks]
    pw = each(lambda c: _dot(a_ab[c], diag2(a_ab[c])).astype(BF16))
    w_s = each(lambda c: _dot(a_ak[c], v_d[c]).astype(BF16))
    for _ in range(4):
        both = each(lambda c: _dot(pw[c], jnp.concatenate([diag2(pw[c]), diag2(inv[c])], axis=1)))
        pw = [both[c][:, :LANES].astype(BF16) for c in chunks]
        inv = [inv[c] + both[c][:, LANES:] for c in chunks]
    last = each(lambda c: _dot(pw[c], diag2(inv[c])))
    inv = [inv[c] + last[c] for c in chunks]
    sol = each(lambda c: _dot(inv[c], jnp.concatenate([diag2(a_t[cs[c]]), diag2(w_s[c])], axis=1)))
    ahat_d = [diag2(sol[c][:, :LANES]) for c in chunks]
    u0_d = [diag2(sol[c][:, LANES:]) for c in chunks]

    def heads_down(x):
        return jnp.concatenate([x[:, :HEAD_DIM], x[:, HEAD_DIM:]], axis=0)

    bmn = each(lambda c: _dot_tn(heads_down(b_t[cs[c]] * p_l[c]),
                                 jnp.concatenate([ahat_d[c], u0_d[c]], axis=1)))
    kv = each(lambda c: _dot_tn(heads_down(k_t[cs[c]] * p_l[c]), v_d[c]))
    rb = each(lambda c: _dot(a_rb[c], ahat_d[c]))
    y0 = each(lambda c: _dot(jnp.concatenate([a_rb[c], a_rk[c]], axis=1),
                             jnp.concatenate([u0_d[c], v_d[c]], axis=0)))
    while carry["next"] < len(chunks):
        advance()

    state_ref[...] = carry["H"]
    y = jnp.concatenate(carry["ys"], axis=0)
    mean_blk = jnp.where(head_blk, 1.0 / HEAD_DIM, 0.0).astype(BF16)
    mean = _dot(y, mean_blk)
    yc = y - mean
    var = _dot(yc * yc, mean_blk)
    yn = yc * lax.rsqrt(var + GN_EPS) * lnw_ref[...] + lnb_ref[...]
    o_ref[...] = ((yn + post_st[prv, 0]) * post_st[prv, 1]).astype(o_ref.dtype)

    for c in chunks:
        m_st[cur, c] = (bmn[c][:, :LANES] + jnp.where(eye, p_l[c], 0.0)).astype(BF16)
        n_st[cur, c] = bmn[c][:, LANES:] + kv[c]
        rhat_st[cur, cs[c], :] = (r_t[cs[c]] + rb[c]).astype(BF16)
        y0_st[cur, cs[c], :] = y0[c]


def _rwkv(proj, mu, w0, w_up, a0, a_up, k_k, k_a, r_k, ln_w, ln_b, batch, seq):
    M = proj.shape[0]
    TT = RWKV_TT
    nt = seq // TT
    n_chunks = TT // CHUNK
    total = batch * PAIRS * nt

    def where(s):
        b, p, t = s // (PAIRS * nt), (s // nt) % PAIRS, s % nt
        return b * nt + t, p

    src = lambda s: where(jnp.minimum(s, total - 1))
    dst = lambda s: where(jnp.maximum(s - 1, 0))

    slab = lambda col: pl.BlockSpec((TT, LANES), lambda s: (src(s)[0], col // LANES + src(s)[1]))
    vec = pl.BlockSpec((1, LANES), lambda s: (0, src(s)[1]))
    vec0 = pl.BlockSpec((1, LANES), lambda s: (0, 0))
    vec_dst = pl.BlockSpec((1, LANES), lambda s: (0, dst(s)[1]))
    lora = pl.BlockSpec((LORA, LANES), lambda s: (0, src(s)[1]))
    mu_r, mu_k, mu_v, mu_wa = (mu[:, :1024], mu[:, 1024:2048], mu[:, 2048:3072], mu[:, 3072:])
    return pl.pallas_call(
        functools.partial(_rwkv_kernel, nt=nt, total=total),
        out_shape=jax.ShapeDtypeStruct((M, RWKV_WIDTH), BF16),
        grid=(total + 1,),
        in_specs=[slab(COL_R), slab(COL_RK), slab(COL_RV), slab(COL_RG),
                  pl.BlockSpec((TT, LANES), lambda s: (src(s)[0], COL_WA // LANES)),
                  vec, vec, vec, vec0,
                  vec, lora, vec, lora, vec, vec, vec, vec_dst, vec_dst],
        out_specs=pl.BlockSpec((TT, LANES), lambda s: dst(s)),
        scratch_shapes=[pltpu.VMEM((CHUNK, LANES), F32),
                        pltpu.VMEM((4, 8, LANES), F32),
                        pltpu.VMEM((2, n_chunks, CHUNK, LANES), BF16),
                        pltpu.VMEM((2, n_chunks, CHUNK, LANES), F32),
                        pltpu.VMEM((2, TT, LANES), BF16),
                        pltpu.VMEM((2, TT, LANES), F32),
                        pltpu.VMEM((2, 2, TT, LANES), F32)],
        compiler_params=pltpu.CompilerParams(dimension_semantics=("arbitrary",),
                                             vmem_limit_bytes=VMEM_LIMIT),
        name="rwkv7_mix",
    )(proj, proj, proj, proj, proj, mu_r, mu_k, mu_v, mu_wa,
      w0, w_up, a0, a_up, k_k, k_a, r_k, ln_w, ln_b)


def _out_proj_kernel(x_ref, ya_ref, yr_ref, wa_ref, wr_ref, g_ref, gate_ref, o_ref):
    mix = (jnp.dot(ya_ref[...], wa_ref[...], preferred_element_type=F32)
           + jnp.dot(yr_ref[...], wr_ref[...], preferred_element_type=F32))
    inv = lax.rsqrt(jnp.mean(mix * mix, axis=-1, keepdims=True) + RMS_EPS)
    o_ref[...] = x_ref[...] + gate_ref[0] * ((mix * inv) * g_ref[...])


def _out_proj(x2, ya, yr, w, g, gate, seq):
    M, D = x2.shape
    tm = 512
    per_b = seq // tm
    half = ya.shape[1]
    return pl.pallas_call(
        _out_proj_kernel,
        out_shape=jax.ShapeDtypeStruct((M, D), F32),
        grid=(M // tm,),
        in_specs=[pl.BlockSpec((tm, D), lambda i: (i, 0)),
                  pl.BlockSpec((tm, half), lambda i: (i, 0)),
                  pl.BlockSpec((tm, half), lambda i: (i, 0)),
                  pl.BlockSpec((half, D), lambda i: (0, 0)),
                  pl.BlockSpec((half, D), lambda i: (1, 0)),
                  pl.BlockSpec((1, D), lambda i: (0, 0)),
                  pl.BlockSpec((1, 1, D), lambda i: (i // per_b, 0, 0))],
        out_specs=pl.BlockSpec((tm, D), lambda i: (i, 0)),
        compiler_params=pltpu.CompilerParams(dimension_semantics=("parallel",),
                                             vmem_limit_bytes=VMEM_LIMIT),
        name="out_proj",
    )(x2, ya, yr, w, w, g, gate)


def kernel(x, c, w_ada, b_ada, pre_norm_g, post_norm_g, w_in, w_out, attn_sinks, rwkv_mu, rwkv_w0,
           rwkv_w_up, rwkv_a0, rwkv_a_up, rwkv_k_k, rwkv_k_a, rwkv_r_k, rwkv_ln_w, rwkv_ln_b):
    B, T, D = x.shape
    depth = w_ada.shape[0]
    x2 = x.reshape(B * T, D)
    for l in range(depth):
        mod = _adaln(c, w_ada[l], b_ada[l][None, :])
        shift, scale, gate = (mod[:, i * D:(i + 1) * D].reshape(B, 1, D) for i in range(3))
        proj = _in_proj(x2, pre_norm_g[l][None, :], scale, shift, w_in[l].astype(BF16), T)
        y_attn = _attention(proj, attn_sinks[l], B, T)
        y_rwkv = _rwkv(proj, rwkv_mu[l][None, :], rwkv_w0[l][None, :], rwkv_w_up[l].astype(BF16),
                       rwkv_a0[l][None, :], rwkv_a_up[l].astype(BF16), rwkv_k_k[l][None, :],
                       rwkv_k_a[l][None, :], rwkv_r_k[l].reshape(1, RWKV_WIDTH),
                       rwkv_ln_w[l][None, :], rwkv_ln_b[l][None, :], B, T)
        x2 = _out_proj(x2, y_attn, y_rwkv, w_out[l].astype(BF16), post_norm_g[l][None, :], gate, T)
    return x2.reshape(B, T, D)
```

```python
import functools
import math

import jax
import jax.numpy as jnp
from jax import lax
from jax.experimental import pallas as pl
from jax.experimental.pallas import tpu as pltpu

D_MODEL = 2048
HEAD_DIM = 64
ATTN_WIDTH = 1024
ATTN_Q_HEADS = 16
KV_WIDTH = 256
WINDOW = 128
RWKV_WIDTH = 1024
LORA = 64
IN_COLS = 6784
RMS_EPS = 1e-6
GN_EPS = 64e-5
NEG_BIG = -1e30

LANES = 128
PAIRS = RWKV_WIDTH // LANES
CHUNK = 64

COL_Q = 0
COL_K = 1024
COL_V = 1280
COL_GA = 1536
COL_R = 2560
COL_RK = 3584
COL_RV = 4608
COL_WA = 5632
COL_RG = 5760
GA_BLOCK = 512

VMEM_LIMIT = 56 * 1024 * 1024

F32 = jnp.float32
BF16 = jnp.bfloat16


def _dot(a, b):
    return jnp.dot(a.astype(BF16), b.astype(BF16), preferred_element_type=F32)


def _dot_nt(a, b):
    return lax.dot_general(a.astype(BF16), b.astype(BF16), (((1,), (1,)), ((), ())),
                           preferred_element_type=F32)


def _dot_tn(a, b):
    return lax.dot_general(a.astype(BF16), b.astype(BF16), (((0,), (0,)), ((), ())),
                           preferred_element_type=F32)


def _split2(x):
    hi = x.astype(BF16)
    return hi, (x - hi.astype(F32)).astype(BF16)


def _dot_exact_lhs(a_bf16, b):
    hi, lo = _split2(b)
    return jnp.dot(a_bf16, hi, preferred_element_type=F32) + jnp.dot(a_bf16, lo, preferred_element_type=F32)


def _sigmoid(x):
    return 1.0 / (1.0 + jnp.exp(-x))


def _silu(x):
    return x * _sigmoid(x)


def _adaln_kernel(c_ref, w_ref, b_ref, o_ref):
    s = _silu(c_ref[...])
    s_hi, s_lo = _split2(s)
    w_hi, w_lo = _split2(w_ref[...])
    d = lambda p, q: jnp.dot(p, q, preferred_element_type=F32)
    o_ref[...] = d(s_hi, w_hi) + (d(s_hi, w_lo) + d(s_lo, w_hi)) + b_ref[...]


def _adaln(c, w, b):
    B, D = c.shape
    N = w.shape[1]
    tn = 1024
    return pl.pallas_call(
        _adaln_kernel,
        out_shape=jax.ShapeDtypeStruct((B, N), F32),
        grid=(N // tn,),
        in_specs=[pl.BlockSpec((B, D), lambda j: (0, 0)),
                  pl.BlockSpec((D, tn), lambda j: (0, j)),
                  pl.BlockSpec((1, tn), lambda j: (0, j))],
        out_specs=pl.BlockSpec((B, tn), lambda j: (0, j)),
        compiler_params=pltpu.CompilerParams(dimension_semantics=("arbitrary",),
                                             vmem_limit_bytes=VMEM_LIMIT),
        name="adaln_mod",
    )(c, w, b)


NORM_ROWS = 16


def _in_proj_kernel(x_ref, g_ref, scale_ref, shift_ref, w_ref, o_ref, h_ref):
    @pl.when(pl.program_id(1) == 0)
    def _():
        gain = g_ref[...] * (1.0 + scale_ref[0])
        shift = shift_ref[0]

        def rows(c, carry):
            sl = pl.ds(pl.multiple_of(c * NORM_ROWS, NORM_ROWS), NORM_ROWS)
            x = x_ref[sl, :]
            inv = lax.rsqrt(jnp.mean(x * x, axis=-1, keepdims=True) + RMS_EPS)
            h_ref[sl, :] = ((x * inv) * gain + shift).astype(BF16)
            return carry

        lax.fori_loop(0, x_ref.shape[0] // NORM_ROWS, rows, 0, unroll=8)

    o_ref[...] = jnp.dot(h_ref[...], w_ref[...], preferred_element_type=F32).astype(o_ref.dtype)


def _in_proj(x2, g, scale, shift, w_packed, seq):
    M, D = x2.shape
    NP = w_packed.shape[1]
    tm, tn = 512, 2304
    per_b = seq // tm
    return pl.pallas_call(
        _in_proj_kernel,
        out_shape=jax.ShapeDtypeStruct((M, NP), BF16),
        grid=(M // tm, pl.cdiv(NP, tn)),
        in_specs=[pl.BlockSpec((tm, D), lambda i, j: (i, 0)),
                  pl.BlockSpec((1, D), lambda i, j: (0, 0)),
                  pl.BlockSpec((1, 1, D), lambda i, j: (i // per_b, 0, 0)),
                  pl.BlockSpec((1, 1, D), lambda i, j: (i // per_b, 0, 0)),
                  pl.BlockSpec((D, tn), lambda i, j: (0, j))],
        out_specs=pl.BlockSpec((tm, tn), lambda i, j: (i, j)),
        scratch_shapes=[pltpu.VMEM((tm, D), BF16)],
        compiler_params=pltpu.CompilerParams(dimension_semantics=("parallel", "arbitrary"),
                                             vmem_limit_bytes=VMEM_LIMIT),
        name="in_proj",
    )(x2, g, scale, shift, w_packed)


LOG2_E = math.log2(math.e)
ATTN_GROUP = 8
ATTN_ROWS = 8 * WINDOW


def _dup_halves(slab, lane_lo):
    swapped = pltpu.roll(slab, HEAD_DIM, axis=1)
    return jnp.where(lane_lo, slab, swapped), jnp.where(lane_lo, swapped, slab)


def _attn_kernel(sink_ref, q_ref, ga_lo_ref, ga_hi_ref, k_ref, v_ref, o_ref, kprev_ref, vprev_ref):
    i = pl.program_id(1)
    blk = WINDOW
    lane_lo = lax.broadcasted_iota(jnp.int32, (blk, LANES), 1) < HEAD_DIM

    @pl.when(i == 0)
    def _():
        kprev_ref[...] = jnp.zeros_like(kprev_ref)
        vprev_ref[...] = jnp.zeros_like(vprev_ref)

    ones = jnp.ones((blk, LANES), BF16)
    qi = lax.broadcasted_iota(jnp.int32, (2 * blk, blk), 0) % blk
    si = lax.broadcasted_iota(jnp.int32, (2 * blk, blk), 1)
    cur_ok = si <= qi
    scale = HEAD_DIM ** -0.5 * LOG2_E
    group = ATTN_Q_HEADS // (KV_WIDTH // HEAD_DIM)
    n_kv = KV_WIDTH // HEAD_DIM
    row_lo = lax.broadcasted_iota(jnp.int32, (2 * blk, 1), 0) < blk

    kprev = [kprev_ref[g] for g in range(n_kv)]
    vprev = [vprev_ref[g] for g in range(n_kv)]
    for w in range(ATTN_ROWS // blk):
        rows = slice(w * blk, (w + 1) * blk)
        prev_ok = jnp.logical_and(si > qi, i > 0) if w == 0 else si > qi

        kcur, vcur = [], []
        for s in range(KV_WIDTH // LANES):
            ks = k_ref[rows, s * LANES:(s + 1) * LANES].astype(F32)
            vs = v_ref[rows, s * LANES:(s + 1) * LANES].astype(F32)
            kcur.extend(t.astype(BF16) for t in _dup_halves(ks, lane_lo))
            vcur.extend(t.astype(BF16) for t in _dup_halves(vs, lane_lo))

        def stacked_q(p):
            q = q_ref[rows, p * LANES:(p + 1) * LANES].astype(F32) * scale
            zero = jnp.zeros_like(q)
            return jnp.concatenate([jnp.where(lane_lo, q, zero), jnp.where(lane_lo, zero, q)], axis=0).astype(BF16)

        for p0 in range(0, PAIRS, ATTN_GROUP):
            ps = range(p0, p0 + ATTN_GROUP)
            kvh = {p: (2 * p) // group for p in ps}
            qs = {p: stacked_q(p) for p in ps}
            s = {p: jnp.where(cur_ok, _dot_nt(qs[p], kcur[kvh[p]]),
                              jnp.where(prev_ok, _dot_nt(qs[p], kprev[kvh[p]]), NEG_BIG)) for p in ps}
            sink = {p: jnp.where(row_lo, sink_ref[2 * p] * LOG2_E, sink_ref[2 * p + 1] * LOG2_E) for p in ps}
            m = {p: jnp.max(s[p], axis=-1, keepdims=True) for p in ps}
            e = {p: jnp.exp2(s[p] - m[p]) for p in ps}
            e2 = {p: jnp.concatenate([jnp.where(cur_ok, e[p], 0.0), jnp.where(cur_ok, 0.0, e[p])],
                                     axis=1).astype(BF16) for p in ps}
            acc = {p: jnp.dot(e2[p], jnp.concatenate([jnp.concatenate([vcur[kvh[p]], ones], axis=1),
                                                      jnp.concatenate([vprev[kvh[p]], ones], axis=1)], axis=0),
                              preferred_element_type=F32) for p in ps}
            for p in ps:
                num = acc[p][:, :LANES]
                den = acc[p][:, LANES:] + jnp.exp2(sink[p] - m[p])
                o = jnp.where(lane_lo, num[:blk], num[blk:]) / jnp.where(lane_lo, den[:blk], den[blk:])
                ga_ref, gp = (ga_lo_ref, p) if p < PAIRS // 2 else (ga_hi_ref, p - PAIRS // 2)
                ga = ga_ref[rows, gp * LANES:(gp + 1) * LANES].astype(F32)
                o_ref[rows, p * LANES:(p + 1) * LANES] = (o * _silu(ga)).astype(o_ref.dtype)
        kprev, vprev = kcur, vcur

    for g in range(n_kv):
        kprev_ref[g] = kprev[g]
        vprev_ref[g] = vprev[g]


def _attention(proj, sinks, batch, seq):
    M = proj.shape[0]
    nb = seq // ATTN_ROWS
    row = lambda b, i: b * nb + i
    n_kv = KV_WIDTH // HEAD_DIM
    return pl.pallas_call(
        _attn_kernel,
        out_shape=jax.ShapeDtypeStruct((M, ATTN_WIDTH), BF16),
        grid=(batch, nb),
        in_specs=[pl.BlockSpec(memory_space=pltpu.SMEM),
                  pl.BlockSpec((ATTN_ROWS, ATTN_WIDTH), lambda b, i: (row(b, i), COL_Q // ATTN_WIDTH)),
                  pl.BlockSpec((ATTN_ROWS, GA_BLOCK), lambda b, i: (row(b, i), COL_GA // GA_BLOCK)),
                  pl.BlockSpec((ATTN_ROWS, GA_BLOCK), lambda b, i: (row(b, i), COL_GA // GA_BLOCK + 1)),
                  pl.BlockSpec((ATTN_ROWS, KV_WIDTH), lambda b, i: (row(b, i), COL_K // KV_WIDTH)),
                  pl.BlockSpec((ATTN_ROWS, KV_WIDTH), lambda b, i: (row(b, i), COL_V // KV_WIDTH))],
        out_specs=pl.BlockSpec((ATTN_ROWS, ATTN_WIDTH), lambda b, i: (row(b, i), 0)),
        scratch_shapes=[pltpu.VMEM((n_kv, WINDOW, LANES), BF16),
                        pltpu.VMEM((n_kv, WINDOW, LANES), BF16)],
        compiler_params=pltpu.CompilerParams(dimension_semantics=("parallel", "arbitrary"),
                                             vmem_limit_bytes=VMEM_LIMIT),
        name="swa_attn",
    )(sinks, proj, proj, proj, proj, proj)


DECAY_SCALE = math.exp(-0.5)
RWKV_TT = 1024
LINK_PERIOD = 15


def _rwkv_kernel(r_ref, k_ref, v_ref, g_ref, wa_ref,
                 mu_r_ref, mu_k_ref, mu_v_ref, mu_wa_ref,
                 w0_ref, wup_ref, a0_ref, aup_ref, kk_ref, ka_ref, rk_ref, lnw_ref, lnb_ref,
                 o_ref, state_ref, last_ref, m_st, n_st, rhat_st, y0_st, post_st, *, nt, total):
    s = pl.program_id(0)
    TT = RWKV_TT
    L = CHUNK
    t_in = jnp.minimum(s, total - 1) % nt
    t_out = jnp.maximum(s - 1, 0) % nt
    cur = s % 2
    prv = 1 - cur

    @pl.when(s == 0)
    def _():
        for ref in (state_ref, last_ref, m_st, n_st, rhat_st, y0_st, post_st):
            ref[...] = jnp.zeros_like(ref)

    chunks = range(TT // L)
    cs = [slice(c * L, (c + 1) * L) for c in chunks]

    lane_lo = lax.broadcasted_iota(jnp.int32, (L, LANES), 1) < HEAD_DIM

    def diag2(x):
        zero = jnp.zeros_like(x)
        return jnp.concatenate([jnp.where(lane_lo, x, zero), jnp.where(lane_lo, zero, x)], axis=0).astype(BF16)

    carry = {"H": jnp.where(t_out == 0, 0.0, state_ref[...]), "ys": [], "next": 0}

    def advance():
        c = carry["next"]
        if c >= len(chunks):
            return
        H = diag2(carry["H"])
        carry["ys"].append(jnp.dot(rhat_st[prv, cs[c], :], H, preferred_element_type=F32) + y0_st[prv, cs[c], :])
        carry["H"] = jnp.dot(m_st[prv, c], H, preferred_element_type=F32) + n_st[prv, c]
        carry["next"] = c + 1

    row0 = lax.broadcasted_iota(jnp.int32, (8, LANES), 0) == 0
    first = t_in == 0

    def shifted(ref, slot, mu_ref):
        x = ref[...].astype(F32)
        carried = jnp.where(first, 0.0, last_ref[slot, 7:8, :])
        rolled = pltpu.roll(x, 1, axis=0)
        prev = jnp.concatenate([jnp.where(row0, carried, rolled[:8]), rolled[8:]], axis=0)
        last_ref[slot] = x[TT - 8:, :]
        return x + (prev - x) * mu_ref[...]

    r = shifted(r_ref, 0, mu_r_ref)
    k = shifted(k_ref, 1, mu_k_ref)
    v = shifted(v_ref, 2, mu_v_ref)
    wa = shifted(wa_ref, 3, mu_wa_ref)
    wd = wa[:, :LORA]
    ad = wa[:, LORA:]

    u = w0_ref[...] + _dot(jnp.tanh(wd), wup_ref[...])
    lw = -(DECAY_SCALE * LOG2_E) / (1.0 + jnp.exp(-u))
    a = _sigmoid(a0_ref[...] + _dot(ad, aup_ref[...]))
    advance()

    lane_r = lax.broadcasted_iota(jnp.int32, (LANES, LANES), 0)
    lane_c = lax.broadcasted_iota(jnp.int32, (LANES, LANES), 1)
    head_blk = (lane_r // HEAD_DIM) == (lane_c // HEAD_DIM)
    ones_blk = jnp.where(head_blk, 1.0, 0.0).astype(BF16)

    kk = k * kk_ref[...]
    kk = kk * lax.rsqrt(jnp.maximum(_dot(kk * kk, ones_blk), 1e-24))
    k = k * ((1.0 - ka_ref[...]) + a * ka_ref[...])
    av = -kk
    bv = kk * a
    post_st[cur, 0] = _dot(r * k * rk_ref[...], ones_blk) * v
    post_st[cur, 1] = _silu(g_ref[...].astype(F32))

    TB = 4 * L
    tr = lax.broadcasted_iota(jnp.int32, (TB, TB), 0)
    tc = lax.broadcasted_iota(jnp.int32, (TB, TB), 1)
    tri = jnp.where(jnp.logical_and(tr // L == tc // L, tc <= tr), 1.0, 0.0).astype(BF16)
    cum = jnp.concatenate([_dot_exact_lhs(tri, lw[i * TB:(i + 1) * TB]) for i in range(TT // TB)], axis=0)
    advance()

    e_pos = jnp.exp2(cum)
    e_neg = 1.0 / e_pos
    r_t = r * e_pos
    k_t = k * e_neg
    b_t = bv * e_neg
    a_t = av * jnp.exp2(cum - lw)

    row = lax.broadcasted_iota(jnp.int32, (L, LANES), 0)
    col = lax.broadcasted_iota(jnp.int32, (L, LANES), 1) % HEAD_DIM
    strict = col < row
    incl = col <= row
    eye = col == row

    ticks = {"n": 0}

    def each(fn):
        out = []
        for c in chunks:
            out.append(fn(c))
            ticks["n"] += 1
            if ticks["n"] % LINK_PERIOD == 0:
                advance()
        return out

    cum_l = [cum[sl][L - 1:L, :] for sl in cs]
    p_l = [jnp.exp2(cum_l[c]) for c in chunks]
    bk_d = [jnp.concatenate([diag2(b_t[sl]), diag2(k_t[sl])], axis=0) for sl in cs]
    v_d = [diag2(v[sl]) for sl in cs]
    gram = each(lambda c: _dot_nt(jnp.concatenate([a_t[cs[c]], r_t[cs[c]]], axis=0), bk_d[c]))
    a_ab = [jnp.where(strict, gram[c][:L, :LANES], 0.0) for c in chunks]
    a_ak = [jnp.where(strict, gram[c][:L, LANES:], 0.0).astype(BF16) for c in chunks]
    a_rb = [jnp.where(incl, gram[c][L:, :LANES], 0.0).astype(BF16) for c in chunks]
    a_rk = [jnp.where(incl, gram[c][L:, LANES:], 0.0).astype(BF16) for c in chunks]
    inv = [jnp.where(eye, 1.0, 0.0) + a_ab[c] for c in chunks]
    pw = each(lambda c: _dot(a_ab[c], diag2(a_ab[c])).astype(BF16))
    w_s = each(lambda c: _dot(a_ak[c], v_d[c]).astype(BF16))
    for _ in range(4):
        both = each(lambda c: _dot(pw[c], jnp.concatenate([diag2(pw[c]), diag2(inv[c])], axis=1)))
        pw = [both[c][:, :LANES].astype(BF16) for c in chunks]
        inv = [inv[c] + both[c][:, LANES:] for c in chunks]
    last = each(lambda c: _dot(pw[c], diag2(inv[c])))
    inv = [inv[c] + last[c] for c in chunks]
    sol = each(lambda c: _dot(inv[c], jnp.concatenate([diag2(a_t[cs[c]]), diag2(w_s[c])], axis=1)))
    ahat_d = [diag2(sol[c][:, :LANES]) for c in chunks]
    u0_d = [diag2(sol[c][:, LANES:]) for c in chunks]

    def heads_down(x):
        return jnp.concatenate([x[:, :HEAD_DIM], x[:, HEAD_DIM:]], axis=0)

    bmn = each(lambda c: _dot_tn(heads_down(b_t[cs[c]] * p_l[c]),
                                 jnp.concatenate([ahat_d[c], u0_d[c]], axis=1)))
    kv = each(lambda c: _dot_tn(heads_down(k_t[cs[c]] * p_l[c]), v_d[c]))
    rb = each(lambda c: _dot(a_rb[c], ahat_d[c]))
    y0 = each(lambda c: _dot(jnp.concatenate([a_rb[c], a_rk[c]], axis=1),
                             jnp.concatenate([u0_d[c], v_d[c]], axis=0)))
    while carry["next"] < len(chunks):
        advance()

    state_ref[...] = carry["H"]
    y = jnp.concatenate(carry["ys"], axis=0)
    mean_blk = jnp.where(head_blk, 1.0 / HEAD_DIM, 0.0).astype(BF16)
    mean = _dot(y, mean_blk)
    yc = y - mean
    var = _dot(yc * yc, mean_blk)
    yn = yc * lax.rsqrt(var + GN_EPS) * lnw_ref[...] + lnb_ref[...]
    o_ref[...] = ((yn + post_st[prv, 0]) * post_st[prv, 1]).astype(o_ref.dtype)

    for c in chunks:
        m_st[cur, c] = (bmn[c][:, :LANES] + jnp.where(eye, p_l[c], 0.0)).astype(BF16)
        n_st[cur, c] = bmn[c][:, LANES:] + kv[c]
        rhat_st[cur, cs[c], :] = (r_t[cs[c]] + rb[c]).astype(BF16)
        y0_st[cur, cs[c], :] = y0[c]


def _rwkv(proj, mu, w0, w_up, a0, a_up, k_k, k_a, r_k, ln_w, ln_b, batch, seq):
    M = proj.shape[0]
    TT = RWKV_TT
    nt = seq // TT
    n_chunks = TT // CHUNK
    total = batch * PAIRS * nt

    def where(s):
        b, p, t = s // (PAIRS * nt), (s // nt) % PAIRS, s % nt
        return b * nt + t, p

    src = lambda s: where(jnp.minimum(s, total - 1))
    dst = lambda s: where(jnp.maximum(s - 1, 0))

    slab = lambda col: pl.BlockSpec((TT, LANES), lambda s: (src(s)[0], col // LANES + src(s)[1]))
    vec = pl.BlockSpec((1, LANES), lambda s: (0, src(s)[1]))
    vec0 = pl.BlockSpec((1, LANES), lambda s: (0, 0))
    vec_dst = pl.BlockSpec((1, LANES), lambda s: (0, dst(s)[1]))
    lora = pl.BlockSpec((LORA, LANES), lambda s: (0, src(s)[1]))
    mu_r, mu_k, mu_v, mu_wa = (mu[:, :1024], mu[:, 1024:2048], mu[:, 2048:3072], mu[:, 3072:])
    return pl.pallas_call(
        functools.partial(_rwkv_kernel, nt=nt, total=total),
        out_shape=jax.ShapeDtypeStruct((M, RWKV_WIDTH), BF16),
        grid=(total + 1,),
        in_specs=[slab(COL_R), slab(COL_RK), slab(COL_RV), slab(COL_RG),
                  pl.BlockSpec((TT, LANES), lambda s: (src(s)[0], COL_WA // LANES)),
                  vec, vec, vec, vec0,
                  vec, lora, vec, lora, vec, vec, vec, vec_dst, vec_dst],
        out_specs=pl.BlockSpec((TT, LANES), lambda s: dst(s)),
        scratch_shapes=[pltpu.VMEM((CHUNK, LANES), F32),
                        pltpu.VMEM((4, 8, LANES), F32),
                        pltpu.VMEM((2, n_chunks, CHUNK, LANES), BF16),
                        pltpu.VMEM((2, n_chunks, CHUNK, LANES), F32),
                        pltpu.VMEM((2, TT, LANES), BF16),
                        pltpu.VMEM((2, TT, LANES), F32),
                        pltpu.VMEM((2, 2, TT, LANES), F32)],
        compiler_params=pltpu.CompilerParams(dimension_semantics=("arbitrary",),
                                             vmem_limit_bytes=VMEM_LIMIT),
        name="rwkv7_mix",
    )(proj, proj, proj, proj, proj, mu_r, mu_k, mu_v, mu_wa,
      w0, w_up, a0, a_up, k_k, k_a, r_k, ln_w, ln_b)


def _out_proj_kernel(x_ref, ya_ref, yr_ref, wa_ref, wr_ref, g_ref, gate_ref, o_ref):
    mix = (jnp.dot(ya_ref[...], wa_ref[...], preferred_element_type=F32)
           + jnp.dot(yr_ref[...], wr_ref[...], preferred_element_type=F32))
    inv = lax.rsqrt(jnp.mean(mix * mix, axis=-1, keepdims=True) + RMS_EPS)
    o_ref[...] = x_ref[...] + (mix * inv) * (gate_ref[0] * g_ref[...])


def _out_proj(x2, ya, yr, w, g, gate, seq):
    M, D = x2.shape
    tm = 512
    per_b = seq // tm
    half = ya.shape[1]
    return pl.pallas_call(
        _out_proj_kernel,
        out_shape=jax.ShapeDtypeStruct((M, D), F32),
        grid=(M // tm,),
        in_specs=[pl.BlockSpec((tm, D), lambda i: (i, 0)),
                  pl.BlockSpec((tm, half), lambda i: (i, 0)),
                  pl.BlockSpec((tm, half), lambda i: (i, 0)),
                  pl.BlockSpec((half, D), lambda i: (0, 0)),
                  pl.BlockSpec((half, D), lambda i: (1, 0)),
                  pl.BlockSpec((1, D), lambda i: (0, 0)),
                  pl.BlockSpec((1, 1, D), lambda i: (i // per_b, 0, 0))],
        out_specs=pl.BlockSpec((tm, D), lambda i: (i, 0)),
        compiler_params=pltpu.CompilerParams(dimension_semantics=("parallel",),
                                             vmem_limit_bytes=VMEM_LIMIT),
        name="out_proj",
    )(x2, ya, yr, w, w, g, gate)


def kernel(x, c, w_ada, b_ada, pre_norm_g, post_norm_g, w_in, w_out, attn_sinks, rwkv_mu, rwkv_w0,
           rwkv_w_up, rwkv_a0, rwkv_a_up, rwkv_k_k, rwkv_k_a, rwkv_r_k, rwkv_ln_w, rwkv_ln_b):
    B, T, D = x.shape
    depth = w_ada.shape[0]
    x2 = x.reshape(B * T, D)
    for l in range(depth):
        mod = _adaln(c, w_ada[l], b_ada[l][None, :])
        shift, scale, gate = (mod[:, i * D:(i + 1) * D].reshape(B, 1, D) for i in range(3))
        proj = _in_proj(x2, pre_norm_g[l][None, :], scale, shift, w_in[l].astype(BF16), T)
        y_attn = _attention(proj, attn_sinks[l], B, T)
        y_rwkv = _rwkv(proj, rwkv_mu[l][None, :], rwkv_w0[l][None, :], rwkv_w_up[l].astype(BF16),
                       rwkv_a0[l][None, :], rwkv_a_up[l].astype(BF16), rwkv_k_k[l][None, :],
                       rwkv_k_a[l][None, :], rwkv_r_k[l].reshape(1, RWKV_WIDTH),
                       rwkv_ln_w[l][None, :], rwkv_ln_b[l][None, :], B, T)
        x2 = _out_proj(x2, y_attn, y_rwkv, w_out[l].astype(BF16), post_norm_g[l][None, :], gate, T)
    return x2.reshape(B, T, D)
```

```python
import functools
import math

import jax
import jax.numpy as jnp
from jax import lax
from jax.experimental import pallas as pl
from jax.experimental.pallas import tpu as pltpu

D_MODEL = 2048
HEAD_DIM = 64
ATTN_WIDTH = 1024
ATTN_Q_HEADS = 16
KV_WIDTH = 256
WINDOW = 128
RWKV_WIDTH = 1024
LORA = 64
IN_COLS = 6784
RMS_EPS = 1e-6
GN_EPS = 64e-5
NEG_BIG = -1e30

LANES = 128
PAIRS = RWKV_WIDTH // LANES
CHUNK = 64

COL_Q = 0
COL_K = 1024
COL_V = 1280
COL_GA = 1536
COL_R = 2560
COL_RK = 3584
COL_RV = 4608
COL_WA = 5632
COL_RG = 5760
GA_BLOCK = 512

VMEM_LIMIT = 56 * 1024 * 1024

F32 = jnp.float32
BF16 = jnp.bfloat16


def _dot(a, b):
    return jnp.dot(a.astype(BF16), b.astype(BF16), preferred_element_type=F32)


def _dot_nt(a, b):
    return lax.dot_general(a.astype(BF16), b.astype(BF16), (((1,), (1,)), ((), ())),
                           preferred_element_type=F32)


def _dot_tn(a, b):
    return lax.dot_general(a.astype(BF16), b.astype(BF16), (((0,), (0,)), ((), ())),
                           preferred_element_type=F32)


def _split2(x):
    hi = x.astype(BF16)
    return hi, (x - hi.astype(F32)).astype(BF16)


def _dot_exact_lhs(a_bf16, b):
    hi, lo = _split2(b)
    return jnp.dot(a_bf16, hi, preferred_element_type=F32) + jnp.dot(a_bf16, lo, preferred_element_type=F32)


def _sigmoid(x):
    return 1.0 / (1.0 + jnp.exp(-x))


def _silu(x):
    return x * _sigmoid(x)


def _adaln_kernel(c_ref, w_ref, b_ref, o_ref):
    s = _silu(c_ref[...])
    n = s.shape[0]
    acc = jnp.dot(jnp.concatenate(_split2(s), axis=0), w_ref[...].astype(BF16), preferred_element_type=F32)
    o_ref[...] = acc[:n] + acc[n:] + b_ref[...]


def _adaln(c, w, b):
    B, D = c.shape
    N = w.shape[1]
    tn = 1024
    return pl.pallas_call(
        _adaln_kernel,
        out_shape=jax.ShapeDtypeStruct((B, N), F32),
        grid=(N // tn,),
        in_specs=[pl.BlockSpec((B, D), lambda j: (0, 0)),
                  pl.BlockSpec((D, tn), lambda j: (0, j)),
                  pl.BlockSpec((1, tn), lambda j: (0, j))],
        out_specs=pl.BlockSpec((B, tn), lambda j: (0, j)),
        compiler_params=pltpu.CompilerParams(dimension_semantics=("arbitrary",),
                                             vmem_limit_bytes=VMEM_LIMIT),
        name="adaln_mod",
    )(c, w, b)


NORM_ROWS = 16


def _in_proj_kernel(x_ref, g_ref, scale_ref, shift_ref, w_ref, o_ref, h_ref):
    @pl.when(pl.program_id(1) == 0)
    def _():
        gain = g_ref[...] * (1.0 + scale_ref[0])
        shift = shift_ref[0]

        def rows(c, carry):
            sl = pl.ds(pl.multiple_of(c * NORM_ROWS, NORM_ROWS), NORM_ROWS)
            x = x_ref[sl, :]
            inv = lax.rsqrt(jnp.mean(x * x, axis=-1, keepdims=True) + RMS_EPS)
            h_ref[sl, :] = ((x * inv) * gain + shift).astype(BF16)
            return carry

        lax.fori_loop(0, x_ref.shape[0] // NORM_ROWS, rows, 0, unroll=16)

    o_ref[...] = jnp.dot(h_ref[...], w_ref[...], preferred_element_type=F32).astype(o_ref.dtype)


def _in_proj(x2, g, scale, shift, w_packed, seq):
    M, D = x2.shape
    NP = w_packed.shape[1]
    tm, tn = 512, 2304
    per_b = seq // tm
    return pl.pallas_call(
        _in_proj_kernel,
        out_shape=jax.ShapeDtypeStruct((M, NP), BF16),
        grid=(M // tm, pl.cdiv(NP, tn)),
        in_specs=[pl.BlockSpec((tm, D), lambda i, j: (i, 0)),
                  pl.BlockSpec((1, D), lambda i, j: (0, 0)),
                  pl.BlockSpec((1, 1, D), lambda i, j: (i // per_b, 0, 0)),
                  pl.BlockSpec((1, 1, D), lambda i, j: (i // per_b, 0, 0)),
                  pl.BlockSpec((D, tn), lambda i, j: (0, j))],
        out_specs=pl.BlockSpec((tm, tn), lambda i, j: (i, j)),
        scratch_shapes=[pltpu.VMEM((tm, D), BF16)],
        compiler_params=pltpu.CompilerParams(dimension_semantics=("parallel", "arbitrary"),
                                             vmem_limit_bytes=VMEM_LIMIT),
        name="in_proj",
    )(x2, g, scale, shift, w_packed)


LOG2_E = math.log2(math.e)
ATTN_GROUP = 8
ATTN_ROWS = 8 * WINDOW


def _dup_halves(slab, lane_lo):
    swapped = pltpu.roll(slab, HEAD_DIM, axis=1)
    return jnp.where(lane_lo, slab, swapped), jnp.where(lane_lo, swapped, slab)


def _attn_kernel(sink_ref, q_ref, ga_lo_ref, ga_hi_ref, k_ref, v_ref, o_ref, kprev_ref, vprev_ref):
    i = pl.program_id(1)
    blk = WINDOW
    lane_lo = lax.broadcasted_iota(jnp.int32, (blk, LANES), 1) < HEAD_DIM

    @pl.when(i == 0)
    def _():
        kprev_ref[...] = jnp.zeros_like(kprev_ref)
        vprev_ref[...] = jnp.zeros_like(vprev_ref)

    ones = jnp.ones((blk, LANES), BF16)
    qi = lax.broadcasted_iota(jnp.int32, (2 * blk, blk), 0) % blk
    si = lax.broadcasted_iota(jnp.int32, (2 * blk, blk), 1)
    cur_ok = si <= qi
    scale = HEAD_DIM ** -0.5 * LOG2_E
    group = ATTN_Q_HEADS // (KV_WIDTH // HEAD_DIM)
    n_kv = KV_WIDTH // HEAD_DIM
    row_lo = lax.broadcasted_iota(jnp.int32, (2 * blk, 1), 0) < blk

    kprev = [kprev_ref[g] for g in range(n_kv)]
    vprev = [vprev_ref[g] for g in range(n_kv)]
    for w in range(ATTN_ROWS // blk):
        rows = slice(w * blk, (w + 1) * blk)
        prev_ok = jnp.logical_and(si > qi, i > 0) if w == 0 else si > qi

        kcur, vcur = [], []
        for s in range(KV_WIDTH // LANES):
            ks = k_ref[rows, s * LANES:(s + 1) * LANES].astype(F32)
            vs = v_ref[rows, s * LANES:(s + 1) * LANES].astype(F32)
            kcur.extend(t.astype(BF16) for t in _dup_halves(ks, lane_lo))
            vcur.extend(t.astype(BF16) for t in _dup_halves(vs, lane_lo))

        def stacked_q(p):
            q = q_ref[rows, p * LANES:(p + 1) * LANES].astype(F32) * scale
            zero = jnp.zeros_like(q)
            return jnp.concatenate([jnp.where(lane_lo, q, zero), jnp.where(lane_lo, zero, q)], axis=0).astype(BF16)

        for p0 in range(0, PAIRS, ATTN_GROUP):
            ps = range(p0, p0 + ATTN_GROUP)
            kvh = {p: (2 * p) // group for p in ps}
            qs = {p: stacked_q(p) for p in ps}
            s = {p: jnp.where(cur_ok, _dot_nt(qs[p], kcur[kvh[p]]),
                              jnp.where(prev_ok, _dot_nt(qs[p], kprev[kvh[p]]), NEG_BIG)) for p in ps}
            sink = {p: jnp.where(row_lo, sink_ref[2 * p] * LOG2_E, sink_ref[2 * p + 1] * LOG2_E) for p in ps}
            m = {p: jnp.max(s[p], axis=-1, keepdims=True) for p in ps}
            e = {p: jnp.exp2(s[p] - m[p]) for p in ps}
            e2 = {p: jnp.concatenate([jnp.where(cur_ok, e[p], 0.0), jnp.where(cur_ok, 0.0, e[p])],
                                     axis=1).astype(BF16) for p in ps}
            acc = {p: jnp.dot(e2[p], jnp.concatenate([jnp.concatenate([vcur[kvh[p]], ones], axis=1),
                                                      jnp.concatenate([vprev[kvh[p]], ones], axis=1)], axis=0),
                              preferred_element_type=F32) for p in ps}
            for p in ps:
                num = acc[p][:, :LANES]
                den = acc[p][:, LANES:] + jnp.exp2(sink[p] - m[p])
                o = jnp.where(lane_lo, num[:blk], num[blk:]) / jnp.where(lane_lo, den[:blk], den[blk:])
                ga_ref, gp = (ga_lo_ref, p) if p < PAIRS // 2 else (ga_hi_ref, p - PAIRS // 2)
                ga = ga_ref[rows, gp * LANES:(gp + 1) * LANES].astype(F32)
                o_ref[rows, p * LANES:(p + 1) * LANES] = (o * _silu(ga)).astype(o_ref.dtype)
        kprev, vprev = kcur, vcur

    for g in range(n_kv):
        kprev_ref[g] = kprev[g]
        vprev_ref[g] = vprev[g]


def _attention(proj, sinks, batch, seq):
    M = proj.shape[0]
    nb = seq // ATTN_ROWS
    row = lambda b, i: b * nb + i
    n_kv = KV_WIDTH // HEAD_DIM
    return pl.pallas_call(
        _attn_kernel,
        out_shape=jax.ShapeDtypeStruct((M, ATTN_WIDTH), BF16),
        grid=(batch, nb),
        in_specs=[pl.BlockSpec(memory_space=pltpu.SMEM),
                  pl.BlockSpec((ATTN_ROWS, ATTN_WIDTH), lambda b, i: (row(b, i), COL_Q // ATTN_WIDTH)),
                  pl.BlockSpec((ATTN_ROWS, GA_BLOCK), lambda b, i: (row(b, i), COL_GA // GA_BLOCK)),
                  pl.BlockSpec((ATTN_ROWS, GA_BLOCK), lambda b, i: (row(b, i), COL_GA // GA_BLOCK + 1)),
                  pl.BlockSpec((ATTN_ROWS, KV_WIDTH), lambda b, i: (row(b, i), COL_K // KV_WIDTH)),
                  pl.BlockSpec((ATTN_ROWS, KV_WIDTH), lambda b, i: (row(b, i), COL_V // KV_WIDTH))],
        out_specs=pl.BlockSpec((ATTN_ROWS, ATTN_WIDTH), lambda b, i: (row(b, i), 0)),
        scratch_shapes=[pltpu.VMEM((n_kv, WINDOW, LANES), BF16),
                        pltpu.VMEM((n_kv, WINDOW, LANES), BF16)],
        compiler_params=pltpu.CompilerParams(dimension_semantics=("parallel", "arbitrary"),
                                             vmem_limit_bytes=VMEM_LIMIT),
        name="swa_attn",
    )(sinks, proj, proj, proj, proj, proj)


DECAY_SCALE = math.exp(-0.5)
RWKV_TT = 1024
LINK_PERIOD = 15


def _rwkv_kernel(r_ref, k_ref, v_ref, g_ref, wa_ref,
                 mu_r_ref, mu_k_ref, mu_v_ref, mu_wa_ref,
                 w0_ref, wup_ref, a0_ref, aup_ref, kk_ref, ka_ref, rk_ref, lnw_ref, lnb_ref,
                 o_ref, state_ref, last_ref, m_st, n_st, rhat_st, y0_st, post_st, *, nt, total):
    s = pl.program_id(0)
    TT = RWKV_TT
    L = CHUNK
    t_in = jnp.minimum(s, total - 1) % nt
    t_out = jnp.maximum(s - 1, 0) % nt
    cur = s % 2
    prv = 1 - cur

    @pl.when(s == 0)
    def _():
        for ref in (state_ref, last_ref, m_st, n_st, rhat_st, y0_st, post_st):
            ref[...] = jnp.zeros_like(ref)

    chunks = range(TT // L)
    cs = [slice(c * L, (c + 1) * L) for c in chunks]

    lane_lo = lax.broadcasted_iota(jnp.int32, (L, LANES), 1) < HEAD_DIM

    def diag2(x):
        zero = jnp.zeros_like(x)
        return jnp.concatenate([jnp.where(lane_lo, x, zero), jnp.where(lane_lo, zero, x)], axis=0).astype(BF16)

    carry = {"H": jnp.where(t_out == 0, 0.0, state_ref[...]), "ys": [], "next": 0}

    def advance():
        c = carry["next"]
        if c >= len(chunks):
            return
        H = diag2(carry["H"])
        carry["ys"].append(jnp.dot(rhat_st[prv, cs[c], :], H, preferred_element_type=F32) + y0_st[prv, cs[c], :])
        carry["H"] = jnp.dot(m_st[prv, c], H, preferred_element_type=F32) + n_st[prv, c]
        carry["next"] = c + 1

    row0 = lax.broadcasted_iota(jnp.int32, (8, LANES), 0) == 0
    first = t_in == 0

    def shifted(ref, slot, mu_ref):
        x = ref[...].astype(F32)
        carried = jnp.where(first, 0.0, last_ref[slot, 7:8, :])
        rolled = pltpu.roll(x, 1, axis=0)
        prev = jnp.concatenate([jnp.where(row0, carried, rolled[:8]), rolled[8:]], axis=0)
        last_ref[slot] = x[TT - 8:, :]
        return x + (prev - x) * mu_ref[...]

    r = shifted(r_ref, 0, mu_r_ref)
    k = shifted(k_ref, 1, mu_k_ref)
    v = shifted(v_ref, 2, mu_v_ref)
    wa = shifted(wa_ref, 3, mu_wa_ref)
    wd = wa[:, :LORA]
    ad = wa[:, LORA:]

    u = w0_ref[...] + _dot(jnp.tanh(wd), wup_ref[...])
    lw = -(DECAY_SCALE * LOG2_E) / (1.0 + jnp.exp(-u))
    a = _sigmoid(a0_ref[...] + _dot(ad, aup_ref[...]))
    advance()

    lane_r = lax.broadcasted_iota(jnp.int32, (LANES, LANES), 0)
    lane_c = lax.broadcasted_iota(jnp.int32, (LANES, LANES), 1)
    head_blk = (lane_r // HEAD_DIM) == (lane_c // HEAD_DIM)
    ones_blk = jnp.where(head_blk, 1.0, 0.0).astype(BF16)

    kk = k * kk_ref[...]
    kk = kk * lax.rsqrt(jnp.maximum(_dot(kk * kk, ones_blk), 1e-24))
    k = k * ((1.0 - ka_ref[...]) + a * ka_ref[...])
    av = -kk
    bv = kk * a
    post_st[cur, 0] = _dot(r * k * rk_ref[...], ones_blk) * v
    post_st[cur, 1] = _silu(g_ref[...].astype(F32))

    TB = 4 * L
    tr = lax.broadcasted_iota(jnp.int32, (TB, TB), 0)
    tc = lax.broadcasted_iota(jnp.int32, (TB, TB), 1)
    tri = jnp.where(jnp.logical_and(tr // L == tc // L, tc <= tr), 1.0, 0.0).astype(BF16)
    cum = jnp.concatenate([_dot_exact_lhs(tri, lw[i * TB:(i + 1) * TB]) for i in range(TT // TB)], axis=0)
    advance()

    e_pos = jnp.exp2(cum)
    e_neg = 1.0 / e_pos
    r_t = r * e_pos
    k_t = k * e_neg
    b_t = bv * e_neg
    a_t = av * jnp.exp2(cum - lw)

    row = lax.broadcasted_iota(jnp.int32, (L, LANES), 0)
    col = lax.broadcasted_iota(jnp.int32, (L, LANES), 1) % HEAD_DIM
    strict = col < row
    incl = col <= row
    eye = col == row

    ticks = {"n": 0}

    def each(fn):
        out = []
        for c in chunks:
            out.append(fn(c))
            ticks["n"] += 1
            if ticks["n"] % LINK_PERIOD == 0:
                advance()
        return out

    cum_l = [cum[sl][L - 1:L, :] for sl in cs]
    p_l = [jnp.exp2(cum_l[c]) for c in chunks]
    bk_d = [jnp.concatenate([diag2(b_t[sl]), diag2(k_t[sl])], axis=0) for sl in cs]
    v_d = [diag2(v[sl]) for sl in cs]
    gram = each(lambda c: _dot_nt(jnp.concatenate([a_t[cs[c]], r_t[cs[c]]], axis=0), bk_d[c]))
    a_ab = [jnp.where(strict, gram[c][:L, :LANES], 0.0) for c in chunks]
    a_ak = [jnp.where(strict, gram[c][:L, LANES:], 0.0).astype(BF16) for c in chunks]
    a_rb = [jnp.where(incl, gram[c][L:, :LANES], 0.0).astype(BF16) for c in chunks]
    a_rk = [jnp.where(incl, gram[c][L:, LANES:], 0.0).astype(BF16) for c in chunks]
    inv = [jnp.where(eye, 1.0, 0.0) + a_ab[c] for c in chunks]
    pw = each(lambda c: _dot(a_ab[c], diag2(a_ab[c])).astype(BF16))
    w_s = each(lambda c: _dot(a_ak[c], v_d[c]).astype(BF16))
    for _ in range(4):
        both = each(lambda c: _dot(pw[c], jnp.concatenate([diag2(pw[c]), diag2(inv[c])], axis=1)))
        pw = [both[c][:, :LANES].astype(BF16) for c in chunks]
        inv = [inv[c] + both[c][:, LANES:] for c in chunks]
    last = each(lambda c: _dot(pw[c], diag2(inv[c])))
    inv = [inv[c] + last[c] for c in chunks]
    sol = each(lambda c: _dot(inv[c], jnp.concatenate([diag2(a_t[cs[c]]), diag2(w_s[c])], axis=1)))
    ahat_d = [diag2(sol[c][:, :LANES]) for c in chunks]
    u0_d = [diag2(sol[c][:, LANES:]) for c in chunks]

    def heads_down(x):
        return jnp.concatenate([x[:, :HEAD_DIM], x[:, HEAD_DIM:]], axis=0)

    bmn = each(lambda c: _dot_tn(heads_down(b_t[cs[c]] * p_l[c]),
                                 jnp.concatenate([ahat_d[c], u0_d[c]], axis=1)))
    kv = each(lambda c: _dot_tn(heads_down(k_t[cs[c]] * p_l[c]), v_d[c]))
    rb = each(lambda c: _dot(a_rb[c], ahat_d[c]))
    y0 = each(lambda c: _dot(jnp.concatenate([a_rb[c], a_rk[c]], axis=1),
                             jnp.concatenate([u0_d[c], v_d[c]], axis=0)))
    while carry["next"] < len(chunks):
        advance()

    state_ref[...] = carry["H"]
    y = jnp.concatenate(carry["ys"], axis=0)
    mean_blk = jnp.where(head_blk, 1.0 / HEAD_DIM, 0.0).astype(BF16)
    mean = _dot(y, mean_blk)
    yc = y - mean
    var = _dot(yc * yc, mean_blk)
    yn = yc * lax.rsqrt(var + GN_EPS) * lnw_ref[...] + lnb_ref[...]
    o_ref[...] = ((yn + post_st[prv, 0]) * post_st[prv, 1]).astype(o_ref.dtype)

    for c in chunks:
        m_st[cur, c] = (bmn[c][:, :LANES] + jnp.where(eye, p_l[c], 0.0)).astype(BF16)
        n_st[cur, c] = bmn[c][:, LANES:] + kv[c]
        rhat_st[cur, cs[c], :] = (r_t[cs[c]] + rb[c]).astype(BF16)
        y0_st[cur, cs[c], :] = y0[c]


def _rwkv(proj, mu, w0, w_up, a0, a_up, k_k, k_a, r_k, ln_w, ln_b, batch, seq):
    M = proj.shape[0]
    TT = RWKV_TT
    nt = seq // TT
    n_chunks = TT // CHUNK
    total = batch * PAIRS * nt

    def where(s):
        b, p, t = s // (PAIRS * nt), (s // nt) % PAIRS, s % nt
        return b * nt + t, p

    src = lambda s: where(jnp.minimum(s, total - 1))
    dst = lambda s: where(jnp.maximum(s - 1, 0))

    slab = lambda col: pl.BlockSpec((TT, LANES), lambda s: (src(s)[0], col // LANES + src(s)[1]))
    vec = pl.BlockSpec((1, LANES), lambda s: (0, src(s)[1]))
    vec0 = pl.BlockSpec((1, LANES), lambda s: (0, 0))
    vec_dst = pl.BlockSpec((1, LANES), lambda s: (0, dst(s)[1]))
    lora = pl.BlockSpec((LORA, LANES), lambda s: (0, src(s)[1]))
    mu_r, mu_k, mu_v, mu_wa = (mu[:, :1024], mu[:, 1024:2048], mu[:, 2048:3072], mu[:, 3072:])
    return pl.pallas_call(
        functools.partial(_rwkv_kernel, nt=nt, total=total),
        out_shape=jax.ShapeDtypeStruct((M, RWKV_WIDTH), BF16),
        grid=(total + 1,),
        in_specs=[slab(COL_R), slab(COL_RK), slab(COL_RV), slab(COL_RG),
                  pl.BlockSpec((TT, LANES), lambda s: (src(s)[0], COL_WA // LANES)),
                  vec, vec, vec, vec0,
                  vec, lora, vec, lora, vec, vec, vec, vec_dst, vec_dst],
        out_specs=pl.BlockSpec((TT, LANES), lambda s: dst(s)),
        scratch_shapes=[pltpu.VMEM((CHUNK, LANES), F32),
                        pltpu.VMEM((4, 8, LANES), F32),
                        pltpu.VMEM((2, n_chunks, CHUNK, LANES), BF16),
                        pltpu.VMEM((2, n_chunks, CHUNK, LANES), F32),
                        pltpu.VMEM((2, TT, LANES), BF16),
                        pltpu.VMEM((2, TT, LANES), F32),
                        pltpu.VMEM((2, 2, TT, LANES), F32)],
        compiler_params=pltpu.CompilerParams(dimension_semantics=("arbitrary",),
                                             vmem_limit_bytes=VMEM_LIMIT),
        name="rwkv7_mix",
    )(proj, proj, proj, proj, proj, mu_r, mu_k, mu_v, mu_wa,
      w0, w_up, a0, a_up, k_k, k_a, r_k, ln_w, ln_b)


def _out_proj_kernel(x_ref, ya_ref, yr_ref, wa_ref, wr_ref, g_ref, gate_ref, o_ref):
    mix = (jnp.dot(ya_ref[...], wa_ref[...], preferred_element_type=F32)
           + jnp.dot(yr_ref[...], wr_ref[...], preferred_element_type=F32))
    inv = lax.rsqrt(jnp.mean(mix * mix, axis=-1, keepdims=True) + RMS_EPS)
    o_ref[...] = x_ref[...] + (mix * inv) * (gate_ref[0] * g_ref[...])


def _out_proj(x2, ya, yr, w, g, gate, seq):
    M, D = x2.shape
    tm = 512
    per_b = seq // tm
    half = ya.shape[1]
    return pl.pallas_call(
        _out_proj_kernel,
        out_shape=jax.ShapeDtypeStruct((M, D), F32),
        grid=(M // tm,),
        in_specs=[pl.BlockSpec((tm, D), lambda i: (i, 0)),
                  pl.BlockSpec((tm, half), lambda i: (i, 0)),
                  pl.BlockSpec((tm, half), lambda i: (i, 0)),
                  pl.BlockSpec((half, D), lambda i: (0, 0)),
                  pl.BlockSpec((half, D), lambda i: (1, 0)),
                  pl.BlockSpec((1, D), lambda i: (0, 0)),
                  pl.BlockSpec((1, 1, D), lambda i: (i // per_b, 0, 0))],
        out_specs=pl.BlockSpec((tm, D), lambda i: (i, 0)),
        compiler_params=pltpu.CompilerParams(dimension_semantics=("parallel",),
                                             vmem_limit_bytes=VMEM_LIMIT),
        name="out_proj",
    )(x2, ya, yr, w, w, g, gate)


def kernel(x, c, w_ada, b_ada, pre_norm_g, post_norm_g, w_in, w_out, attn_sinks, rwkv_mu, rwkv_w0,
           rwkv_w_up, rwkv_a0, rwkv_a_up, rwkv_k_k, rwkv_k_a, rwkv_r_k, rwkv_ln_w, rwkv_ln_b):
    B, T, D = x.shape
    depth = w_ada.shape[0]
    x2 = x.reshape(B * T, D)
    for l in range(depth):
        mod = _adaln(c, w_ada[l], b_ada[l][None, :])
        shift, scale, gate = (mod[:, i * D:(i + 1) * D].reshape(B, 1, D) for i in range(3))
        proj = _in_proj(x2, pre_norm_g[l][None, :], scale, shift, w_in[l].astype(BF16), T)
        y_attn = _attention(proj, attn_sinks[l], B, T)
        y_rwkv = _rwkv(proj, rwkv_mu[l][None, :], rwkv_w0[l][None, :], rwkv_w_up[l].astype(BF16),
                       rwkv_a0[l][None, :], rwkv_a_up[l].astype(BF16), rwkv_k_k[l][None, :],
                       rwkv_k_a[l][None, :], rwkv_r_k[l].reshape(1, RWKV_WIDTH),
                       rwkv_ln_w[l][None, :], rwkv_ln_b[l][None, :], B, T)
        x2 = _out_proj(x2, y_attn, y_rwkv, w_out[l].astype(BF16), post_norm_g[l][None, :], gate, T)
    return x2.reshape(B, T, D)
```

```python
import functools
import math

import jax
import jax.numpy as jnp
from jax import lax
from jax.experimental import pallas as pl
from jax.experimental.pallas import tpu as pltpu

D_MODEL = 2048
HEAD_DIM = 64
ATTN_WIDTH = 1024
ATTN_Q_HEADS = 16
KV_WIDTH = 256
WINDOW = 128
RWKV_WIDTH = 1024
LORA = 64
IN_COLS = 6784
RMS_EPS = 1e-6
GN_EPS = 64e-5
NEG_BIG = -1e30

LANES = 128
PAIRS = RWKV_WIDTH // LANES
CHUNK = 64

COL_Q = 0
COL_K = 1024
COL_V = 1280
COL_GA = 1536
COL_R = 2560
COL_RK = 3584
COL_RV = 4608
COL_WA = 5632
COL_RG = 5760
GA_BLOCK = 512

VMEM_LIMIT = 56 * 1024 * 1024

F32 = jnp.float32
BF16 = jnp.bfloat16


def _dot(a, b):
    return jnp.dot(a.astype(BF16), b.astype(BF16), preferred_element_type=F32)


def _dot_nt(a, b):
    return lax.dot_general(a.astype(BF16), b.astype(BF16), (((1,), (1,)), ((), ())),
                           preferred_element_type=F32)


def _dot_tn(a, b):
    return lax.dot_general(a.astype(BF16), b.astype(BF16), (((0,), (0,)), ((), ())),
                           preferred_element_type=F32)


def _split2(x):
    hi = x.astype(BF16)
    return hi, (x - hi.astype(F32)).astype(BF16)


def _dot_exact_lhs(a_bf16, b):
    hi, lo = _split2(b)
    return jnp.dot(a_bf16, hi, preferred_element_type=F32) + jnp.dot(a_bf16, lo, preferred_element_type=F32)


def _sigmoid(x):
    return 1.0 / (1.0 + jnp.exp(-x))


def _silu(x):
    return x * _sigmoid(x)


def _adaln_kernel(c_ref, w_ref, b_ref, o_ref):
    s = _silu(c_ref[...])
    n = s.shape[0]
    acc = jnp.dot(jnp.concatenate(_split2(s), axis=0), w_ref[...].astype(BF16), preferred_element_type=F32)
    o_ref[...] = acc[:n] + acc[n:] + b_ref[...]


def _adaln(c, w, b):
    B, D = c.shape
    N = w.shape[1]
    tn = 1024
    return pl.pallas_call(
        _adaln_kernel,
        out_shape=jax.ShapeDtypeStruct((B, N), F32),
        grid=(N // tn,),
        in_specs=[pl.BlockSpec((B, D), lambda j: (0, 0)),
                  pl.BlockSpec((D, tn), lambda j: (0, j)),
                  pl.BlockSpec((1, tn), lambda j: (0, j))],
        out_specs=pl.BlockSpec((B, tn), lambda j: (0, j)),
        compiler_params=pltpu.CompilerParams(dimension_semantics=("arbitrary",),
                                             vmem_limit_bytes=VMEM_LIMIT),
        name="adaln_mod",
    )(c, w, b)


NORM_ROWS = 16


def _in_proj_kernel(x_ref, g_ref, scale_ref, shift_ref, w_ref, o_ref, h_ref):
    @pl.when(pl.program_id(1) == 0)
    def _():
        gain = g_ref[...] * (1.0 + scale_ref[0])
        shift = shift_ref[0]

        def rows(c, carry):
            sl = pl.ds(pl.multiple_of(c * NORM_ROWS, NORM_ROWS), NORM_ROWS)
            x = x_ref[sl, :]
            inv = lax.rsqrt(jnp.mean(x * x, axis=-1, keepdims=True) + RMS_EPS)
            h_ref[sl, :] = ((x * inv) * gain + shift).astype(BF16)
            return carry

        lax.fori_loop(0, x_ref.shape[0] // NORM_ROWS, rows, 0, unroll=16)

    o_ref[...] = jnp.dot(h_ref[...], w_ref[...], preferred_element_type=F32).astype(o_ref.dtype)


def _in_proj(x2, g, scale, shift, w_packed, seq):
    M, D = x2.shape
    NP = w_packed.shape[1]
    tm, tn = 512, 2304
    per_b = seq // tm
    return pl.pallas_call(
        _in_proj_kernel,
        out_shape=jax.ShapeDtypeStruct((M, NP), BF16),
        grid=(M // tm, pl.cdiv(NP, tn)),
        in_specs=[pl.BlockSpec((tm, D), lambda i, j: (i, 0)),
                  pl.BlockSpec((1, D), lambda i, j: (0, 0)),
                  pl.BlockSpec((1, 1, D), lambda i, j: (i // per_b, 0, 0)),
                  pl.BlockSpec((1, 1, D), lambda i, j: (i // per_b, 0, 0)),
                  pl.BlockSpec((D, tn), lambda i, j: (0, j))],
        out_specs=pl.BlockSpec((tm, tn), lambda i, j: (i, j)),
        scratch_shapes=[pltpu.VMEM((tm, D), BF16)],
        compiler_params=pltpu.CompilerParams(dimension_semantics=("parallel", "arbitrary"),
                                             vmem_limit_bytes=VMEM_LIMIT),
        name="in_proj",
    )(x2, g, scale, shift, w_packed)


LOG2_E = math.log2(math.e)
ATTN_GROUP = 8
ATTN_ROWS = 8 * WINDOW


def _dup_halves(slab, lane_lo):
    swapped = pltpu.roll(slab, HEAD_DIM, axis=1)
    return jnp.where(lane_lo, slab, swapped), jnp.where(lane_lo, swapped, slab)


def _attn_kernel(sink_ref, q_ref, ga_lo_ref, ga_hi_ref, k_ref, v_ref, o_ref, kprev_ref, vprev_ref):
    i = pl.program_id(1)
    blk = WINDOW
    lane_lo = lax.broadcasted_iota(jnp.int32, (blk, LANES), 1) < HEAD_DIM

    @pl.when(i == 0)
    def _():
        kprev_ref[...] = jnp.zeros_like(kprev_ref)
        vprev_ref[...] = jnp.zeros_like(vprev_ref)

    ones = jnp.ones((blk, LANES), BF16)
    qi = lax.broadcasted_iota(jnp.int32, (2 * blk, blk), 0) % blk
    si = lax.broadcasted_iota(jnp.int32, (2 * blk, blk), 1)
    cur_ok = si <= qi
    scale = HEAD_DIM ** -0.5 * LOG2_E
    group = ATTN_Q_HEADS // (KV_WIDTH // HEAD_DIM)
    n_kv = KV_WIDTH // HEAD_DIM
    row_lo = lax.broadcasted_iota(jnp.int32, (2 * blk, 1), 0) < blk

    kprev = [kprev_ref[g] for g in range(n_kv)]
    vprev = [vprev_ref[g] for g in range(n_kv)]
    for w in range(ATTN_ROWS // blk):
        rows = slice(w * blk, (w + 1) * blk)
        prev_ok = jnp.logical_and(si > qi, i > 0) if w == 0 else si > qi

        kcur, vcur = [], []
        for s in range(KV_WIDTH // LANES):
            ks = k_ref[rows, s * LANES:(s + 1) * LANES].astype(F32)
            vs = v_ref[rows, s * LANES:(s + 1) * LANES].astype(F32)
            kcur.extend(t.astype(BF16) for t in _dup_halves(ks, lane_lo))
            vcur.extend(t.astype(BF16) for t in _dup_halves(vs, lane_lo))

        def stacked_q(p):
            q = q_ref[rows, p * LANES:(p + 1) * LANES].astype(F32) * scale
            zero = jnp.zeros_like(q)
            return jnp.concatenate([jnp.where(lane_lo, q, zero), jnp.where(lane_lo, zero, q)], axis=0).astype(BF16)

        for p0 in range(0, PAIRS, ATTN_GROUP):
            ps = range(p0, p0 + ATTN_GROUP)
            kvh = {p: (2 * p) // group for p in ps}
            qs = {p: stacked_q(p) for p in ps}
            s = {p: jnp.where(cur_ok, _dot_nt(qs[p], kcur[kvh[p]]),
                              jnp.where(prev_ok, _dot_nt(qs[p], kprev[kvh[p]]), NEG_BIG)) for p in ps}
            sink = {p: jnp.where(row_lo, sink_ref[2 * p] * LOG2_E, sink_ref[2 * p + 1] * LOG2_E) for p in ps}
            m = {p: jnp.max(s[p], axis=-1, keepdims=True) for p in ps}
            e = {p: jnp.exp2(s[p] - m[p]) for p in ps}
            e2 = {p: jnp.concatenate([jnp.where(cur_ok, e[p], 0.0), jnp.where(cur_ok, 0.0, e[p])],
                                     axis=1).astype(BF16) for p in ps}
            acc = {p: jnp.dot(e2[p], jnp.concatenate([jnp.concatenate([vcur[kvh[p]], ones], axis=1),
                                                      jnp.concatenate([vprev[kvh[p]], ones], axis=1)], axis=0),
                              preferred_element_type=F32) for p in ps}
            for p in ps:
                num = acc[p][:, :LANES]
                den = acc[p][:, LANES:] + jnp.exp2(sink[p] - m[p])
                o = jnp.where(lane_lo, num[:blk], num[blk:]) / jnp.where(lane_lo, den[:blk], den[blk:])
                ga_ref, gp = (ga_lo_ref, p) if p < PAIRS // 2 else (ga_hi_ref, p - PAIRS // 2)
                ga = ga_ref[rows, gp * LANES:(gp + 1) * LANES].astype(F32)
                o_ref[rows, p * LANES:(p + 1) * LANES] = (o * _silu(ga)).astype(o_ref.dtype)
        kprev, vprev = kcur, vcur

    for g in range(n_kv):
        kprev_ref[g] = kprev[g]
        vprev_ref[g] = vprev[g]


def _attention(proj, sinks, batch, seq):
    M = proj.shape[0]
    nb = seq // ATTN_ROWS
    row = lambda b, i: b * nb + i
    n_kv = KV_WIDTH // HEAD_DIM
    return pl.pallas_call(
        _attn_kernel,
        out_shape=jax.ShapeDtypeStruct((M, ATTN_WIDTH), BF16),
        grid=(batch, nb),
        in_specs=[pl.BlockSpec(memory_space=pltpu.SMEM),
                  pl.BlockSpec((ATTN_ROWS, ATTN_WIDTH), lambda b, i: (row(b, i), COL_Q // ATTN_WIDTH)),
                  pl.BlockSpec((ATTN_ROWS, GA_BLOCK), lambda b, i: (row(b, i), COL_GA // GA_BLOCK)),
                  pl.BlockSpec((ATTN_ROWS, GA_BLOCK), lambda b, i: (row(b, i), COL_GA // GA_BLOCK + 1)),
                  pl.BlockSpec((ATTN_ROWS, KV_WIDTH), lambda b, i: (row(b, i), COL_K // KV_WIDTH)),
                  pl.BlockSpec((ATTN_ROWS, KV_WIDTH), lambda b, i: (row(b, i), COL_V // KV_WIDTH))],
        out_specs=pl.BlockSpec((ATTN_ROWS, ATTN_WIDTH), lambda b, i: (row(b, i), 0)),
        scratch_shapes=[pltpu.VMEM((n_kv, WINDOW, LANES), BF16),
                        pltpu.VMEM((n_kv, WINDOW, LANES), BF16)],
        compiler_params=pltpu.CompilerParams(dimension_semantics=("parallel", "arbitrary"),
                                             vmem_limit_bytes=VMEM_LIMIT),
        name="swa_attn",
    )(sinks, proj, proj, proj, proj, proj)


DECAY_SCALE = math.exp(-0.5)
RWKV_TT = 1024
LINK_PERIOD = 15


def _rwkv_kernel(r_ref, k_ref, v_ref, g_ref, wa_ref,
                 mu_r_ref, mu_k_ref, mu_v_ref, mu_wa_ref,
                 w0_ref, wup_ref, a0_ref, aup_ref, kk_ref, ka_ref, rk_ref, lnw_ref, lnb_ref,
                 o_ref, state_ref, last_ref, m_st, n_st, rhat_st, y0_st, post_st, *, nt, total):
    s = pl.program_id(0)
    TT = RWKV_TT
    L = CHUNK
    t_in = jnp.minimum(s, total - 1) % nt
    t_out = jnp.maximum(s - 1, 0) % nt
    cur = s % 2
    prv = 1 - cur

    @pl.when(s == 0)
    def _():
        for ref in (state_ref, last_ref, m_st, n_st, rhat_st, y0_st, post_st):
            ref[...] = jnp.zeros_like(ref)

    chunks = range(TT // L)
    cs = [slice(c * L, (c + 1) * L) for c in chunks]

    lane_lo = lax.broadcasted_iota(jnp.int32, (L, LANES), 1) < HEAD_DIM

    def diag2(x):
        zero = jnp.zeros_like(x)
        return jnp.concatenate([jnp.where(lane_lo, x, zero), jnp.where(lane_lo, zero, x)], axis=0).astype(BF16)

    carry = {"H": jnp.where(t_out == 0, 0.0, state_ref[...]), "ys": [], "next": 0}

    def advance():
        c = carry["next"]
        if c >= len(chunks):
            return
        H = diag2(carry["H"])
        carry["ys"].append(jnp.dot(rhat_st[prv, cs[c], :], H, preferred_element_type=F32) + y0_st[prv, cs[c], :])
        carry["H"] = jnp.dot(m_st[prv, c], H, preferred_element_type=F32) + n_st[prv, c]
        carry["next"] = c + 1

    row0 = lax.broadcasted_iota(jnp.int32, (8, LANES), 0) == 0
    first = t_in == 0

    def shifted(ref, slot, mu_ref):
        x = ref[...].astype(F32)
        carried = jnp.where(first, 0.0, last_ref[slot, 7:8, :])
        rolled = pltpu.roll(x, 1, axis=0)
        prev = jnp.concatenate([jnp.where(row0, carried, rolled[:8]), rolled[8:]], axis=0)
        last_ref[slot] = x[TT - 8:, :]
        return x + (prev - x) * mu_ref[...]

    r = shifted(r_ref, 0, mu_r_ref)
    k = shifted(k_ref, 1, mu_k_ref)
    v = shifted(v_ref, 2, mu_v_ref)
    wa = shifted(wa_ref, 3, mu_wa_ref)
    wd = wa[:, :LORA]
    ad = wa[:, LORA:]

    u = w0_ref[...] + _dot(jnp.tanh(wd), wup_ref[...])
    lw = -(DECAY_SCALE * LOG2_E) / (1.0 + jnp.exp(-u))
    a = _sigmoid(a0_ref[...] + _dot(ad, aup_ref[...]))
    advance()

    lane_r = lax.broadcasted_iota(jnp.int32, (LANES, LANES), 0)
    lane_c = lax.broadcasted_iota(jnp.int32, (LANES, LANES), 1)
    head_blk = (lane_r // HEAD_DIM) == (lane_c // HEAD_DIM)
    ones_blk = jnp.where(head_blk, 1.0, 0.0).astype(BF16)

    kk = k * kk_ref[...]
    kk = kk * lax.rsqrt(jnp.maximum(_dot(kk * kk, ones_blk), 1e-24))
    k = k * ((1.0 - ka_ref[...]) + a * ka_ref[...])
    av = -kk
    bv = kk * a
    post_st[cur, 0] = _dot(r * k * rk_ref[...], ones_blk) * v
    post_st[cur, 1] = _silu(g_ref[...].astype(F32))

    TB = 4 * L
    tr = lax.broadcasted_iota(jnp.int32, (TB, TB), 0)
    tc = lax.broadcasted_iota(jnp.int32, (TB, TB), 1)
    tri = jnp.where(jnp.logical_and(tr // L == tc // L, tc <= tr), 1.0, 0.0).astype(BF16)
    cum = jnp.concatenate([_dot_exact_lhs(tri, lw[i * TB:(i + 1) * TB]) for i in range(TT // TB)], axis=0)
    advance()

    e_pos = jnp.exp2(cum)
    e_neg = 1.0 / e_pos
    r_t = r * e_pos
    k_t = k * e_neg
    b_t = bv * e_neg
    a_t = av * jnp.exp2(cum - lw)

    row = lax.broadcasted_iota(jnp.int32, (L, LANES), 0)
    col = lax.broadcasted_iota(jnp.int32, (L, LANES), 1) % HEAD_DIM
    strict = col < row
    incl = col <= row
    eye = col == row

    ticks = {"n": 0}

    def each(fn):
        out = []
        for c in chunks:
            out.append(fn(c))
            ticks["n"] += 1
            if ticks["n"] % LINK_PERIOD == 0:
                advance()
        return out

    cum_l = [cum[sl][L - 1:L, :] for sl in cs]
    p_l = [jnp.exp2(cum_l[c]) for c in chunks]
    bk_d = [jnp.concatenate([diag2(b_t[sl]), diag2(k_t[sl])], axis=0) for sl in cs]
    v_d = [diag2(v[sl]) for sl in cs]
    def heads_down(x):
        return jnp.concatenate([x[:, :HEAD_DIM], x[:, HEAD_DIM:]], axis=0)

    gram = each(lambda c: _dot_nt(jnp.concatenate([a_t[cs[c]], r_t[cs[c]]], axis=0), bk_d[c]))
    kv = each(lambda c: _dot_tn(heads_down(k_t[cs[c]] * p_l[c]), v_d[c]))
    a_ab = [jnp.where(strict, gram[c][:L, :LANES], 0.0) for c in chunks]
    a_ak = [jnp.where(strict, gram[c][:L, LANES:], 0.0).astype(BF16) for c in chunks]
    a_rb = [jnp.where(incl, gram[c][L:, :LANES], 0.0).astype(BF16) for c in chunks]
    a_rk = [jnp.where(incl, gram[c][L:, LANES:], 0.0).astype(BF16) for c in chunks]
    inv = [jnp.where(eye, 1.0, 0.0) + a_ab[c] for c in chunks]
    pw = each(lambda c: _dot(a_ab[c], diag2(a_ab[c])).astype(BF16))
    w_s = each(lambda c: _dot(a_ak[c], v_d[c]).astype(BF16))
    for _ in range(4):
        both = each(lambda c: _dot(pw[c], jnp.concatenate([diag2(pw[c]), diag2(inv[c])], axis=1)))
        pw = [both[c][:, :LANES].astype(BF16) for c in chunks]
        inv = [inv[c] + both[c][:, LANES:] for c in chunks]
    last = each(lambda c: _dot(pw[c], diag2(inv[c])))
    inv = [inv[c] + last[c] for c in chunks]
    sol = each(lambda c: _dot(inv[c], jnp.concatenate([diag2(a_t[cs[c]]), diag2(w_s[c])], axis=1)))
    ahat_d = [diag2(sol[c][:, :LANES]) for c in chunks]
    u0_d = [diag2(sol[c][:, LANES:]) for c in chunks]

    bmn = each(lambda c: _dot_tn(heads_down(b_t[cs[c]] * p_l[c]),
                                 jnp.concatenate([ahat_d[c], u0_d[c]], axis=1)))
    rb = each(lambda c: _dot(a_rb[c], ahat_d[c]))
    y0 = each(lambda c: _dot(jnp.concatenate([a_rb[c], a_rk[c]], axis=1),
                             jnp.concatenate([u0_d[c], v_d[c]], axis=0)))
    while carry["next"] < len(chunks):
        advance()

    state_ref[...] = carry["H"]
    y = jnp.concatenate(carry["ys"], axis=0)
    mean_blk = jnp.where(head_blk, 1.0 / HEAD_DIM, 0.0).astype(BF16)
    mean = _dot(y, mean_blk)
    yc = y - mean
    var = _dot(yc * yc, mean_blk)
    yn = yc * lax.rsqrt(var + GN_EPS) * lnw_ref[...] + lnb_ref[...]
    o_ref[...] = ((yn + post_st[prv, 0]) * post_st[prv, 1]).astype(o_ref.dtype)

    for c in chunks:
        m_st[cur, c] = (bmn[c][:, :LANES] + jnp.where(eye, p_l[c], 0.0)).astype(BF16)
        n_st[cur, c] = bmn[c][:, LANES:] + kv[c]
        rhat_st[cur, cs[c], :] = (r_t[cs[c]] + rb[c]).astype(BF16)
        y0_st[cur, cs[c], :] = y0[c]


def _rwkv(proj, mu, w0, w_up, a0, a_up, k_k, k_a, r_k, ln_w, ln_b, batch, seq):
    M = proj.shape[0]
    TT = RWKV_TT
    nt = seq // TT
    n_chunks = TT // CHUNK
    total = batch * PAIRS * nt

    def where(s):
        b, p, t = s // (PAIRS * nt), (s // nt) % PAIRS, s % nt
        return b * nt + t, p

    src = lambda s: where(jnp.minimum(s, total - 1))
    dst = lambda s: where(jnp.maximum(s - 1, 0))

    slab = lambda col: pl.BlockSpec((TT, LANES), lambda s: (src(s)[0], col // LANES + src(s)[1]))
    vec = pl.BlockSpec((1, LANES), lambda s: (0, src(s)[1]))
    vec0 = pl.BlockSpec((1, LANES), lambda s: (0, 0))
    vec_dst = pl.BlockSpec((1, LANES), lambda s: (0, dst(s)[1]))
    lora = pl.BlockSpec((LORA, LANES), lambda s: (0, src(s)[1]))
    mu_r, mu_k, mu_v, mu_wa = (mu[:, :1024], mu[:, 1024:2048], mu[:, 2048:3072], mu[:, 3072:])
    return pl.pallas_call(
        functools.partial(_rwkv_kernel, nt=nt, total=total),
        out_shape=jax.ShapeDtypeStruct((M, RWKV_WIDTH), BF16),
        grid=(total + 1,),
        in_specs=[slab(COL_R), slab(COL_RK), slab(COL_RV), slab(COL_RG),
                  pl.BlockSpec((TT, LANES), lambda s: (src(s)[0], COL_WA // LANES)),
                  vec, vec, vec, vec0,
                  vec, lora, vec, lora, vec, vec, vec, vec_dst, vec_dst],
        out_specs=pl.BlockSpec((TT, LANES), lambda s: dst(s)),
        scratch_shapes=[pltpu.VMEM((CHUNK, LANES), F32),
                        pltpu.VMEM((4, 8, LANES), F32),
                        pltpu.VMEM((2, n_chunks, CHUNK, LANES), BF16),
                        pltpu.VMEM((2, n_chunks, CHUNK, LANES), F32),
                        pltpu.VMEM((2, TT, LANES), BF16),
                        pltpu.VMEM((2, TT, LANES), F32),
                        pltpu.VMEM((2, 2, TT, LANES), F32)],
        compiler_params=pltpu.CompilerParams(dimension_semantics=("arbitrary",),
                                             vmem_limit_bytes=VMEM_LIMIT),
        name="rwkv7_mix",
    )(proj, proj, proj, proj, proj, mu_r, mu_k, mu_v, mu_wa,
      w0, w_up, a0, a_up, k_k, k_a, r_k, ln_w, ln_b)


def _out_proj_kernel(x_ref, ya_ref, yr_ref, wa_ref, wr_ref, g_ref, gate_ref, o_ref):
    mix = (jnp.dot(ya_ref[...], wa_ref[...], preferred_element_type=F32)
           + jnp.dot(yr_ref[...], wr_ref[...], preferred_element_type=F32))
    inv = lax.rsqrt(jnp.mean(mix * mix, axis=-1, keepdims=True) + RMS_EPS)
    o_ref[...] = x_ref[...] + (mix * inv) * (gate_ref[0] * g_ref[...])


def _out_proj(x2, ya, yr, w, g, gate, seq):
    M, D = x2.shape
    tm = 512
    per_b = seq // tm
    half = ya.shape[1]
    return pl.pallas_call(
        _out_proj_kernel,
        out_shape=jax.ShapeDtypeStruct((M, D), F32),
        grid=(M // tm,),
        in_specs=[pl.BlockSpec((tm, D), lambda i: (i, 0)),
                  pl.BlockSpec((tm, half), lambda i: (i, 0)),
                  pl.BlockSpec((tm, half), lambda i: (i, 0)),
                  pl.BlockSpec((half, D), lambda i: (0, 0)),
                  pl.BlockSpec((half, D), lambda i: (1, 0)),
                  pl.BlockSpec((1, D), lambda i: (0, 0)),
                  pl.BlockSpec((1, 1, D), lambda i: (i // per_b, 0, 0))],
        out_specs=pl.BlockSpec((tm, D), lambda i: (i, 0)),
        compiler_params=pltpu.CompilerParams(dimension_semantics=("parallel",),
                                             vmem_limit_bytes=VMEM_LIMIT),
        name="out_proj",
    )(x2, ya, yr, w, w, g, gate)


def kernel(x, c, w_ada, b_ada, pre_norm_g, post_norm_g, w_in, w_out, attn_sinks, rwkv_mu, rwkv_w0,
           rwkv_w_up, rwkv_a0, rwkv_a_up, rwkv_k_k, rwkv_k_a, rwkv_r_k, rwkv_ln_w, rwkv_ln_b):
    B, T, D = x.shape
    depth = w_ada.shape[0]
    x2 = x.reshape(B * T, D)
    for l in range(depth):
        mod = _adaln(c, w_ada[l], b_ada[l][None, :])
        shift, scale, gate = (mod[:, i * D:(i + 1) * D].reshape(B, 1, D) for i in range(3))
        proj = _in_proj(x2, pre_norm_g[l][None, :], scale, shift, w_in[l].astype(BF16), T)
        y_attn = _attention(proj, attn_sinks[l], B, T)
        y_rwkv = _rwkv(proj, rwkv_mu[l][None, :], rwkv_w0[l][None, :], rwkv_w_up[l].astype(BF16),
                       rwkv_a0[l][None, :], rwkv_a_up[l].astype(BF16), rwkv_k_k[l][None, :],
                       rwkv_k_a[l][None, :], rwkv_r_k[l].reshape(1, RWKV_WIDTH),
                       rwkv_ln_w[l][None, :], rwkv_ln_b[l][None, :], B, T)
        x2 = _out_proj(x2, y_attn, y_rwkv, w_out[l].astype(BF16), post_norm_g[l][None, :], gate, T)
    return x2.reshape(B, T, D)
```

```python
import functools
import math

import jax
import jax.numpy as jnp
from jax import lax
from jax.experimental import pallas as pl
from jax.experimental.pallas import tpu as pltpu

D_MODEL = 2048
HEAD_DIM = 64
ATTN_WIDTH = 1024
ATTN_Q_HEADS = 16
KV_WIDTH = 256
WINDOW = 128
RWKV_WIDTH = 1024
LORA = 64
IN_COLS = 6784
RMS_EPS = 1e-6
GN_EPS = 64e-5
NEG_BIG = -1e30

LANES = 128
PAIRS = RWKV_WIDTH // LANES
CHUNK = 64

COL_Q = 0
COL_K = 1024
COL_V = 1280
COL_GA = 1536
COL_R = 2560
COL_RK = 3584
COL_RV = 4608
COL_WA = 5632
COL_RG = 5760
GA_BLOCK = 512

VMEM_LIMIT = 56 * 1024 * 1024

F32 = jnp.float32
BF16 = jnp.bfloat16


def _dot(a, b):
    return jnp.dot(a.astype(BF16), b.astype(BF16), preferred_element_type=F32)


def _dot_nt(a, b):
    return lax.dot_general(a.astype(BF16), b.astype(BF16), (((1,), (1,)), ((), ())),
                           preferred_element_type=F32)


def _dot_tn(a, b):
    return lax.dot_general(a.astype(BF16), b.astype(BF16), (((0,), (0,)), ((), ())),
                           preferred_element_type=F32)


def _split2(x):
    hi = x.astype(BF16)
    return hi, (x - hi.astype(F32)).astype(BF16)


def _dot_exact_lhs(a_bf16, b):
    hi, lo = _split2(b)
    return jnp.dot(a_bf16, hi, preferred_element_type=F32) + jnp.dot(a_bf16, lo, preferred_element_type=F32)


def _sigmoid(x):
    return 1.0 / (1.0 + jnp.exp(-x))


def _silu(x):
    return x * _sigmoid(x)


def _adaln_kernel(c_ref, w_ref, b_ref, o_ref):
    s = _silu(c_ref[...])
    n = s.shape[0]
    acc = jnp.dot(jnp.concatenate(_split2(s), axis=0), w_ref[...].astype(BF16), preferred_element_type=F32)
    o_ref[...] = acc[:n] + acc[n:] + b_ref[...]


def _adaln(c, w, b):
    B, D = c.shape
    N = w.shape[1]
    tn = 1024
    return pl.pallas_call(
        _adaln_kernel,
        out_shape=jax.ShapeDtypeStruct((B, N), F32),
        grid=(N // tn,),
        in_specs=[pl.BlockSpec((B, D), lambda j: (0, 0)),
                  pl.BlockSpec((D, tn), lambda j: (0, j)),
                  pl.BlockSpec((1, tn), lambda j: (0, j))],
        out_specs=pl.BlockSpec((B, tn), lambda j: (0, j)),
        compiler_params=pltpu.CompilerParams(dimension_semantics=("arbitrary",),
                                             vmem_limit_bytes=VMEM_LIMIT),
        name="adaln_mod",
    )(c, w, b)


NORM_ROWS = 16


def _in_proj_kernel(x_ref, g_ref, scale_ref, shift_ref, w_ref, o_ref, h_ref):
    @pl.when(pl.program_id(1) == 0)
    def _():
        gain = g_ref[...] * (1.0 + scale_ref[0])
        shift = shift_ref[0]

        def rows(c, carry):
            sl = pl.ds(pl.multiple_of(c * NORM_ROWS, NORM_ROWS), NORM_ROWS)
            x = x_ref[sl, :]
            inv = lax.rsqrt(jnp.mean(x * x, axis=-1, keepdims=True) + RMS_EPS)
            h_ref[sl, :] = ((x * inv) * gain + shift).astype(BF16)
            return carry

        lax.fori_loop(0, x_ref.shape[0] // NORM_ROWS, rows, 0, unroll=16)

    o_ref[...] = jnp.dot(h_ref[...], w_ref[...], preferred_element_type=F32).astype(o_ref.dtype)


def _in_proj(x2, g, scale, shift, w_packed, seq):
    M, D = x2.shape
    NP = w_packed.shape[1]
    tm, tn = 512, 2304
    per_b = seq // tm
    return pl.pallas_call(
        _in_proj_kernel,
        out_shape=jax.ShapeDtypeStruct((M, NP), BF16),
        grid=(M // tm, pl.cdiv(NP, tn)),
        in_specs=[pl.BlockSpec((tm, D), lambda i, j: (i, 0)),
                  pl.BlockSpec((1, D), lambda i, j: (0, 0)),
                  pl.BlockSpec((1, 1, D), lambda i, j: (i // per_b, 0, 0)),
                  pl.BlockSpec((1, 1, D), lambda i, j: (i // per_b, 0, 0)),
                  pl.BlockSpec((D, tn), lambda i, j: (0, j))],
        out_specs=pl.BlockSpec((tm, tn), lambda i, j: (i, j)),
        scratch_shapes=[pltpu.VMEM((tm, D), BF16)],
        compiler_params=pltpu.CompilerParams(dimension_semantics=("parallel", "arbitrary"),
                                             vmem_limit_bytes=VMEM_LIMIT),
        name="in_proj",
    )(x2, g, scale, shift, w_packed)


LOG2_E = math.log2(math.e)
ATTN_GROUP = 8
ATTN_ROWS = 8 * WINDOW


def _dup_halves(slab, lane_lo):
    swapped = pltpu.roll(slab, HEAD_DIM, axis=1)
    return jnp.where(lane_lo, slab, swapped), jnp.where(lane_lo, swapped, slab)


def _attn_kernel(sink_ref, q_ref, ga_lo_ref, ga_hi_ref, k_ref, v_ref, o_ref, kprev_ref, vprev_ref):
    i = pl.program_id(1)
    blk = WINDOW
    lane_lo = lax.broadcasted_iota(jnp.int32, (blk, LANES), 1) < HEAD_DIM

    @pl.when(i == 0)
    def _():
        kprev_ref[...] = jnp.zeros_like(kprev_ref)
        vprev_ref[...] = jnp.zeros_like(vprev_ref)

    ones = jnp.ones((blk, LANES), BF16)
    qi = lax.broadcasted_iota(jnp.int32, (2 * blk, blk), 0) % blk
    si = lax.broadcasted_iota(jnp.int32, (2 * blk, blk), 1)
    cur_ok = si <= qi
    scale = HEAD_DIM ** -0.5 * LOG2_E
    group = ATTN_Q_HEADS // (KV_WIDTH // HEAD_DIM)
    n_kv = KV_WIDTH // HEAD_DIM
    row_lo = lax.broadcasted_iota(jnp.int32, (2 * blk, 1), 0) < blk

    kprev = [kprev_ref[g] for g in range(n_kv)]
    vprev = [vprev_ref[g] for g in range(n_kv)]
    for w in range(ATTN_ROWS // blk):
        rows = slice(w * blk, (w + 1) * blk)
        prev_ok = jnp.logical_and(si > qi, i > 0) if w == 0 else si > qi

        kcur, vcur = [], []
        for s in range(KV_WIDTH // LANES):
            ks = k_ref[rows, s * LANES:(s + 1) * LANES].astype(F32) * scale
            vs = v_ref[rows, s * LANES:(s + 1) * LANES].astype(F32)
            kcur.extend(t.astype(BF16) for t in _dup_halves(ks, lane_lo))
            vcur.extend(t.astype(BF16) for t in _dup_halves(vs, lane_lo))

        def stacked_q(p):
            q = q_ref[rows, p * LANES:(p + 1) * LANES].astype(F32)
            zero = jnp.zeros_like(q)
            return jnp.concatenate([jnp.where(lane_lo, q, zero), jnp.where(lane_lo, zero, q)], axis=0).astype(BF16)

        for p0 in range(0, PAIRS, ATTN_GROUP):
            ps = range(p0, p0 + ATTN_GROUP)
            kvh = {p: (2 * p) // group for p in ps}
            qs = {p: stacked_q(p) for p in ps}
            s = {p: jnp.where(cur_ok, _dot_nt(qs[p], kcur[kvh[p]]),
                              jnp.where(prev_ok, _dot_nt(qs[p], kprev[kvh[p]]), NEG_BIG)) for p in ps}
            sink = {p: jnp.where(row_lo, sink_ref[2 * p] * LOG2_E, sink_ref[2 * p + 1] * LOG2_E) for p in ps}
            m = {p: jnp.max(s[p], axis=-1, keepdims=True) for p in ps}
            e = {p: jnp.exp2(s[p] - m[p]) for p in ps}
            e2 = {p: jnp.concatenate([jnp.where(cur_ok, e[p], 0.0), jnp.where(cur_ok, 0.0, e[p])],
                                     axis=1).astype(BF16) for p in ps}
            acc = {p: jnp.dot(e2[p], jnp.concatenate([jnp.concatenate([vcur[kvh[p]], ones], axis=1),
                                                      jnp.concatenate([vprev[kvh[p]], ones], axis=1)], axis=0),
                              preferred_element_type=F32) for p in ps}
            for p in ps:
                num = acc[p][:, :LANES]
                den = acc[p][:, LANES:] + jnp.exp2(sink[p] - m[p])
                o = jnp.where(lane_lo, num[:blk], num[blk:]) / jnp.where(lane_lo, den[:blk], den[blk:])
                ga_ref, gp = (ga_lo_ref, p) if p < PAIRS // 2 else (ga_hi_ref, p - PAIRS // 2)
                ga = ga_ref[rows, gp * LANES:(gp + 1) * LANES].astype(F32)
                o_ref[rows, p * LANES:(p + 1) * LANES] = (o * _silu(ga)).astype(o_ref.dtype)
        kprev, vprev = kcur, vcur

    for g in range(n_kv):
        kprev_ref[g] = kprev[g]
        vprev_ref[g] = vprev[g]


def _attention(proj, sinks, batch, seq):
    M = proj.shape[0]
    nb = seq // ATTN_ROWS
    row = lambda b, i: b * nb + i
    n_kv = KV_WIDTH // HEAD_DIM
    return pl.pallas_call(
        _attn_kernel,
        out_shape=jax.ShapeDtypeStruct((M, ATTN_WIDTH), BF16),
        grid=(batch, nb),
        in_specs=[pl.BlockSpec(memory_space=pltpu.SMEM),
                  pl.BlockSpec((ATTN_ROWS, ATTN_WIDTH), lambda b, i: (row(b, i), COL_Q // ATTN_WIDTH)),
                  pl.BlockSpec((ATTN_ROWS, GA_BLOCK), lambda b, i: (row(b, i), COL_GA // GA_BLOCK)),
                  pl.BlockSpec((ATTN_ROWS, GA_BLOCK), lambda b, i: (row(b, i), COL_GA // GA_BLOCK + 1)),
                  pl.BlockSpec((ATTN_ROWS, KV_WIDTH), lambda b, i: (row(b, i), COL_K // KV_WIDTH)),
                  pl.BlockSpec((ATTN_ROWS, KV_WIDTH), lambda b, i: (row(b, i), COL_V // KV_WIDTH))],
        out_specs=pl.BlockSpec((ATTN_ROWS, ATTN_WIDTH), lambda b, i: (row(b, i), 0)),
        scratch_shapes=[pltpu.VMEM((n_kv, WINDOW, LANES), BF16),
                        pltpu.VMEM((n_kv, WINDOW, LANES), BF16)],
        compiler_params=pltpu.CompilerParams(dimension_semantics=("parallel", "arbitrary"),
                                             vmem_limit_bytes=VMEM_LIMIT),
        name="swa_attn",
    )(sinks, proj, proj, proj, proj, proj)


DECAY_SCALE = math.exp(-0.5)
RWKV_TT = 1024
LINK_PERIOD = 15


def _rwkv_kernel(r_ref, k_ref, v_ref, g_ref, wa_ref,
                 mu_r_ref, mu_k_ref, mu_v_ref, mu_wa_ref,
                 w0_ref, wup_ref, a0_ref, aup_ref, kk_ref, ka_ref, rk_ref, lnw_ref, lnb_ref,
                 o_ref, state_ref, last_ref, m_st, n_st, rhat_st, y0_st, post_st, *, nt, total):
    s = pl.program_id(0)
    TT = RWKV_TT
    L = CHUNK
    t_in = jnp.minimum(s, total - 1) % nt
    t_out = jnp.maximum(s - 1, 0) % nt
    cur = s % 2
    prv = 1 - cur

    @pl.when(s == 0)
    def _():
        for ref in (state_ref, last_ref, m_st, n_st, rhat_st, y0_st, post_st):
            ref[...] = jnp.zeros_like(ref)

    chunks = range(TT // L)
    cs = [slice(c * L, (c + 1) * L) for c in chunks]

    lane_lo = lax.broadcasted_iota(jnp.int32, (L, LANES), 1) < HEAD_DIM

    def diag2(x):
        zero = jnp.zeros_like(x)
        return jnp.concatenate([jnp.where(lane_lo, x, zero), jnp.where(lane_lo, zero, x)], axis=0).astype(BF16)

    carry = {"H": jnp.where(t_out == 0, 0.0, state_ref[...]), "ys": [], "next": 0}

    def advance():
        c = carry["next"]
        if c >= len(chunks):
            return
        H = diag2(carry["H"])
        carry["ys"].append(jnp.dot(rhat_st[prv, cs[c], :], H, preferred_element_type=F32) + y0_st[prv, cs[c], :])
        carry["H"] = jnp.dot(m_st[prv, c], H, preferred_element_type=F32) + n_st[prv, c]
        carry["next"] = c + 1

    row0 = lax.broadcasted_iota(jnp.int32, (8, LANES), 0) == 0
    first = t_in == 0

    def shifted(ref, slot, mu_ref):
        x = ref[...].astype(F32)
        carried = jnp.where(first, 0.0, last_ref[slot, 7:8, :])
        rolled = pltpu.roll(x, 1, axis=0)
        prev = jnp.concatenate([jnp.where(row0, carried, rolled[:8]), rolled[8:]], axis=0)
        last_ref[slot] = x[TT - 8:, :]
        return x + (prev - x) * mu_ref[...]

    r = shifted(r_ref, 0, mu_r_ref)
    k = shifted(k_ref, 1, mu_k_ref)
    v = shifted(v_ref, 2, mu_v_ref)
    wa = shifted(wa_ref, 3, mu_wa_ref)
    wd = wa[:, :LORA]
    ad = wa[:, LORA:]

    u = w0_ref[...] + _dot(jnp.tanh(wd), wup_ref[...])
    lw = -(DECAY_SCALE * LOG2_E) / (1.0 + jnp.exp(-u))
    a = _sigmoid(a0_ref[...] + _dot(ad, aup_ref[...]))
    advance()

    lane_r = lax.broadcasted_iota(jnp.int32, (LANES, LANES), 0)
    lane_c = lax.broadcasted_iota(jnp.int32, (LANES, LANES), 1)
    head_blk = (lane_r // HEAD_DIM) == (lane_c // HEAD_DIM)
    ones_blk = jnp.where(head_blk, 1.0, 0.0).astype(BF16)

    kk = k * kk_ref[...]
    kk = kk * lax.rsqrt(jnp.maximum(_dot(kk * kk, ones_blk), 1e-24))
    k = k * ((1.0 - ka_ref[...]) + a * ka_ref[...])
    av = -kk
    bv = kk * a
    post_st[cur, 0] = _dot(r * k * rk_ref[...], ones_blk) * v
    post_st[cur, 1] = _silu(g_ref[...].astype(F32))

    TB = 4 * L
    tr = lax.broadcasted_iota(jnp.int32, (TB, TB), 0)
    tc = lax.broadcasted_iota(jnp.int32, (TB, TB), 1)
    tri = jnp.where(jnp.logical_and(tr // L == tc // L, tc <= tr), 1.0, 0.0).astype(BF16)
    cum = jnp.concatenate([_dot_exact_lhs(tri, lw[i * TB:(i + 1) * TB]) for i in range(TT // TB)], axis=0)
    advance()

    e_pos = jnp.exp2(cum)
    e_neg = 1.0 / e_pos
    r_t = r * e_pos
    k_t = k * e_neg
    b_t = bv * e_neg
    a_t = av * jnp.exp2(cum - lw)

    row = lax.broadcasted_iota(jnp.int32, (L, LANES), 0)
    col = lax.broadcasted_iota(jnp.int32, (L, LANES), 1) % HEAD_DIM
    strict = col < row
    incl = col <= row
    eye = col == row

    ticks = {"n": 0}

    def each(fn):
        out = []
        for c in chunks:
            out.append(fn(c))
            ticks["n"] += 1
            if ticks["n"] % LINK_PERIOD == 0:
                advance()
        return out

    cum_l = [cum[sl][L - 1:L, :] for sl in cs]
    p_l = [jnp.exp2(cum_l[c]) for c in chunks]
    bk_d = [jnp.concatenate([diag2(b_t[sl]), diag2(k_t[sl])], axis=0) for sl in cs]
    v_d = [diag2(v[sl]) for sl in cs]
    def heads_down(x):
        return jnp.concatenate([x[:, :HEAD_DIM], x[:, HEAD_DIM:]], axis=0)

    gram = each(lambda c: _dot_nt(jnp.concatenate([a_t[cs[c]], r_t[cs[c]]], axis=0), bk_d[c]))
    kv = each(lambda c: _dot_tn(heads_down(k_t[cs[c]] * p_l[c]), v_d[c]))
    a_ab = [jnp.where(strict, gram[c][:L, :LANES], 0.0) for c in chunks]
    a_ak = [jnp.where(strict, gram[c][:L, LANES:], 0.0).astype(BF16) for c in chunks]
    a_rb = [jnp.where(incl, gram[c][L:, :LANES], 0.0).astype(BF16) for c in chunks]
    a_rk = [jnp.where(incl, gram[c][L:, LANES:], 0.0).astype(BF16) for c in chunks]
    inv = [jnp.where(eye, 1.0, 0.0) + a_ab[c] for c in chunks]
    pw = each(lambda c: _dot(a_ab[c], diag2(a_ab[c])).astype(BF16))
    w_s = each(lambda c: _dot(a_ak[c], v_d[c]).astype(BF16))
    for _ in range(4):
        both = each(lambda c: _dot(pw[c], jnp.concatenate([diag2(pw[c]), diag2(inv[c])], axis=1)))
        pw = [both[c][:, :LANES].astype(BF16) for c in chunks]
        inv = [inv[c] + both[c][:, LANES:] for c in chunks]
    last = each(lambda c: _dot(pw[c], diag2(inv[c])))
    inv = [inv[c] + last[c] for c in chunks]
    sol = each(lambda c: _dot(inv[c], jnp.concatenate([diag2(a_t[cs[c]]), diag2(w_s[c])], axis=1)))
    ahat_d = [diag2(sol[c][:, :LANES]) for c in chunks]
    u0_d = [diag2(sol[c][:, LANES:]) for c in chunks]

    bmn = each(lambda c: _dot_tn(heads_down(b_t[cs[c]] * p_l[c]),
                                 jnp.concatenate([ahat_d[c], u0_d[c]], axis=1)))
    rb = each(lambda c: _dot(a_rb[c], ahat_d[c]))
    y0 = each(lambda c: _dot(jnp.concatenate([a_rb[c], a_rk[c]], axis=1),
                             jnp.concatenate([u0_d[c], v_d[c]], axis=0)))
    while carry["next"] < len(chunks):
        advance()

    state_ref[...] = carry["H"]
    y = jnp.concatenate(carry["ys"], axis=0)
    mean_blk = jnp.where(head_blk, 1.0 / HEAD_DIM, 0.0).astype(BF16)
    mean = _dot(y, mean_blk)
    yc = y - mean
    var = _dot(yc * yc, mean_blk)
    yn = yc * lax.rsqrt(var + GN_EPS) * lnw_ref[...] + lnb_ref[...]
    o_ref[...] = ((yn + post_st[prv, 0]) * post_st[prv, 1]).astype(o_ref.dtype)

    for c in chunks:
        m_st[cur, c] = (bmn[c][:, :LANES] + jnp.where(eye, p_l[c], 0.0)).astype(BF16)
        n_st[cur, c] = bmn[c][:, LANES:] + kv[c]
        rhat_st[cur, cs[c], :] = (r_t[cs[c]] + rb[c]).astype(BF16)
        y0_st[cur, cs[c], :] = y0[c]


def _rwkv(proj, mu, w0, w_up, a0, a_up, k_k, k_a, r_k, ln_w, ln_b, batch, seq):
    M = proj.shape[0]
    TT = RWKV_TT
    nt = seq // TT
    n_chunks = TT // CHUNK
    total = batch * PAIRS * nt

    def where(s):
        b, p, t = s // (PAIRS * nt), (s // nt) % PAIRS, s % nt
        return b * nt + t, p

    src = lambda s: where(jnp.minimum(s, total - 1))
    dst = lambda s: where(jnp.maximum(s - 1, 0))

    slab = lambda col: pl.BlockSpec((TT, LANES), lambda s: (src(s)[0], col // LANES + src(s)[1]))
    vec = pl.BlockSpec((1, LANES), lambda s: (0, src(s)[1]))
    vec0 = pl.BlockSpec((1, LANES), lambda s: (0, 0))
    vec_dst = pl.BlockSpec((1, LANES), lambda s: (0, dst(s)[1]))
    lora = pl.BlockSpec((LORA, LANES), lambda s: (0, src(s)[1]))
    mu_r, mu_k, mu_v, mu_wa = (mu[:, :1024], mu[:, 1024:2048], mu[:, 2048:3072], mu[:, 3072:])
    return pl.pallas_call(
        functools.partial(_rwkv_kernel, nt=nt, total=total),
        out_shape=jax.ShapeDtypeStruct((M, RWKV_WIDTH), BF16),
        grid=(total + 1,),
        in_specs=[slab(COL_R), slab(COL_RK), slab(COL_RV), slab(COL_RG),
                  pl.BlockSpec((TT, LANES), lambda s: (src(s)[0], COL_WA // LANES)),
                  vec, vec, vec, vec0,
                  vec, lora, vec, lora, vec, vec, vec, vec_dst, vec_dst],
        out_specs=pl.BlockSpec((TT, LANES), lambda s: dst(s)),
        scratch_shapes=[pltpu.VMEM((CHUNK, LANES), F32),
                        pltpu.VMEM((4, 8, LANES), F32),
                        pltpu.VMEM((2, n_chunks, CHUNK, LANES), BF16),
                        pltpu.VMEM((2, n_chunks, CHUNK, LANES), F32),
                        pltpu.VMEM((2, TT, LANES), BF16),
                        pltpu.VMEM((2, TT, LANES), F32),
                        pltpu.VMEM((2, 2, TT, LANES), F32)],
        compiler_params=pltpu.CompilerParams(dimension_semantics=("arbitrary",),
                                             vmem_limit_bytes=VMEM_LIMIT),
        name="rwkv7_mix",
    )(proj, proj, proj, proj, proj, mu_r, mu_k, mu_v, mu_wa,
      w0, w_up, a0, a_up, k_k, k_a, r_k, ln_w, ln_b)


def _out_proj_kernel(x_ref, ya_ref, yr_ref, wa_ref, wr_ref, g_ref, gate_ref, o_ref):
    mix = (jnp.dot(ya_ref[...], wa_ref[...], preferred_element_type=F32)
           + jnp.dot(yr_ref[...], wr_ref[...], preferred_element_type=F32))
    inv = lax.rsqrt(jnp.mean(mix * mix, axis=-1, keepdims=True) + RMS_EPS)
    o_ref[...] = x_ref[...] + (mix * inv) * (gate_ref[0] * g_ref[...])


def _out_proj(x2, ya, yr, w, g, gate, seq):
    M, D = x2.shape
    tm = 512
    per_b = seq // tm
    half = ya.shape[1]
    return pl.pallas_call(
        _out_proj_kernel,
        out_shape=jax.ShapeDtypeStruct((M, D), F32),
        grid=(M // tm,),
        in_specs=[pl.BlockSpec((tm, D), lambda i: (i, 0)),
                  pl.BlockSpec((tm, half), lambda i: (i, 0)),
                  pl.BlockSpec((tm, half), lambda i: (i, 0)),
                  pl.BlockSpec((half, D), lambda i: (0, 0)),
                  pl.BlockSpec((half, D), lambda i: (1, 0)),
                  pl.BlockSpec((1, D), lambda i: (0, 0)),
                  pl.BlockSpec((1, 1, D), lambda i: (i // per_b, 0, 0))],
        out_specs=pl.BlockSpec((tm, D), lambda i: (i, 0)),
        compiler_params=pltpu.CompilerParams(dimension_semantics=("parallel",),
                                             vmem_limit_bytes=VMEM_LIMIT),
        name="out_proj",
    )(x2, ya, yr, w, w, g, gate)


def kernel(x, c, w_ada, b_ada, pre_norm_g, post_norm_g, w_in, w_out, attn_sinks, rwkv_mu, rwkv_w0,
           rwkv_w_up, rwkv_a0, rwkv_a_up, rwkv_k_k, rwkv_k_a, rwkv_r_k, rwkv_ln_w, rwkv_ln_b):
    B, T, D = x.shape
    depth = w_ada.shape[0]
    x2 = x.reshape(B * T, D)
    for l in range(depth):
        mod = _adaln(c, w_ada[l], b_ada[l][None, :])
        shift, scale, gate = (mod[:, i * D:(i + 1) * D].reshape(B, 1, D) for i in range(3))
        proj = _in_proj(x2, pre_norm_g[l][None, :], scale, shift, w_in[l].astype(BF16), T)
        y_attn = _attention(proj, attn_sinks[l], B, T)
        y_rwkv = _rwkv(proj, rwkv_mu[l][None, :], rwkv_w0[l][None, :], rwkv_w_up[l].astype(BF16),
                       rwkv_a0[l][None, :], rwkv_a_up[l].astype(BF16), rwkv_k_k[l][None, :],
                       rwkv_k_a[l][None, :], rwkv_r_k[l].reshape(1, RWKV_WIDTH),
                       rwkv_ln_w[l][None, :], rwkv_ln_b[l][None, :], B, T)
        x2 = _out_proj(x2, y_attn, y_rwkv, w_out[l].astype(BF16), post_norm_g[l][None, :], gate, T)
    return x2.reshape(B, T, D)
```

```python
import functools
import math

import jax
import jax.numpy as jnp
from jax import lax
from jax.experimental import pallas as pl
from jax.experimental.pallas import tpu as pltpu

D_MODEL = 2048
HEAD_DIM = 64
ATTN_WIDTH = 1024
ATTN_Q_HEADS = 16
KV_WIDTH = 256
WINDOW = 128
RWKV_WIDTH = 1024
LORA = 64
IN_COLS = 6784
RMS_EPS = 1e-6
GN_EPS = 64e-5
NEG_BIG = -1e30

LANES = 128
PAIRS = RWKV_WIDTH // LANES
CHUNK = 64

COL_Q = 0
COL_K = 1024
COL_V = 1280
COL_GA = 1536
COL_R = 2560
COL_RK = 3584
COL_RV = 4608
COL_WA = 5632
COL_RG = 5760
GA_BLOCK = 512

VMEM_LIMIT = 56 * 1024 * 1024

F32 = jnp.float32
BF16 = jnp.bfloat16


def _dot(a, b):
    return jnp.dot(a.astype(BF16), b.astype(BF16), preferred_element_type=F32)


def _dot_nt(a, b):
    return lax.dot_general(a.astype(BF16), b.astype(BF16), (((1,), (1,)), ((), ())),
                           preferred_element_type=F32)


def _dot_tn(a, b):
    return lax.dot_general(a.astype(BF16), b.astype(BF16), (((0,), (0,)), ((), ())),
                           preferred_element_type=F32)


def _split2(x):
    hi = x.astype(BF16)
    return hi, (x - hi.astype(F32)).astype(BF16)


def _dot_exact_lhs(a_bf16, b):
    hi, lo = _split2(b)
    return jnp.dot(a_bf16, hi, preferred_element_type=F32) + jnp.dot(a_bf16, lo, preferred_element_type=F32)


def _sigmoid(x):
    return 1.0 / (1.0 + jnp.exp(-x))


def _silu(x):
    return x * _sigmoid(x)


def _adaln_kernel(c_ref, w_ref, b_ref, o_ref):
    s = _silu(c_ref[...])
    n = s.shape[0]
    acc = jnp.dot(jnp.concatenate(_split2(s), axis=0), w_ref[...].astype(BF16), preferred_element_type=F32)
    o_ref[...] = acc[:n] + acc[n:] + b_ref[...]


def _adaln(c, w, b):
    B, D = c.shape
    N = w.shape[1]
    tn = 1024
    return pl.pallas_call(
        _adaln_kernel,
        out_shape=jax.ShapeDtypeStruct((B, N), F32),
        grid=(N // tn,),
        in_specs=[pl.BlockSpec((B, D), lambda j: (0, 0)),
                  pl.BlockSpec((D, tn), lambda j: (0, j)),
                  pl.BlockSpec((1, tn), lambda j: (0, j))],
        out_specs=pl.BlockSpec((B, tn), lambda j: (0, j)),
        compiler_params=pltpu.CompilerParams(dimension_semantics=("arbitrary",),
                                             vmem_limit_bytes=VMEM_LIMIT),
        name="adaln_mod",
    )(c, w, b)


NORM_ROWS = 16


def _in_proj_kernel(x_ref, g_ref, scale_ref, shift_ref, w_ref, o_ref, h_ref):
    gain = g_ref[...] * (1.0 + scale_ref[0])
    shift = shift_ref[0]

    def rows(c, carry):
        sl = pl.ds(pl.multiple_of(c * NORM_ROWS, NORM_ROWS), NORM_ROWS)
        x = x_ref[sl, :]
        inv = lax.rsqrt(jnp.mean(x * x, axis=-1, keepdims=True) + RMS_EPS)
        h_ref[sl, :] = ((x * inv) * gain + shift).astype(BF16)
        return carry

    lax.fori_loop(0, x_ref.shape[0] // NORM_ROWS, rows, 0, unroll=16)
    o_ref[...] = jnp.dot(h_ref[...], w_ref[...], preferred_element_type=F32).astype(o_ref.dtype)


def _in_proj(x2, g, scale, shift, w_packed, seq):
    M, D = x2.shape
    NP = w_packed.shape[1]
    tm, tn = 512, 2304
    per_b = seq // tm
    return pl.pallas_call(
        _in_proj_kernel,
        out_shape=jax.ShapeDtypeStruct((M, NP), BF16),
        grid=(pl.cdiv(NP, tn), M // tm),
        in_specs=[pl.BlockSpec((tm, D), lambda j, i: (i, 0)),
                  pl.BlockSpec((1, D), lambda j, i: (0, 0)),
                  pl.BlockSpec((1, 1, D), lambda j, i: (i // per_b, 0, 0)),
                  pl.BlockSpec((1, 1, D), lambda j, i: (i // per_b, 0, 0)),
                  pl.BlockSpec((D, tn), lambda j, i: (0, j))],
        out_specs=pl.BlockSpec((tm, tn), lambda j, i: (i, j)),
        scratch_shapes=[pltpu.VMEM((tm, D), BF16)],
        compiler_params=pltpu.CompilerParams(dimension_semantics=("arbitrary", "arbitrary"),
                                             vmem_limit_bytes=VMEM_LIMIT),
        name="in_proj",
    )(x2, g, scale, shift, w_packed)


LOG2_E = math.log2(math.e)
ATTN_GROUP = 8
ATTN_ROWS = 8 * WINDOW


def _dup_halves(slab, lane_lo):
    swapped = pltpu.roll(slab, HEAD_DIM, axis=1)
    return jnp.where(lane_lo, slab, swapped), jnp.where(lane_lo, swapped, slab)


def _attn_kernel(sink_ref, q_ref, ga_lo_ref, ga_hi_ref, k_ref, v_ref, o_ref, kprev_ref, vprev_ref):
    i = pl.program_id(1)
    blk = WINDOW
    lane_lo = lax.broadcasted_iota(jnp.int32, (blk, LANES), 1) < HEAD_DIM

    @pl.when(i == 0)
    def _():
        kprev_ref[...] = jnp.zeros_like(kprev_ref)
        vprev_ref[...] = jnp.zeros_like(vprev_ref)

    ones = jnp.ones((blk, LANES), BF16)
    qi = lax.broadcasted_iota(jnp.int32, (2 * blk, blk), 0) % blk
    si = lax.broadcasted_iota(jnp.int32, (2 * blk, blk), 1)
    cur_ok = si <= qi
    scale = HEAD_DIM ** -0.5 * LOG2_E
    group = ATTN_Q_HEADS // (KV_WIDTH // HEAD_DIM)
    n_kv = KV_WIDTH // HEAD_DIM
    row_lo = lax.broadcasted_iota(jnp.int32, (2 * blk, 1), 0) < blk

    kprev = [kprev_ref[g] for g in range(n_kv)]
    vprev = [vprev_ref[g] for g in range(n_kv)]
    for w in range(ATTN_ROWS // blk):
        rows = slice(w * blk, (w + 1) * blk)
        prev_ok = jnp.logical_and(si > qi, i > 0) if w == 0 else si > qi

        kcur, vcur = [], []
        for s in range(KV_WIDTH // LANES):
            ks = k_ref[rows, s * LANES:(s + 1) * LANES].astype(F32) * scale
            vs = v_ref[rows, s * LANES:(s + 1) * LANES].astype(F32)
            kcur.extend(t.astype(BF16) for t in _dup_halves(ks, lane_lo))
            vcur.extend(t.astype(BF16) for t in _dup_halves(vs, lane_lo))

        def stacked_q(p):
            q = q_ref[rows, p * LANES:(p + 1) * LANES].astype(F32)
            zero = jnp.zeros_like(q)
            return jnp.concatenate([jnp.where(lane_lo, q, zero), jnp.where(lane_lo, zero, q)], axis=0).astype(BF16)

        for p0 in range(0, PAIRS, ATTN_GROUP):
            ps = range(p0, p0 + ATTN_GROUP)
            kvh = {p: (2 * p) // group for p in ps}
            qs = {p: stacked_q(p) for p in ps}
            s = {p: jnp.where(cur_ok, _dot_nt(qs[p], kcur[kvh[p]]),
                              jnp.where(prev_ok, _dot_nt(qs[p], kprev[kvh[p]]), NEG_BIG)) for p in ps}
            sink = {p: jnp.where(row_lo, sink_ref[2 * p] * LOG2_E, sink_ref[2 * p + 1] * LOG2_E) for p in ps}
            m = {p: jnp.max(s[p], axis=-1, keepdims=True) for p in ps}
            e = {p: jnp.exp2(s[p] - m[p]) for p in ps}
            e2 = {p: jnp.concatenate([jnp.where(cur_ok, e[p], 0.0), jnp.where(cur_ok, 0.0, e[p])],
                                     axis=1).astype(BF16) for p in ps}
            acc = {p: jnp.dot(e2[p], jnp.concatenate([jnp.concatenate([vcur[kvh[p]], ones], axis=1),
                                                      jnp.concatenate([vprev[kvh[p]], ones], axis=1)], axis=0),
                              preferred_element_type=F32) for p in ps}
            for p in ps:
                num = acc[p][:, :LANES]
                den = acc[p][:, LANES:] + jnp.exp2(sink[p] - m[p])
                o = jnp.where(lane_lo, num[:blk], num[blk:]) / jnp.where(lane_lo, den[:blk], den[blk:])
                ga_ref, gp = (ga_lo_ref, p) if p < PAIRS // 2 else (ga_hi_ref, p - PAIRS // 2)
                ga = ga_ref[rows, gp * LANES:(gp + 1) * LANES].astype(F32)
                o_ref[rows, p * LANES:(p + 1) * LANES] = (o * _silu(ga)).astype(o_ref.dtype)
        kprev, vprev = kcur, vcur

    for g in range(n_kv):
        kprev_ref[g] = kprev[g]
        vprev_ref[g] = vprev[g]


def _attention(proj, sinks, batch, seq):
    M = proj.shape[0]
    nb = seq // ATTN_ROWS
    row = lambda b, i: b * nb + i
    n_kv = KV_WIDTH // HEAD_DIM
    return pl.pallas_call(
        _attn_kernel,
        out_shape=jax.ShapeDtypeStruct((M, ATTN_WIDTH), BF16),
        grid=(batch, nb),
        in_specs=[pl.BlockSpec(memory_space=pltpu.SMEM),
                  pl.BlockSpec((ATTN_ROWS, ATTN_WIDTH), lambda b, i: (row(b, i), COL_Q // ATTN_WIDTH)),
                  pl.BlockSpec((ATTN_ROWS, GA_BLOCK), lambda b, i: (row(b, i), COL_GA // GA_BLOCK)),
                  pl.BlockSpec((ATTN_ROWS, GA_BLOCK), lambda b, i: (row(b, i), COL_GA // GA_BLOCK + 1)),
                  pl.BlockSpec((ATTN_ROWS, KV_WIDTH), lambda b, i: (row(b, i), COL_K // KV_WIDTH)),
                  pl.BlockSpec((ATTN_ROWS, KV_WIDTH), lambda b, i: (row(b, i), COL_V // KV_WIDTH))],
        out_specs=pl.BlockSpec((ATTN_ROWS, ATTN_WIDTH), lambda b, i: (row(b, i), 0)),
        scratch_shapes=[pltpu.VMEM((n_kv, WINDOW, LANES), BF16),
                        pltpu.VMEM((n_kv, WINDOW, LANES), BF16)],
        compiler_params=pltpu.CompilerParams(dimension_semantics=("parallel", "arbitrary"),
                                             vmem_limit_bytes=VMEM_LIMIT),
        name="swa_attn",
    )(sinks, proj, proj, proj, proj, proj)


DECAY_SCALE = math.exp(-0.5)
RWKV_TT = 1024
LINK_PERIOD = 15


def _rwkv_kernel(r_ref, k_ref, v_ref, g_ref, wa_ref,
                 mu_r_ref, mu_k_ref, mu_v_ref, mu_wa_ref,
                 w0_ref, wup_ref, a0_ref, aup_ref, kk_ref, ka_ref, rk_ref, lnw_ref, lnb_ref,
                 o_ref, state_ref, last_ref, m_st, n_st, rhat_st, y0_st, post_st, *, nt, total):
    s = pl.program_id(0)
    TT = RWKV_TT
    L = CHUNK
    t_in = jnp.minimum(s, total - 1) % nt
    t_out = jnp.maximum(s - 1, 0) % nt
    cur = s % 2
    prv = 1 - cur

    @pl.when(s == 0)
    def _():
        for ref in (state_ref, last_ref, m_st, n_st, rhat_st, y0_st, post_st):
            ref[...] = jnp.zeros_like(ref)

    chunks = range(TT // L)
    cs = [slice(c * L, (c + 1) * L) for c in chunks]

    lane_lo = lax.broadcasted_iota(jnp.int32, (L, LANES), 1) < HEAD_DIM

    def diag2(x):
        zero = jnp.zeros_like(x)
        return jnp.concatenate([jnp.where(lane_lo, x, zero), jnp.where(lane_lo, zero, x)], axis=0).astype(BF16)

    carry = {"H": jnp.where(t_out == 0, 0.0, state_ref[...]), "ys": [], "next": 0}

    def advance():
        c = carry["next"]
        if c >= len(chunks):
            return
        H = diag2(carry["H"])
        carry["ys"].append(jnp.dot(rhat_st[prv, cs[c], :], H, preferred_element_type=F32) + y0_st[prv, cs[c], :])
        carry["H"] = jnp.dot(m_st[prv, c], H, preferred_element_type=F32) + n_st[prv, c]
        carry["next"] = c + 1

    row0 = lax.broadcasted_iota(jnp.int32, (8, LANES), 0) == 0
    first = t_in == 0

    def shifted(ref, slot, mu_ref):
        x = ref[...].astype(F32)
        carried = jnp.where(first, 0.0, last_ref[slot, 7:8, :])
        rolled = pltpu.roll(x, 1, axis=0)
        prev = jnp.concatenate([jnp.where(row0, carried, rolled[:8]), rolled[8:]], axis=0)
        last_ref[slot] = x[TT - 8:, :]
        return x + (prev - x) * mu_ref[...]

    r = shifted(r_ref, 0, mu_r_ref)
    k = shifted(k_ref, 1, mu_k_ref)
    v = shifted(v_ref, 2, mu_v_ref)
    wa = shifted(wa_ref, 3, mu_wa_ref)
    wd = wa[:, :LORA]
    ad = wa[:, LORA:]

    u = w0_ref[...] + _dot(jnp.tanh(wd), wup_ref[...])
    lw = -(DECAY_SCALE * LOG2_E) / (1.0 + jnp.exp(-u))
    a = _sigmoid(a0_ref[...] + _dot(ad, aup_ref[...]))
    advance()

    lane_r = lax.broadcasted_iota(jnp.int32, (LANES, LANES), 0)
    lane_c = lax.broadcasted_iota(jnp.int32, (LANES, LANES), 1)
    head_blk = (lane_r // HEAD_DIM) == (lane_c // HEAD_DIM)
    ones_blk = jnp.where(head_blk, 1.0, 0.0).astype(BF16)

    kk = k * kk_ref[...]
    kk = kk * lax.rsqrt(jnp.maximum(_dot(kk * kk, ones_blk), 1e-24))
    k = k * ((1.0 - ka_ref[...]) + a * ka_ref[...])
    av = -kk
    bv = kk * a
    post_st[cur, 0] = _dot(r * k * rk_ref[...], ones_blk) * v
    post_st[cur, 1] = _silu(g_ref[...].astype(F32))

    TB = 4 * L
    tr = lax.broadcasted_iota(jnp.int32, (TB, TB), 0)
    tc = lax.broadcasted_iota(jnp.int32, (TB, TB), 1)
    tri = jnp.where(jnp.logical_and(tr // L == tc // L, tc <= tr), 1.0, 0.0).astype(BF16)
    cum = jnp.concatenate([_dot_exact_lhs(tri, lw[i * TB:(i + 1) * TB]) for i in range(TT // TB)], axis=0)
    advance()

    e_pos = jnp.exp2(cum)
    e_neg = 1.0 / e_pos
    r_t = r * e_pos
    k_t = k * e_neg
    b_t = bv * e_neg
    a_t = av * jnp.exp2(cum - lw)

    row = lax.broadcasted_iota(jnp.int32, (L, LANES), 0)
    col = lax.broadcasted_iota(jnp.int32, (L, LANES), 1) % HEAD_DIM
    strict = col < row
    incl = col <= row
    eye = col == row

    ticks = {"n": 0}

    def each(fn):
        out = []
        for c in chunks:
            out.append(fn(c))
            ticks["n"] += 1
            if ticks["n"] % LINK_PERIOD == 0:
                advance()
        return out

    cum_l = [cum[sl][L - 1:L, :] for sl in cs]
    p_l = [jnp.exp2(cum_l[c]) for c in chunks]
    bk_d = [jnp.concatenate([diag2(b_t[sl]), diag2(k_t[sl])], axis=0) for sl in cs]
    v_d = [diag2(v[sl]) for sl in cs]
    def heads_down(x):
        return jnp.concatenate([x[:, :HEAD_DIM], x[:, HEAD_DIM:]], axis=0)

    gram = each(lambda c: _dot_nt(jnp.concatenate([a_t[cs[c]], r_t[cs[c]]], axis=0), bk_d[c]))
    kv = each(lambda c: _dot_tn(heads_down(k_t[cs[c]] * p_l[c]), v_d[c]))
    a_ab = [jnp.where(strict, gram[c][:L, :LANES], 0.0) for c in chunks]
    a_ak = [jnp.where(strict, gram[c][:L, LANES:], 0.0).astype(BF16) for c in chunks]
    a_rb = [jnp.where(incl, gram[c][L:, :LANES], 0.0).astype(BF16) for c in chunks]
    a_rk = [jnp.where(incl, gram[c][L:, LANES:], 0.0).astype(BF16) for c in chunks]
    inv = [jnp.where(eye, 1.0, 0.0) + a_ab[c] for c in chunks]
    pw = each(lambda c: _dot(a_ab[c], diag2(a_ab[c])).astype(BF16))
    w_s = each(lambda c: _dot(a_ak[c], v_d[c]).astype(BF16))
    for _ in range(4):
        both = each(lambda c: _dot(pw[c], jnp.concatenate([diag2(pw[c]), diag2(inv[c])], axis=1)))
        pw = [both[c][:, :LANES].astype(BF16) for c in chunks]
        inv = [inv[c] + both[c][:, LANES:] for c in chunks]
    last = each(lambda c: _dot(pw[c], diag2(inv[c])))
    inv = [inv[c] + last[c] for c in chunks]
    sol = each(lambda c: _dot(inv[c], jnp.concatenate([diag2(a_t[cs[c]]), diag2(w_s[c])], axis=1)))
    ahat_d = [diag2(sol[c][:, :LANES]) for c in chunks]
    u0_d = [diag2(sol[c][:, LANES:]) for c in chunks]

    bmn = each(lambda c: _dot_tn(heads_down(b_t[cs[c]] * p_l[c]),
                                 jnp.concatenate([ahat_d[c], u0_d[c]], axis=1)))
    rb = each(lambda c: _dot(a_rb[c], ahat_d[c]))
    y0 = each(lambda c: _dot(jnp.concatenate([a_rb[c], a_rk[c]], axis=1),
                             jnp.concatenate([u0_d[c], v_d[c]], axis=0)))
    while carry["next"] < len(chunks):
        advance()

    state_ref[...] = carry["H"]
    y = jnp.concatenate(carry["ys"], axis=0)
    mean_blk = jnp.where(head_blk, 1.0 / HEAD_DIM, 0.0).astype(BF16)
    mean = _dot(y, mean_blk)
    yc = y - mean
    var = _dot(yc * yc, mean_blk)
    yn = yc * lax.rsqrt(var + GN_EPS) * lnw_ref[...] + lnb_ref[...]
    o_ref[...] = ((yn + post_st[prv, 0]) * post_st[prv, 1]).astype(o_ref.dtype)

    for c in chunks:
        m_st[cur, c] = (bmn[c][:, :LANES] + jnp.where(eye, p_l[c], 0.0)).astype(BF16)
        n_st[cur, c] = bmn[c][:, LANES:] + kv[c]
        rhat_st[cur, cs[c], :] = (r_t[cs[c]] + rb[c]).astype(BF16)
        y0_st[cur, cs[c], :] = y0[c]


def _rwkv(proj, mu, w0, w_up, a0, a_up, k_k, k_a, r_k, ln_w, ln_b, batch, seq):
    M = proj.shape[0]
    TT = RWKV_TT
    nt = seq // TT
    n_chunks = TT // CHUNK
    total = batch * PAIRS * nt

    def where(s):
        b, p, t = s // (PAIRS * nt), (s // nt) % PAIRS, s % nt
        return b * nt + t, p

    src = lambda s: where(jnp.minimum(s, total - 1))
    dst = lambda s: where(jnp.maximum(s - 1, 0))

    slab = lambda col: pl.BlockSpec((TT, LANES), lambda s: (src(s)[0], col // LANES + src(s)[1]))
    vec = pl.BlockSpec((1, LANES), lambda s: (0, src(s)[1]))
    vec0 = pl.BlockSpec((1, LANES), lambda s: (0, 0))
    vec_dst = pl.BlockSpec((1, LANES), lambda s: (0, dst(s)[1]))
    lora = pl.BlockSpec((LORA, LANES), lambda s: (0, src(s)[1]))
    mu_r, mu_k, mu_v, mu_wa = (mu[:, :1024], mu[:, 1024:2048], mu[:, 2048:3072], mu[:, 3072:])
    return pl.pallas_call(
        functools.partial(_rwkv_kernel, nt=nt, total=total),
        out_shape=jax.ShapeDtypeStruct((M, RWKV_WIDTH), BF16),
        grid=(total + 1,),
        in_specs=[slab(COL_R), slab(COL_RK), slab(COL_RV), slab(COL_RG),
                  pl.BlockSpec((TT, LANES), lambda s: (src(s)[0], COL_WA // LANES)),
                  vec, vec, vec, vec0,
                  vec, lora, vec, lora, vec, vec, vec, vec_dst, vec_dst],
        out_specs=pl.BlockSpec((TT, LANES), lambda s: dst(s)),
        scratch_shapes=[pltpu.VMEM((CHUNK, LANES), F32),
                        pltpu.VMEM((4, 8, LANES), F32),
                        pltpu.VMEM((2, n_chunks, CHUNK, LANES), BF16),
                        pltpu.VMEM((2, n_chunks, CHUNK, LANES), F32),
                        pltpu.VMEM((2, TT, LANES), BF16),
                        pltpu.VMEM((2, TT, LANES), F32),
                        pltpu.VMEM((2, 2, TT, LANES), F32)],
        compiler_params=pltpu.CompilerParams(dimension_semantics=("arbitrary",),
                                             vmem_limit_bytes=VMEM_LIMIT),
        name="rwkv7_mix",
    )(proj, proj, proj, proj, proj, mu_r, mu_k, mu_v, mu_wa,
      w0, w_up, a0, a_up, k_k, k_a, r_k, ln_w, ln_b)


def _out_proj_kernel(x_ref, ya_ref, yr_ref, wa_ref, wr_ref, g_ref, gate_ref, o_ref):
    mix = (jnp.dot(ya_ref[...], wa_ref[...], preferred_element_type=F32)
           + jnp.dot(yr_ref[...], wr_ref[...], preferred_element_type=F32))
    inv = lax.rsqrt(jnp.mean(mix * mix, axis=-1, keepdims=True) + RMS_EPS)
    o_ref[...] = x_ref[...] + (mix * inv) * (gate_ref[0] * g_ref[...])


def _out_proj(x2, ya, yr, w, g, gate, seq):
    M, D = x2.shape
    tm = 512
    per_b = seq // tm
    half = ya.shape[1]
    return pl.pallas_call(
        _out_proj_kernel,
        out_shape=jax.ShapeDtypeStruct((M, D), F32),
        grid=(M // tm,),
        in_specs=[pl.BlockSpec((tm, D), lambda i: (i, 0)),
                  pl.BlockSpec((tm, half), lambda i: (i, 0)),
                  pl.BlockSpec((tm, half), lambda i: (i, 0)),
                  pl.BlockSpec((half, D), lambda i: (0, 0)),
                  pl.BlockSpec((half, D), lambda i: (1, 0)),
                  pl.BlockSpec((1, D), lambda i: (0, 0)),
                  pl.BlockSpec((1, 1, D), lambda i: (i // per_b, 0, 0))],
        out_specs=pl.BlockSpec((tm, D), lambda i: (i, 0)),
        compiler_params=pltpu.CompilerParams(dimension_semantics=("parallel",),
                                             vmem_limit_bytes=VMEM_LIMIT),
        name="out_proj",
    )(x2, ya, yr, w, w, g, gate)


def kernel(x, c, w_ada, b_ada, pre_norm_g, post_norm_g, w_in, w_out, attn_sinks, rwkv_mu, rwkv_w0,
           rwkv_w_up, rwkv_a0, rwkv_a_up, rwkv_k_k, rwkv_k_a, rwkv_r_k, rwkv_ln_w, rwkv_ln_b):
    B, T, D = x.shape
    depth = w_ada.shape[0]
    x2 = x.reshape(B * T, D)
    for l in range(depth):
        mod = _adaln(c, w_ada[l], b_ada[l][None, :])
        shift, scale, gate = (mod[:, i * D:(i + 1) * D].reshape(B, 1, D) for i in range(3))
        proj = _in_proj(x2, pre_norm_g[l][None, :], scale, shift, w_in[l].astype(BF16), T)
        y_attn = _attention(proj, attn_sinks[l], B, T)
        y_rwkv = _rwkv(proj, rwkv_mu[l][None, :], rwkv_w0[l][None, :], rwkv_w_up[l].astype(BF16),
                       rwkv_a0[l][None, :], rwkv_a_up[l].astype(BF16), rwkv_k_k[l][None, :],
                       rwkv_k_a[l][None, :], rwkv_r_k[l].reshape(1, RWKV_WIDTH),
                       rwkv_ln_w[l][None, :], rwkv_ln_b[l][None, :], B, T)
        x2 = _out_proj(x2, y_attn, y_rwkv, w_out[l].astype(BF16), post_norm_g[l][None, :], gate, T)
    return x2.reshape(B, T, D)
```
